```python
import math
import jax, jax.numpy as jnp
from jax import lax
import numpy as np

D_MODEL = 1024
BATCH = 8
SEQ = 2048
DEPTH = 1

PLE_DIM = 256
D_MIX = D_MODEL
D_LRU = D_MIX // 2
LRU_BLOCKS = 8
LRU_BLOCK_DIM = D_LRU // LRU_BLOCKS
CONV_WIDTH = 4
LRU_C = 8.0
D_ATTN = D_MIX - D_LRU
N_ATT_HEADS = 8
HEAD_DIM = D_ATTN // N_ATT_HEADS
DILATED_GROUPS = ((128, 1), (512, 4), (2048, 16))
ATT_BLOCK = 128
REL_BUCKETS = 32
REL_MAX_DIST = 2048
PEER_HEADS = 8
PEER_NKEYS = 128
PEER_EXPERTS = PEER_NKEYS * PEER_NKEYS
PEER_QDIM = 256
PEER_TOPK = 16
PEER_TOKEN_BLOCK = 128
NORM_EPS = 1e-6
D_IN_PROJ = 2 * D_LRU + 3 * D_ATTN

kernel_name = 'hybrid_rglru_dilated_attn_peer'


def rmsnorm(x, g):
    xf = x.astype(jnp.float32)
    y = xf * lax.rsqrt(jnp.mean(xf * xf, axis=-1, keepdims=True) + NORM_EPS)
    return (y * g.astype(jnp.float32)).astype(x.dtype)


def t5_causal_bucket(dist):
    max_exact = REL_BUCKETS // 2
    n = jnp.maximum(dist, 0)
    nf = jnp.maximum(n, 1).astype(jnp.float32)
    large = max_exact + (jnp.log(nf / max_exact) / math.log(REL_MAX_DIST / max_exact)
                         * (REL_BUCKETS - max_exact)).astype(jnp.int32)
    large = jnp.minimum(large, REL_BUCKETS - 1)
    return jnp.where(n < max_exact, n, large)


def causal_depthwise_conv(x, w, b):
    y = lax.conv_general_dilated(
        x, w[:, None, :], window_strides=(1,), padding=[(CONV_WIDTH - 1, 0)],
        dimension_numbers=('NWC', 'WIO', 'NWC'), feature_group_count=x.shape[-1])
    return y + b


def rg_lru(x, w_rg, b_rg, w_ig, b_ig, lam):
    B, T, C = x.shape
    xf = x.astype(jnp.float32)
    xb = xf.reshape(B, T, LRU_BLOCKS, LRU_BLOCK_DIM)
    r = jax.nn.sigmoid(jnp.einsum('btgi,gij->btgj', xb, w_rg.astype(jnp.float32)).reshape(B, T, C)
                       + b_rg.astype(jnp.float32))
    i = jax.nn.sigmoid(jnp.einsum('btgi,gij->btgj', xb, w_ig.astype(jnp.float32)).reshape(B, T, C)
                       + b_ig.astype(jnp.float32))
    log_a = -LRU_C * r * jax.nn.softplus(-lam.astype(jnp.float32))
    a = jnp.exp(log_a)
    bterm = jnp.sqrt(-jnp.expm1(2.0 * log_a)) * (i * xf)

    def combine(c1, c2):
        a1, b1 = c1
        a2, b2 = c2
        return a1 * a2, a2 * b1 + b2

    _, h = lax.associative_scan(combine, (a, bterm), axis=1)
    return h.astype(x.dtype)


def dilated_branch(q, k, v, rel_bias, window, dilation):
    B, T, H, Dh = q.shape
    L = T // dilation
    W = window // dilation
    nb = -(-L // ATT_BLOCK)
    Lp = nb * ATT_BLOCK
    n_prev = -(-W // ATT_BLOCK)
    band = (n_prev + 1) * ATT_BLOCK

    def to_sub(t):
        return t.reshape(B, L, dilation, H, Dh).transpose(0, 2, 3, 1, 4)

    qs = jnp.pad(to_sub(q), ((0, 0), (0, 0), (0, 0), (0, Lp - L), (0, 0)))
    qb = qs.reshape(B, dilation, H, nb, ATT_BLOCK, Dh)
    pad_kv = ((0, 0), (0, 0), (0, 0), (n_prev * ATT_BLOCK, Lp - L), (0, 0))

    def bands(t):
        tb = jnp.pad(to_sub(t), pad_kv).reshape(B, dilation, H, nb + n_prev, ATT_BLOCK, Dh)
        return jnp.concatenate([tb[:, :, :, j:j + nb] for j in range(n_prev + 1)], axis=4)

    kb = bands(k)
    vb = bands(v)
    qi = jnp.arange(ATT_BLOCK)[:, None]
    kj = jnp.arange(band)[None, :]
    dist_sub = qi + n_prev * ATT_BLOCK - kj
    key_sub = (jnp.arange(nb)[:, None, None] - n_prev) * ATT_BLOCK + kj[None]
    mask = (dist_sub >= 0) & (dist_sub <= W) & (key_sub >= 0)
    bias = rel_bias[t5_causal_bucket(dist_sub * dilation)].transpose(2, 0, 1)

    s = jnp.einsum('bdhnqc,bdhnkc->bdhnqk', qb, kb) + bias[None, None, :, None].astype(jnp.float32)
    s = jnp.where(mask[None, None, None], s, -jnp.inf)
    m = jnp.max(s, axis=-1, keepdims=True)
    e = jnp.exp(s - m)
    den = jnp.sum(e, axis=-1)
    o = jnp.einsum('bdhnqk,bdhnkc->bdhnqc', e, vb) / den[..., None]
    lse = m[..., 0] + jnp.log(den)
    o = o.reshape(B, dilation, H, Lp, Dh)[:, :, :, :L].transpose(0, 3, 1, 2, 4).reshape(B, T, H, Dh)
    lse = lse.reshape(B, dilation, H, Lp)[..., :L].transpose(0, 3, 1, 2).reshape(B, T, H)
    return o, lse


def dilated_attention(q, k, v, rel_bias):
    outs = []
    lses = []
    for window, dilation in DILATED_GROUPS:
        o, lse = dilated_branch(q, k, v, rel_bias, window, dilation)
        outs.append(o)
        lses.append(lse)
    wts = jax.nn.softmax(jnp.stack(lses, axis=0), axis=0)
    return jnp.sum(wts[..., None] * jnp.stack(outs, axis=0), axis=0)


def hybrid_mixer(u, w_in, conv_w, conv_b, w_rg, b_rg, w_ig, b_ig, lru_lambda,
                 g_lru_out, g_attn_out, w_out, rel_bias):
    B, T, _ = u.shape
    z = u @ w_in
    x_lru = z[..., :D_LRU]
    gate = z[..., D_LRU:2 * D_LRU]
    o0 = 2 * D_LRU
    q = z[..., o0:o0 + D_ATTN]
    k = z[..., o0 + D_ATTN:o0 + 2 * D_ATTN]
    v = z[..., o0 + 2 * D_ATTN:]
    xc = causal_depthwise_conv(x_lru, conv_w, conv_b)
    y_lru = rg_lru(xc, w_rg, b_rg, w_ig, b_ig, lru_lambda) * jax.nn.gelu(gate)
    qh = q.reshape(B, T, N_ATT_HEADS, HEAD_DIM).astype(jnp.float32) * (HEAD_DIM ** -0.5)
    kh = k.reshape(B, T, N_ATT_HEADS, HEAD_DIM).astype(jnp.float32)
    vh = v.reshape(B, T, N_ATT_HEADS, HEAD_DIM).astype(jnp.float32)
    y_att = dilated_attention(qh, kh, vh, rel_bias).reshape(B, T, D_ATTN).astype(u.dtype)
    y = jnp.concatenate([rmsnorm(y_lru, g_lru_out), rmsnorm(y_att, g_attn_out)], axis=-1)
    return y @ w_out


def peer_ffn(x, w_pq, peer_k1, peer_k2, peer_u, peer_v):
    B, T, D = x.shape
    N = B * T
    xt = x.reshape(N, D)
    half = PEER_QDIM // 2
    q = (xt @ w_pq).astype(jnp.float32).reshape(N, PEER_HEADS, PEER_QDIM)
    s1 = jnp.einsum('nhc,kc->nhk', q[..., :half], peer_k1.astype(jnp.float32))
    s2 = jnp.einsum('nhc,kc->nhk', q[..., half:], peer_k2.astype(jnp.float32))
    t1, i1 = lax.top_k(s1, PEER_TOPK)
    t2, i2 = lax.top_k(s2, PEER_TOPK)
    cand_s = (t1[..., :, None] + t2[..., None, :]).reshape(N, PEER_HEADS, PEER_TOPK * PEER_TOPK)
    cand_id = (i1[..., :, None] * PEER_NKEYS + i2[..., None, :]).reshape(N, PEER_HEADS, PEER_TOPK * PEER_TOPK)
    top_s, top_pos = lax.top_k(cand_s, PEER_TOPK)
    ids = jnp.take_along_axis(cand_id, top_pos, axis=-1)
    g = jax.nn.softmax(top_s, axis=-1).astype(x.dtype)
    nblk = N // PEER_TOKEN_BLOCK

    def expert_block(args):
        xb, idb, gb = args
        act = jax.nn.gelu(jnp.einsum('td,thkd->thk', xb, peer_u[idb])) * gb
        return jnp.einsum('thk,thkd->td', act, peer_v[idb])

    y = lax.map(expert_block, (xt.reshape(nblk, PEER_TOKEN_BLOCK, D),
                               ids.reshape(nblk, PEER_TOKEN_BLOCK, PEER_HEADS, PEER_TOPK),
                               g.reshape(nblk, PEER_TOKEN_BLOCK, PEER_HEADS, PEER_TOPK)))
    return y.reshape(B, T, D)


def setup_inputs(seed: int = 0) -> dict:
    key = jax.random.key(seed)
    ks = jax.random.split(key, 26)
    f32 = jnp.float32

    def nrm(k, shape, scale):
        return jax.random.normal(k, shape, f32) * scale

    def gain(k, shape):
        return 1.0 + 0.02 * jax.random.normal(k, shape, f32)

    u = jax.random.uniform(ks[10], (DEPTH, D_LRU), f32, 0.9, 0.999)
    a0 = u ** (1.0 / LRU_C)
    lru_lambda = jnp.log(a0) - jnp.log1p(-a0)
    return {
        'x': jax.random.normal(ks[0], (BATCH, SEQ, D_MODEL), f32),
        'p': jax.random.normal(ks[1], (DEPTH, BATCH, SEQ, PLE_DIM), f32),
        'g_mix': gain(ks[2], (DEPTH, D_MODEL)),
        'w_in': nrm(ks[3], (DEPTH, D_MODEL, D_IN_PROJ), D_MODEL ** -0.5),
        'conv_w': nrm(ks[4], (DEPTH, CONV_WIDTH, D_LRU), CONV_WIDTH ** -0.5),
        'conv_b': nrm(ks[5], (DEPTH, D_LRU), 0.02),
        'w_rg': nrm(ks[6], (DEPTH, LRU_BLOCKS, LRU_BLOCK_DIM, LRU_BLOCK_DIM), LRU_BLOCK_DIM ** -0.5),
        'b_rg': nrm(ks[7], (DEPTH, D_LRU), 0.02),
        'w_ig': nrm(ks[8], (DEPTH, LRU_BLOCKS, LRU_BLOCK_DIM, LRU_BLOCK_DIM), LRU_BLOCK_DIM ** -0.5),
        'b_ig': nrm(ks[9], (DEPTH, D_LRU), 0.02),
        'lru_lambda': lru_lambda,
        'g_lru_out': gain(ks[11], (DEPTH, D_LRU)),
        'g_attn_out': gain(ks[12], (DEPTH, D_ATTN)),
        'w_out': nrm(ks[13], (DEPTH, D_MIX, D_MODEL), D_MIX ** -0.5),
        'rel_bias': nrm(ks[14], (REL_BUCKETS, N_ATT_HEADS), 0.5),
        'g_ffn': gain(ks[15], (DEPTH, D_MODEL)),
        'peer_wq': nrm(ks[16], (DEPTH, D_MODEL, PEER_HEADS * PEER_QDIM), D_MODEL ** -0.5),
        'peer_k1': nrm(ks[17], (DEPTH, PEER_NKEYS, PEER_QDIM // 2), (PEER_QDIM // 2) ** -0.5),
        'peer_k2': nrm(ks[18], (DEPTH, PEER_NKEYS, PEER_QDIM // 2), (PEER_QDIM // 2) ** -0.5),
        'peer_u': nrm(ks[19], (DEPTH, PEER_EXPERTS, D_MODEL), D_MODEL ** -0.5),
        'peer_v': nrm(ks[20], (DEPTH, PEER_EXPERTS, D_MODEL), PEER_HEADS ** -0.5),
        'g_ple': gain(ks[21], (DEPTH, D_MODEL)),
        'w_ple_gate': nrm(ks[22], (DEPTH, D_MODEL, D_MODEL), D_MODEL ** -0.5),
        'w_ple_proj': nrm(ks[23], (DEPTH, PLE_DIM, D_MODEL), PLE_DIM ** -0.5),
        'g_final': gain(ks[24], (D_MODEL,)),
    }


def reference(x, p, g_mix, w_in, conv_w, conv_b, w_rg, b_rg, w_ig, b_ig, lru_lambda,
              g_lru_out, g_attn_out, w_out, rel_bias, g_ffn, peer_wq, peer_k1, peer_k2,
              peer_u, peer_v, g_ple, w_ple_gate, w_ple_proj, g_final):
    h = x
    for i in range(DEPTH):
        h = h + hybrid_mixer(rmsnorm(h, g_mix[i]), w_in[i], conv_w[i], conv_b[i], w_rg[i], b_rg[i],
                             w_ig[i], b_ig[i], lru_lambda[i], g_lru_out[i], g_attn_out[i], w_out[i],
                             rel_bias)
        h = h + peer_ffn(rmsnorm(h, g_ffn[i]), peer_wq[i], peer_k1[i], peer_k2[i], peer_u[i], peer_v[i])
        h = h + jax.nn.sigmoid(rmsnorm(h, g_ple[i]) @ w_ple_gate[i]) * (p[i] @ w_ple_proj[i])
    return rmsnorm(h, g_final)
```

```python
import functools
import math

import jax
import jax.numpy as jnp
import numpy as np
from jax import lax
from jax.experimental import pallas as pl
from jax.experimental.pallas import tpu as pltpu

F32 = jnp.float32
BF16 = jnp.bfloat16
I32 = jnp.int32

D_MODEL = 1024
PLE_DIM = 256
D_LRU = 512
LRU_BLOCKS = 8
CONV_WIDTH = 4
LRU_C = 8.0
D_ATTN = 512
N_ATT_HEADS = 8
HEAD_DIM = 64
DILATED_GROUPS = ((128, 1), (512, 4), (2048, 16))
ATT_BLOCK = 128
REL_BUCKETS = 32
REL_MAX_DIST = 2048
PEER_HEADS = 8
PEER_NKEYS = 128
PEER_QDIM = 256
PEER_TOPK = 16
NORM_EPS = 1e-6

LANES = 128
SUBLANES = 8
NEG_BIG = -1e30
VMEM_LIMIT = 56 * 1024 * 1024


def _cparams(sem):
    return pltpu.CompilerParams(dimension_semantics=sem, vmem_limit_bytes=VMEM_LIMIT)


def _rms(x, g):
    ms = jnp.mean(x * x, axis=-1, keepdims=True)
    return (x * lax.rsqrt(ms + NORM_EPS)) * g


def _gelu(x):
    c = math.sqrt(2.0 / math.pi)
    return x * (0.5 * (1.0 + jnp.tanh(c * (x + 0.044715 * (x * x * x)))))


def _inproj_kernel(x_ref, g_ref, w_ref, xl_ref, gate_ref, q_ref, k_ref, v_ref):
    u = _rms(x_ref[...], g_ref[...]).astype(BF16)
    outs = (xl_ref, gate_ref, q_ref, k_ref, v_ref)
    for i, o_ref in enumerate(outs):
        z = jnp.dot(u, w_ref[:, i * 512:(i + 1) * 512], preferred_element_type=F32)
        if o_ref is q_ref:
            z = z * (HEAD_DIM ** -0.5)
        o_ref[...] = z


def _inproj(x2, g_mix, w_in_b, tm=512):
    n = x2.shape[0]
    blk = lambda: pl.BlockSpec((tm, 512), lambda i: (i, 0))
    return pl.pallas_call(
        _inproj_kernel,
        grid=(n // tm,),
        in_specs=[pl.BlockSpec((tm, D_MODEL), lambda i: (i, 0)),
                  pl.BlockSpec((1, D_MODEL), lambda i: (0, 0)),
                  pl.BlockSpec((D_MODEL, 2560), lambda i: (0, 0))],
        out_specs=[blk() for _ in range(5)],
        out_shape=[jax.ShapeDtypeStruct((n, 512), F32) for _ in range(5)],
        compiler_params=_cparams(("parallel",)),
        name="inproj",
    )(x2, g_mix, w_in_b)


LRU_CHUNK = 64


def _lru_kernel(xl_ref, gate_ref, cw_ref, cb_ref, wrg_ref, brg_ref, wig_ref, big_ref,
                lam_ref, g_ref, o_ref, xp_ref, h_ref):
    T = xl_ref.shape[0]
    tc = LRU_CHUNK
    xp_ref[0:SUBLANES, :] = jnp.zeros((SUBLANES, D_LRU), F32)
    xp_ref[SUBLANES:SUBLANES + T, :] = xl_ref[...]
    h_ref[...] = jnp.zeros_like(h_ref)
    nlam = -lam_ref[...]
    sp = jnp.maximum(nlam, 0.0) + jnp.log1p(jnp.exp(-jnp.abs(nlam)))
    row = lax.broadcasted_iota(I32, (tc, D_LRU), 0)

    def chunk(c, carry):
        t0 = pl.multiple_of(c * tc, tc)
        xa = xp_ref[pl.ds(t0, tc + SUBLANES), :]
        xc = cb_ref[...] + cw_ref[CONV_WIDTH - 1:CONV_WIDTH, :] * xa[SUBLANES:, :]
        for j in range(CONV_WIDTH - 1):
            sh = pltpu.roll(xa, CONV_WIDTH - 1 - j, axis=0)[SUBLANES:, :]
            xc = xc + cw_ref[j:j + 1, :] * sh
        xcb = xc.astype(BF16)
        r = jax.nn.sigmoid(jnp.dot(xcb, wrg_ref[...], preferred_element_type=F32) + brg_ref[...])
        ig = jax.nn.sigmoid(jnp.dot(xcb, wig_ref[...], preferred_element_type=F32) + big_ref[...])
        log_a = (-LRU_C) * r * sp
        a = jnp.exp(log_a)
        b = jnp.sqrt(-jnp.tanh(log_a) * (a * a + 1.0)) * (ig * xc)
        d = 1
        while d < tc:
            keep = row >= d
            a_sh = jnp.where(keep, pltpu.roll(a, d, axis=0), 1.0)
            b_sh = jnp.where(keep, pltpu.roll(b, d, axis=0), 0.0)
            b = a * b_sh + b
            a = a * a_sh
            d *= 2
        h = b + a * h_ref[...]
        h_ref[...] = h[tc - 1:tc, :]
        y = h * _gelu(gate_ref[pl.ds(t0, tc), :])
        o_ref[pl.ds(t0, tc), :] = _rms(y, g_ref[...]).astype(BF16)
        return carry

    lax.fori_loop(0, T // tc, chunk, 0)


def _lru(xl, gate, conv_w, conv_b, wrg_bd, b_rg, wig_bd, b_ig, lam, g_lru, batch, seq):
    vec = lambda: pl.BlockSpec((1, D_LRU), lambda b: (0, 0))
    mat = lambda: pl.BlockSpec((D_LRU, D_LRU), lambda b: (0, 0))
    return pl.pallas_call(
        _lru_kernel,
        grid=(batch,),
        in_specs=[pl.BlockSpec((seq, D_LRU), lambda b: (b, 0)),
                  pl.BlockSpec((seq, D_LRU), lambda b: (b, 0)),
                  pl.BlockSpec((CONV_WIDTH, D_LRU), lambda b: (0, 0)),
                  vec(), mat(), vec(), mat(), vec(), vec(), vec()],
        out_specs=pl.BlockSpec((seq, D_LRU), lambda b: (b, 0)),
        out_shape=jax.ShapeDtypeStruct((batch * seq, D_LRU), BF16),
        scratch_shapes=[pltpu.VMEM((seq + SUBLANES, D_LRU), F32),
                        pltpu.VMEM((1, D_LRU), F32)],
        compiler_params=_cparams(("parallel",)),
        name="rglru",
    )(xl, gate, conv_w, conv_b, wrg_bd, b_rg, wig_bd, b_ig, lam, g_lru)


def _attend(qf, kf, vf, bias_a, bias_b, is_a):
    kb = kf.astype(BF16)
    vb = vf.astype(BF16)
    dn = (((1,), (1,)), ((), ()))
    outs = []
    for q_h, bias in ((jnp.where(is_a, qf, 0.0), bias_a), (jnp.where(is_a, 0.0, qf), bias_b)):
        s = lax.dot_general(q_h.astype(BF16), kb, dn, preferred_element_type=F32) + bias
        m = jnp.max(s, axis=-1, keepdims=True)
        e = jnp.exp(s - m)
        den = jnp.sum(e, axis=-1, keepdims=True)
        num = jnp.dot(e.astype(BF16), vb, preferred_element_type=F32)
        outs.append((num, m, den))
    (na, ma, da), (nb, mb, db) = outs
    shape = na.shape
    return (jnp.where(is_a, na, nb),
            jnp.where(is_a, jnp.broadcast_to(ma, shape), jnp.broadcast_to(mb, shape)),
            jnp.where(is_a, jnp.broadcast_to(da, shape), jnp.broadcast_to(db, shape)))


def _attn_kernel(q_ref, k_ref, v_ref, bt_ref, o_ref, num_ref, m_ref, den_ref):
    T = q_ref.shape[0]
    blk = ATT_BLOCK
    is_a = lax.broadcasted_iota(I32, (blk, LANES), 1) < HEAD_DIM

    def rows(start, size, d):
        return pl.ds(start, size) if d == 1 else pl.ds(start, size, stride=d)

    def first_block(br, d, r):
        sl = rows(r, blk, d)
        out = _attend(q_ref[sl, :], k_ref[sl, :], v_ref[sl, :],
                      bt_ref[br, 0, :, blk:], bt_ref[br, 1, :, blk:], is_a)
        for ref, val in zip((num_ref, m_ref, den_ref), out):
            ref[br, sl, :] = val

    def later_block(br, d, r, n):
        qs = rows(r + n * (blk * d), blk, d)
        ks = rows(r + (n - 1) * (blk * d), 2 * blk, d)
        out = _attend(q_ref[qs, :], k_ref[ks, :], v_ref[ks, :],
                      bt_ref[br, 0], bt_ref[br, 1], is_a)
        for ref, val in zip((num_ref, m_ref, den_ref), out):
            ref[br, qs, :] = val

    for br, (_, d) in enumerate(DILATED_GROUPS):
        nb = T // (d * blk)

        def per_residue(r, carry, br=br, d=d, nb=nb):
            first_block(br, d, r)
            if nb > 1:
                def per_block(n, c2):
                    later_block(br, d, r, n)
                    return c2
                lax.fori_loop(1, nb, per_block, 0)
            return carry

        if d == 1:
            per_residue(0, 0)
        else:
            lax.fori_loop(0, d, per_residue, 0)

    def merge(c, carry):
        sl = pl.ds(pl.multiple_of(c * blk, blk), blk)
        m0, m1, m2 = m_ref[0, sl, :], m_ref[1, sl, :], m_ref[2, sl, :]
        mm = jnp.maximum(jnp.maximum(m0, m1), m2)
        w0, w1, w2 = jnp.exp(m0 - mm), jnp.exp(m1 - mm), jnp.exp(m2 - mm)
        num = w0 * num_ref[0, sl, :] + w1 * num_ref[1, sl, :] + w2 * num_ref[2, sl, :]
        den = w0 * den_ref[0, sl, :] + w1 * den_ref[1, sl, :] + w2 * den_ref[2, sl, :]
        o_ref[sl, :] = num / den
        return carry

    lax.fori_loop(0, T // blk, merge, 0)


def _attention(q, k, v, btab, batch, seq):
    npairs = D_ATTN // LANES
    qkv = lambda: pl.BlockSpec((seq, LANES), lambda b, p: (b, p))
    return pl.pallas_call(
        _attn_kernel,
        grid=(batch, npairs),
        in_specs=[qkv(), qkv(), qkv(),
                  pl.BlockSpec((3, 2, ATT_BLOCK, 2 * ATT_BLOCK), lambda b, p: (0, p, 0, 0))],
        out_specs=pl.BlockSpec((seq, LANES), lambda b, p: (b, p)),
        out_shape=jax.ShapeDtypeStruct((batch * seq, D_ATTN), F32),
        scratch_shapes=[pltpu.VMEM((3, seq, LANES), F32) for _ in range(3)],
        compiler_params=_cparams(("parallel", "parallel")),
        name="dilated_attn",
    )(q, k, v, btab)


def _bias_tables(rel_bias):
    tabs = []
    for window, d in DILATED_GROUPS:
        w_sub = window // d
        qi = jnp.arange(ATT_BLOCK)[:, None]
        kj = jnp.arange(2 * ATT_BLOCK)[None, :]
        dist = qi + ATT_BLOCK - kj
        max_exact = REL_BUCKETS // 2
        nn = jnp.maximum(dist * d, 0)
        nf = jnp.maximum(nn, 1).astype(F32)
        large = max_exact + (jnp.log(nf / max_exact) / math.log(REL_MAX_DIST / max_exact)
                             * (REL_BUCKETS - max_exact)).astype(I32)
        large = jnp.minimum(large, REL_BUCKETS - 1)
        bucket = jnp.where(nn < max_exact, nn, large)
        bias = rel_bias[bucket].transpose(2, 0, 1).astype(F32)
        ok = (dist >= 0) & (dist <= w_sub)
        tabs.append(jnp.where(ok[None], bias, NEG_BIG))
    return jnp.stack(tabs, axis=0)


def _outproj_kernel(yl_ref, ya_ref, x_ref, ga_ref, w_ref, gf_ref, h_ref, xn_ref):
    ya = _rms(ya_ref[...], ga_ref[...]).astype(BF16)
    acc = jnp.dot(yl_ref[...], w_ref[0:D_LRU, :], preferred_element_type=F32)
    acc = acc + jnp.dot(ya, w_ref[D_LRU:, :], preferred_element_type=F32)
    h = x_ref[...] + acc
    h_ref[...] = h
    xn_ref[...] = _rms(h, gf_ref[...]).astype(BF16)


def _outproj(y_lru, y_att, x2, g_att, w_out_b, g_ffn, tm=512):
    n = x2.shape[0]
    return pl.pallas_call(
        _outproj_kernel,
        grid=(n // tm,),
        in_specs=[pl.BlockSpec((tm, D_LRU), lambda i: (i, 0)),
                  pl.BlockSpec((tm, D_ATTN), lambda i: (i, 0)),
                  pl.BlockSpec((tm, D_MODEL), lambda i: (i, 0)),
                  pl.BlockSpec((1, D_ATTN), lambda i: (0, 0)),
                  pl.BlockSpec((D_MODEL, D_MODEL), lambda i: (0, 0)),
                  pl.BlockSpec((1, D_MODEL), lambda i: (0, 0))],
        out_specs=[pl.BlockSpec((tm, D_MODEL), lambda i: (i, 0)),
                   pl.BlockSpec((tm, D_MODEL), lambda i: (i, 0))],
        out_shape=[jax.ShapeDtypeStruct((n, D_MODEL), F32),
                   jax.ShapeDtypeStruct((n, D_MODEL), BF16)],
        compiler_params=_cparams(("parallel",)),
        name="outproj",
    )(y_lru, y_att, x2, g_att, w_out_b, g_ffn)


ROUTE_TM = 128
_CAND_GROUPS = (("col", 0, 0, 8), ("col", 0, 8, 16), ("row", 0, 1, 8), ("row", 0, 8, 16),
                ("col", 1, 1, 8), ("row", 1, 2, 8), ("col", 2, 2, 5), ("row", 2, 3, 5),
                ("col", 3, 3, 4))


def _extract_topk(s, ids, val_ref, id_ref, big):
    def rnd(k, s):
        m = jnp.max(s, axis=0, keepdims=True)
        sel = jnp.min(jnp.where(s == m, ids, big), axis=0, keepdims=True)
        val_ref[pl.ds(k, 1), :] = m
        id_ref[pl.ds(k, 1), :] = sel
        return jnp.where(ids == sel, NEG_BIG, s)

    lax.fori_loop(0, PEER_TOPK, rnd, s)


def _route_kernel(xn_ref, wq_ref, k1_ref, k2_ref, a_ref, b_ref, w_ref,
                  q_ref, t1_ref, i1_ref, t2_ref, i2_ref, ts_ref, tp_ref, oa_ref, ob_ref, ow_ref):
    tm = xn_ref.shape[0]
    half = PEER_QDIM // 2
    q_ref[...] = jnp.dot(xn_ref[...], wq_ref[...], preferred_element_type=F32)
    key_ids = lax.broadcasted_iota(I32, (PEER_NKEYS, tm), 0).astype(F32)
    sub = lax.broadcasted_iota(I32, (SUBLANES, tm), 0)
    dn = (((1,), (1,)), ((), ()))

    def head(h, carry):
        c0 = pl.multiple_of(h * PEER_QDIM, half)
        c1 = pl.multiple_of(h * PEER_QDIM + half, half)
        q1 = q_ref[:, pl.ds(c0, half)].astype(BF16)
        q2 = q_ref[:, pl.ds(c1, half)].astype(BF16)
        s1 = lax.dot_general(k1_ref[...], q1, dn, preferred_element_type=F32)
        s2 = lax.dot_general(k2_ref[...], q2, dn, preferred_element_type=F32)
        _extract_topk(s1, key_ids, t1_ref, i1_ref, float(PEER_NKEYS))
        _extract_topk(s2, key_ids, t2_ref, i2_ref, float(PEER_NKEYS))
        cands, poss = [], []
        for kind, fixed, lo, hi in _CAND_GROUPS:
            base = (lo // SUBLANES) * SUBLANES
            rng = sub + base
            valid = (rng >= lo) & (rng < hi)
            if kind == "col":
                val = t1_ref[base:base + SUBLANES, :] + t2_ref[fixed:fixed + 1, :]
                pos = rng * PEER_TOPK + fixed
            else:
                val = t1_ref[fixed:fixed + 1, :] + t2_ref[base:base + SUBLANES, :]
                pos = fixed * PEER_TOPK + rng
            cands.append(jnp.where(valid, val, NEG_BIG))
            poss.append(pos.astype(F32))
        cand = jnp.concatenate(cands, axis=0)
        pos = jnp.concatenate(poss, axis=0)
        _extract_topk(cand, pos, ts_ref, tp_ref, float(PEER_TOPK * PEER_TOPK))
        top_s = ts_ref[...]
        top_p = tp_ref[...].astype(I32)
        ri = lax.shift_right_logical(top_p, 4)
        rj = lax.bitwise_and(top_p, PEER_TOPK - 1)
        a_sel = jnp.zeros((PEER_TOPK, tm), F32)
        b_sel = jnp.zeros((PEER_TOPK, tm), F32)
        for r in range(PEER_TOPK):
            a_sel = jnp.where(ri == r, i1_ref[r:r + 1, :], a_sel)
            b_sel = jnp.where(rj == r, i2_ref[r:r + 1, :], b_sel)
        e = jnp.exp(top_s - top_s[0:1, :])
        g = e / jnp.sum(e, axis=0, keepdims=True)
        r0 = pl.multiple_of(h * PEER_TOPK, PEER_TOPK)
        oa_ref[pl.ds(r0, PEER_TOPK), :] = a_sel
        ob_ref[pl.ds(r0, PEER_TOPK), :] = b_sel
        ow_ref[pl.ds(r0, PEER_TOPK), :] = g
        return carry

    lax.fori_loop(0, PEER_HEADS, head, 0)
    a_ref[...] = oa_ref[...].T.astype(I32)
    b_ref[...] = ob_ref[...].T.astype(I32)
    w_ref[...] = ow_ref[...].T


def _route(xn, wq_b, k1_b, k2_b):
    n = xn.shape[0]
    tm = ROUTE_TM
    nslot = PEER_HEADS * PEER_TOPK
    out = lambda: pl.BlockSpec((tm, nslot), lambda i: (i, 0))
    small = lambda: pltpu.VMEM((PEER_TOPK, tm), F32)
    return pl.pallas_call(
        _route_kernel,
        grid=(n // tm,),
        in_specs=[pl.BlockSpec((tm, D_MODEL), lambda i: (i, 0)),
                  pl.BlockSpec((D_MODEL, PEER_HEADS * PEER_QDIM), lambda i: (0, 0)),
                  pl.BlockSpec((PEER_NKEYS, PEER_QDIM // 2), lambda i: (0, 0)),
                  pl.BlockSpec((PEER_NKEYS, PEER_QDIM // 2), lambda i: (0, 0))],
        out_specs=[out(), out(), out()],
        out_shape=[jax.ShapeDtypeStruct((n, nslot), I32),
                   jax.ShapeDtypeStruct((n, nslot), I32),
                   jax.ShapeDtypeStruct((n, nslot), F32)],
        scratch_shapes=[pltpu.VMEM((tm, PEER_HEADS * PEER_QDIM), F32),
                        small(), small(), small(), small(), small(), small(),
                        pltpu.VMEM((nslot, tm), F32), pltpu.VMEM((nslot, tm), F32),
                        pltpu.VMEM((nslot, tm), F32)],
        compiler_params=_cparams(("parallel",)),
        name="peer_route",
    )(xn, wq_b, k1_b, k2_b)


EXP_CHUNK = 2048
PAIR = 2 * PEER_NKEYS


def _expu_kernel(xn_ref, ut_ref, a_ref, b_ref, z_ref, as_ref):
    tm = xn_ref.shape[0]
    j = pl.program_id(1)

    @pl.when(j == 0)
    def _():
        z_ref[...] = jnp.zeros_like(z_ref)

    for pp in range(EXP_CHUNK // PAIR):
        as_ref[...] = jnp.dot(xn_ref[...], ut_ref[:, pp * PAIR:(pp + 1) * PAIR],
                              preferred_element_type=F32)
        a0 = j * (EXP_CHUNK // PEER_NKEYS) + 2 * pp

        def grp(g, carry):
            rs = pl.ds(pl.multiple_of(g * SUBLANES, SUBLANES), SUBLANES)
            bi = b_ref[rs, :]
            ai = a_ref[rs, :]
            g0 = jnp.take_along_axis(as_ref[rs, 0:PEER_NKEYS], bi, axis=1, mode="promise_in_bounds")
            g1 = jnp.take_along_axis(as_ref[rs, PEER_NKEYS:PAIR], bi, axis=1, mode="promise_in_bounds")
            z_ref[rs, :] = (z_ref[rs, :] + jnp.where(ai == a0, g0, 0.0)
                            + jnp.where(ai == a0 + 1, g1, 0.0))
            return carry

        lax.fori_loop(0, tm // SUBLANES, grp, 0, unroll=4)


def _expert_u(xn, ut_b, aidx, bidx, tm=512):
    n = xn.shape[0]
    nexp = ut_b.shape[1]
    nslot = PEER_HEADS * PEER_TOPK
    return pl.pallas_call(
        _expu_kernel,
        grid=(n // tm, nexp // EXP_CHUNK),
        in_specs=[pl.BlockSpec((tm, D_MODEL), lambda i, j: (i, 0)),
                  pl.BlockSpec((D_MODEL, EXP_CHUNK), lambda i, j: (0, j)),
                  pl.BlockSpec((tm, nslot), lambda i, j: (i, 0)),
                  pl.BlockSpec((tm, nslot), lambda i, j: (i, 0))],
        out_specs=pl.BlockSpec((tm, nslot), lambda i, j: (i, 0)),
        out_shape=jax.ShapeDtypeStruct((n, nslot), F32),
        scratch_shapes=[pltpu.VMEM((tm, PAIR), F32)],
        compiler_params=_cparams(("parallel", "arbitrary")),
        name="peer_expert_u",
    )(xn, ut_b, aidx, bidx)


def _expv_kernel(z_ref, w_ref, a_ref, b_ref, v_ref, h_ref, o_ref, s_ref, acc_ref, c_ref):
    tm = z_ref.shape[0]
    nk = PEER_NKEYS
    j = pl.program_id(1)

    @pl.when(j == 0)
    def _():
        acc_ref[...] = jnp.zeros_like(acc_ref)
        c_ref[...] = w_ref[...] * _gelu(z_ref[...])
        sub = lax.broadcasted_iota(I32, (nk, nk), 0)
        dn = (((1,), (1,)), ((), ()))

        def tok(n, carry):
            arow = a_ref[pl.ds(n, 1), :]
            brow = b_ref[pl.ds(n, 1), :]
            crow = c_ref[pl.ds(n, 1), :]
            oat = jnp.where(arow == sub, 1.0, 0.0).astype(BF16)
            wobt = jnp.where(brow == sub, crow, 0.0).astype(BF16)
            s_ref[pl.ds(pl.multiple_of(n * nk, nk), nk), :] = lax.dot_general(
                oat, wobt, dn, preferred_element_type=F32)
            return carry

        lax.fori_loop(0, tm, tok, 0, unroll=2)

    for pp in range(EXP_CHUNK // PAIR):
        a0 = j * (EXP_CHUNK // nk) + 2 * pp
        lhs = jnp.concatenate([s_ref[pl.ds(a0, tm, stride=nk), :],
                               s_ref[pl.ds(a0 + 1, tm, stride=nk), :]], axis=1).astype(BF16)
        acc_ref[...] += jnp.dot(lhs, v_ref[pp * PAIR:(pp + 1) * PAIR, :],
                                preferred_element_type=F32)

    @pl.when(j == pl.num_programs(1) - 1)
    def _():
        o_ref[...] = h_ref[...] + acc_ref[...]


def _expert_v(zsel, w, aidx, bidx, v_b, h1, tm=256):
    n = zsel.shape[0]
    nexp = v_b.shape[0]
    nslot = PEER_HEADS * PEER_TOPK
    slot = lambda: pl.BlockSpec((tm, nslot), lambda i, j: (i, 0))
    return pl.pallas_call(
        _expv_kernel,
        grid=(n // tm, nexp // EXP_CHUNK),
        in_specs=[slot(), slot(), slot(), slot(),
                  pl.BlockSpec((EXP_CHUNK, D_MODEL), lambda i, j: (j, 0)),
                  pl.BlockSpec((tm, D_MODEL), lambda i, j: (i, 0))],
        out_specs=pl.BlockSpec((tm, D_MODEL), lambda i, j: (i, 0)),
        out_shape=jax.ShapeDtypeStruct((n, D_MODEL), F32),
        scratch_shapes=[pltpu.VMEM((tm * PEER_NKEYS, PEER_NKEYS), F32),
                        pltpu.VMEM((tm, D_MODEL), F32),
                        pltpu.VMEM((tm, nslot), F32)],
        compiler_params=_cparams(("parallel", "arbitrary")),
        name="peer_expert_v",
    )(zsel, w, aidx, bidx, v_b, h1)


def _ple_kernel(h_ref, p_ref, gp_ref, wg_ref, wp_ref, gf_ref, o_ref):
    h = h_ref[...]
    xn = _rms(h, gp_ref[...]).astype(BF16)
    gate = jax.nn.sigmoid(jnp.dot(xn, wg_ref[...], preferred_element_type=F32))
    proj = jnp.dot(p_ref[...].astype(BF16), wp_ref[...], preferred_element_type=F32)
    o_ref[...] = _rms(h + gate * proj, gf_ref[...])


def _ple(h2, p2, g_ple, wg_b, wp_b, g_final, tm=512):
    n = h2.shape[0]
    return pl.pallas_call(
        _ple_kernel,
        grid=(n // tm,),
        in_specs=[pl.BlockSpec((tm, D_MODEL), lambda i: (i, 0)),
                  pl.BlockSpec((tm, PLE_DIM), lambda i: (i, 0)),
                  pl.BlockSpec((1, D_MODEL), lambda i: (0, 0)),
                  pl.BlockSpec((D_MODEL, D_MODEL), lambda i: (0, 0)),
                  pl.BlockSpec((PLE_DIM, D_MODEL), lambda i: (0, 0)),
                  pl.BlockSpec((1, D_MODEL), lambda i: (0, 0))],
        out_specs=pl.BlockSpec((tm, D_MODEL), lambda i: (i, 0)),
        out_shape=jax.ShapeDtypeStruct((n, D_MODEL), F32),
        compiler_params=_cparams(("parallel",)),
        name="ple_final",
    )(h2, p2, g_ple, wg_b, wp_b, g_final)


def _block_diag(w):
    g, d, _ = w.shape
    eye = jnp.eye(g, dtype=w.dtype)
    return (eye[:, None, :, None] * w[:, :, None, :]).reshape(g * d, g * d)


def kernel(x, p, g_mix, w_in, conv_w, conv_b, w_rg, b_rg, w_ig, b_ig, lru_lambda, g_lru_out,
           g_attn_out, w_out, rel_bias, g_ffn, peer_wq, peer_k1, peer_k2, peer_u, peer_v,
           g_ple, w_ple_gate, w_ple_proj, g_final):
    batch, seq, dm = x.shape
    n = batch * seq
    assert w_in.shape[0] == 1, "single-layer problem: the last kernel fuses the final norm"
    row = lambda v: v.reshape(1, -1).astype(F32)
    btab = _bias_tables(rel_bias)
    h = x.reshape(n, dm)
    for i in range(1):
        xl, gate, q, k, v = _inproj(h, row(g_mix[i]), w_in[i].astype(BF16))
        y_lru = _lru(xl, gate, conv_w[i], row(conv_b[i]),
                     _block_diag(w_rg[i]).astype(BF16), row(b_rg[i]),
                     _block_diag(w_ig[i]).astype(BF16), row(b_ig[i]),
                     row(lru_lambda[i]), row(g_lru_out[i]), batch, seq)
        y_att = _attention(q, k, v, btab, batch, seq)
        h1, xn = _outproj(y_lru, y_att, h, row(g_attn_out[i]), w_out[i].astype(BF16),
                          row(g_ffn[i]))
        aidx, bidx, gates = _route(xn, peer_wq[i].astype(BF16), peer_k1[i].astype(BF16),
                                   peer_k2[i].astype(BF16))
        zsel = _expert_u(xn, peer_u[i].T.astype(BF16), aidx, bidx)
        h2 = _expert_v(zsel, gates, aidx, bidx, peer_v[i].astype(BF16), h1)
        h = _ple(h2, p[i].reshape(n, -1), row(g_ple[i]), w_ple_gate[i].astype(BF16),
                 w_ple_proj[i].astype(BF16), g_final.reshape(1, -1))
    return h.reshape(batch, seq, dm)
```

```python
import functools
import math

import jax
import jax.numpy as jnp
import numpy as np
from jax import lax
from jax.experimental import pallas as pl
from jax.experimental.pallas import tpu as pltpu

F32 = jnp.float32
BF16 = jnp.bfloat16
I32 = jnp.int32

D_MODEL = 1024
PLE_DIM = 256
D_LRU = 512
LRU_BLOCKS = 8
CONV_WIDTH = 4
LRU_C = 8.0
D_ATTN = 512
N_ATT_HEADS = 8
HEAD_DIM = 64
DILATED_GROUPS = ((128, 1), (512, 4), (2048, 16))
ATT_BLOCK = 128
REL_BUCKETS = 32
REL_MAX_DIST = 2048
PEER_HEADS = 8
PEER_NKEYS = 128
PEER_QDIM = 256
PEER_TOPK = 16
NORM_EPS = 1e-6

LANES = 128
SUBLANES = 8
NEG_BIG = -1e30
VMEM_LIMIT = 56 * 1024 * 1024


def _cparams(sem):
    return pltpu.CompilerParams(dimension_semantics=sem, vmem_limit_bytes=VMEM_LIMIT)


def _rms(x, g):
    ms = jnp.mean(x * x, axis=-1, keepdims=True)
    return (x * lax.rsqrt(ms + NORM_EPS)) * g


def _gelu(x):
    c = math.sqrt(2.0 / math.pi)
    return x * (0.5 * (1.0 + jnp.tanh(c * (x + 0.044715 * (x * x * x)))))


def _inproj_kernel(x_ref, g_ref, w_ref, xl_ref, gate_ref, q_ref, k_ref, v_ref):
    u = _rms(x_ref[...], g_ref[...]).astype(BF16)
    outs = (xl_ref, gate_ref, q_ref, k_ref, v_ref)
    for i, o_ref in enumerate(outs):
        z = jnp.dot(u, w_ref[:, i * 512:(i + 1) * 512], preferred_element_type=F32)
        if o_ref is q_ref:
            z = z * (HEAD_DIM ** -0.5)
        o_ref[...] = z


def _inproj(x2, g_mix, w_in_b, tm=512):
    n = x2.shape[0]
    blk = lambda: pl.BlockSpec((tm, 512), lambda i: (i, 0))
    return pl.pallas_call(
        _inproj_kernel,
        grid=(n // tm,),
        in_specs=[pl.BlockSpec((tm, D_MODEL), lambda i: (i, 0)),
                  pl.BlockSpec((1, D_MODEL), lambda i: (0, 0)),
                  pl.BlockSpec((D_MODEL, 2560), lambda i: (0, 0))],
        out_specs=[blk() for _ in range(5)],
        out_shape=[jax.ShapeDtypeStruct((n, 512), F32) for _ in range(5)],
        compiler_params=_cparams(("parallel",)),
        name="inproj",
    )(x2, g_mix, w_in_b)


LRU_CHUNK = 64


def _lru_kernel(xl_ref, gate_ref, cw_ref, cb_ref, wrg_ref, brg_ref, wig_ref, big_ref,
                lam_ref, g_ref, o_ref, xp_ref, h_ref):
    T = xl_ref.shape[0]
    tc = LRU_CHUNK
    xp_ref[0:SUBLANES, :] = jnp.zeros((SUBLANES, D_LRU), F32)
    xp_ref[SUBLANES:SUBLANES + T, :] = xl_ref[...]
    h_ref[...] = jnp.zeros_like(h_ref)
    nlam = -lam_ref[...]
    sp = jnp.maximum(nlam, 0.0) + jnp.log1p(jnp.exp(-jnp.abs(nlam)))
    row = lax.broadcasted_iota(I32, (tc, D_LRU), 0)

    def chunk(c, carry):
        t0 = pl.multiple_of(c * tc, tc)
        xa = xp_ref[pl.ds(t0, tc + SUBLANES), :]
        xc = cb_ref[...] + cw_ref[CONV_WIDTH - 1:CONV_WIDTH, :] * xa[SUBLANES:, :]
        for j in range(CONV_WIDTH - 1):
            sh = pltpu.roll(xa, CONV_WIDTH - 1 - j, axis=0)[SUBLANES:, :]
            xc = xc + cw_ref[j:j + 1, :] * sh
        xcb = xc.astype(BF16)
        r = jax.nn.sigmoid(jnp.dot(xcb, wrg_ref[...], preferred_element_type=F32) + brg_ref[...])
        ig = jax.nn.sigmoid(jnp.dot(xcb, wig_ref[...], preferred_element_type=F32) + big_ref[...])
        log_a = (-LRU_C) * r * sp
        a = jnp.exp(log_a)
        b = jnp.sqrt(-jnp.tanh(log_a) * (a * a + 1.0)) * (ig * xc)
        d = 1
        while d < tc:
            keep = row >= d
            a_sh = jnp.where(keep, pltpu.roll(a, d, axis=0), 1.0)
            b_sh = jnp.where(keep, pltpu.roll(b, d, axis=0), 0.0)
            b = a * b_sh + b
            a = a * a_sh
            d *= 2
        h = b + a * h_ref[...]
        h_ref[...] = h[tc - 1:tc, :]
        y = h * _gelu(gate_ref[pl.ds(t0, tc), :])
        o_ref[pl.ds(t0, tc), :] = _rms(y, g_ref[...]).astype(BF16)
        return carry

    lax.fori_loop(0, T // tc, chunk, 0)


def _lru(xl, gate, conv_w, conv_b, wrg_bd, b_rg, wig_bd, b_ig, lam, g_lru, batch, seq):
    vec = lambda: pl.BlockSpec((1, D_LRU), lambda b: (0, 0))
    mat = lambda: pl.BlockSpec((D_LRU, D_LRU), lambda b: (0, 0))
    return pl.pallas_call(
        _lru_kernel,
        grid=(batch,),
        in_specs=[pl.BlockSpec((seq, D_LRU), lambda b: (b, 0)),
                  pl.BlockSpec((seq, D_LRU), lambda b: (b, 0)),
                  pl.BlockSpec((CONV_WIDTH, D_LRU), lambda b: (0, 0)),
                  vec(), mat(), vec(), mat(), vec(), vec(), vec()],
        out_specs=pl.BlockSpec((seq, D_LRU), lambda b: (b, 0)),
        out_shape=jax.ShapeDtypeStruct((batch * seq, D_LRU), BF16),
        scratch_shapes=[pltpu.VMEM((seq + SUBLANES, D_LRU), F32),
                        pltpu.VMEM((1, D_LRU), F32)],
        compiler_params=_cparams(("parallel",)),
        name="rglru",
    )(xl, gate, conv_w, conv_b, wrg_bd, b_rg, wig_bd, b_ig, lam, g_lru)


ATT_UNROLL = 3


def _attend(qf, kf, vf, bias_a, bias_b, is_a):
    kb = kf.astype(BF16)
    vb = vf.astype(BF16)
    dn = (((1,), (1,)), ((), ()))
    outs = []
    for q_h, bias in ((jnp.where(is_a, qf, 0.0), bias_a), (jnp.where(is_a, 0.0, qf), bias_b)):
        s = lax.dot_general(q_h.astype(BF16), kb, dn, preferred_element_type=F32) + bias
        m = jnp.max(s, axis=-1, keepdims=True)
        e = jnp.exp(s - m)
        den = jnp.sum(e, axis=-1, keepdims=True)
        num = jnp.dot(e.astype(BF16), vb, preferred_element_type=F32)
        outs.append((num, m, den))
    (na, ma, da), (nb, mb, db) = outs
    shape = na.shape
    return (jnp.where(is_a, na, nb),
            jnp.where(is_a, jnp.broadcast_to(ma, shape), jnp.broadcast_to(mb, shape)),
            jnp.where(is_a, jnp.broadcast_to(da, shape), jnp.broadcast_to(db, shape)))


def _attn_kernel(q_ref, k_ref, v_ref, bt_ref, o_ref, num_ref, m_ref, den_ref):
    T = q_ref.shape[0]
    blk = ATT_BLOCK
    is_a = lax.broadcasted_iota(I32, (blk, LANES), 1) < HEAD_DIM

    def rows(start, size, d):
        return pl.ds(start, size) if d == 1 else pl.ds(start, size, stride=d)

    def first_block(br, d, r):
        sl = rows(r, blk, d)
        out = _attend(q_ref[sl, :], k_ref[sl, :], v_ref[sl, :],
                      bt_ref[br, 0, :, blk:], bt_ref[br, 1, :, blk:], is_a)
        for ref, val in zip((num_ref, m_ref, den_ref), out):
            ref[br, sl, :] = val

    def later_block(br, d, r, n):
        qs = rows(r + n * (blk * d), blk, d)
        ks = rows(r + (n - 1) * (blk * d), 2 * blk, d)
        out = _attend(q_ref[qs, :], k_ref[ks, :], v_ref[ks, :],
                      bt_ref[br, 0], bt_ref[br, 1], is_a)
        for ref, val in zip((num_ref, m_ref, den_ref), out):
            ref[br, qs, :] = val

    for br, (_, d) in enumerate(DILATED_GROUPS):
        nb = T // (d * blk)

        def firsts(r, carry, br=br, d=d):
            first_block(br, d, r)
            return carry

        def laters(i, carry, br=br, d=d, nb=nb):
            r = lax.div(i, nb - 1)
            later_block(br, d, r, 1 + lax.rem(i, nb - 1))
            return carry

        if d == 1:
            first_block(br, d, 0)
        else:
            lax.fori_loop(0, d, firsts, 0, unroll=ATT_UNROLL)
        if nb > 1:
            lax.fori_loop(0, d * (nb - 1), laters, 0, unroll=ATT_UNROLL)

    def merge(c, carry):
        sl = pl.ds(pl.multiple_of(c * blk, blk), blk)
        m0, m1, m2 = m_ref[0, sl, :], m_ref[1, sl, :], m_ref[2, sl, :]
        mm = jnp.maximum(jnp.maximum(m0, m1), m2)
        w0, w1, w2 = jnp.exp(m0 - mm), jnp.exp(m1 - mm), jnp.exp(m2 - mm)
        num = w0 * num_ref[0, sl, :] + w1 * num_ref[1, sl, :] + w2 * num_ref[2, sl, :]
        den = w0 * den_ref[0, sl, :] + w1 * den_ref[1, sl, :] + w2 * den_ref[2, sl, :]
        o_ref[sl, :] = num / den
        return carry

    lax.fori_loop(0, T // blk, merge, 0)


def _attention(q, k, v, btab, batch, seq):
    npairs = D_ATTN // LANES
    qkv = lambda: pl.BlockSpec((seq, LANES), lambda b, p: (b, p))
    return pl.pallas_call(
        _attn_kernel,
        grid=(batch, npairs),
        in_specs=[qkv(), qkv(), qkv(),
                  pl.BlockSpec((3, 2, ATT_BLOCK, 2 * ATT_BLOCK), lambda b, p: (0, p, 0, 0))],
        out_specs=pl.BlockSpec((seq, LANES), lambda b, p: (b, p)),
        out_shape=jax.ShapeDtypeStruct((batch * seq, D_ATTN), F32),
        scratch_shapes=[pltpu.VMEM((3, seq, LANES), F32) for _ in range(3)],
        compiler_params=_cparams(("parallel", "parallel")),
        name="dilated_attn",
    )(q, k, v, btab)


def _bias_tables(rel_bias):
    tabs = []
    for window, d in DILATED_GROUPS:
        w_sub = window // d
        qi = jnp.arange(ATT_BLOCK)[:, None]
        kj = jnp.arange(2 * ATT_BLOCK)[None, :]
        dist = qi + ATT_BLOCK - kj
        max_exact = REL_BUCKETS // 2
        nn = jnp.maximum(dist * d, 0)
        nf = jnp.maximum(nn, 1).astype(F32)
        large = max_exact + (jnp.log(nf / max_exact) / math.log(REL_MAX_DIST / max_exact)
                             * (REL_BUCKETS - max_exact)).astype(I32)
        large = jnp.minimum(large, REL_BUCKETS - 1)
        bucket = jnp.where(nn < max_exact, nn, large)
        bias = rel_bias[bucket].transpose(2, 0, 1).astype(F32)
        ok = (dist >= 0) & (dist <= w_sub)
        tabs.append(jnp.where(ok[None], bias, NEG_BIG))
    return jnp.stack(tabs, axis=0)


def _outproj_kernel(yl_ref, ya_ref, x_ref, ga_ref, w_ref, gf_ref, h_ref, xn_ref):
    ya = _rms(ya_ref[...], ga_ref[...]).astype(BF16)
    acc = jnp.dot(yl_ref[...], w_ref[0:D_LRU, :], preferred_element_type=F32)
    acc = acc + jnp.dot(ya, w_ref[D_LRU:, :], preferred_element_type=F32)
    h = x_ref[...] + acc
    h_ref[...] = h
    xn_ref[...] = _rms(h, gf_ref[...]).astype(BF16)


def _outproj(y_lru, y_att, x2, g_att, w_out_b, g_ffn, tm=512):
    n = x2.shape[0]
    return pl.pallas_call(
        _outproj_kernel,
        grid=(n // tm,),
        in_specs=[pl.BlockSpec((tm, D_LRU), lambda i: (i, 0)),
                  pl.BlockSpec((tm, D_ATTN), lambda i: (i, 0)),
                  pl.BlockSpec((tm, D_MODEL), lambda i: (i, 0)),
                  pl.BlockSpec((1, D_ATTN), lambda i: (0, 0)),
                  pl.BlockSpec((D_MODEL, D_MODEL), lambda i: (0, 0)),
                  pl.BlockSpec((1, D_MODEL), lambda i: (0, 0))],
        out_specs=[pl.BlockSpec((tm, D_MODEL), lambda i: (i, 0)),
                   pl.BlockSpec((tm, D_MODEL), lambda i: (i, 0))],
        out_shape=[jax.ShapeDtypeStruct((n, D_MODEL), F32),
                   jax.ShapeDtypeStruct((n, D_MODEL), BF16)],
        compiler_params=_cparams(("parallel",)),
        name="outproj",
    )(y_lru, y_att, x2, g_att, w_out_b, g_ffn)


ROUTE_TM = 128
_CAND_GROUPS = (("col", 0, 0, 8), ("col", 0, 8, 16), ("row", 0, 1, 8), ("row", 0, 8, 16),
                ("col", 1, 1, 8), ("row", 1, 2, 8), ("col", 2, 2, 5), ("row", 2, 3, 5),
                ("col", 3, 3, 4))


def _extract_topk(s, ids, val_ref, id_ref, big):
    def rnd(k, s):
        m = jnp.max(s, axis=0, keepdims=True)
        sel = jnp.min(jnp.where(s == m, ids, big), axis=0, keepdims=True)
        val_ref[pl.ds(k, 1), :] = m
        id_ref[pl.ds(k, 1), :] = sel
        return jnp.where(ids == sel, NEG_BIG, s)

    lax.fori_loop(0, PEER_TOPK, rnd, s)


def _route_kernel(xn_ref, wq_ref, k1_ref, k2_ref, a_ref, b_ref, w_ref,
                  q_ref, t1_ref, i1_ref, t2_ref, i2_ref, ts_ref, tp_ref, oa_ref, ob_ref, ow_ref):
    tm = xn_ref.shape[0]
    half = PEER_QDIM // 2
    q_ref[...] = jnp.dot(xn_ref[...], wq_ref[...], preferred_element_type=F32)
    key_ids = lax.broadcasted_iota(I32, (PEER_NKEYS, tm), 0).astype(F32)
    sub = lax.broadcasted_iota(I32, (SUBLANES, tm), 0)
    dn = (((1,), (1,)), ((), ()))

    def head(h, carry):
        c0 = pl.multiple_of(h * PEER_QDIM, half)
        c1 = pl.multiple_of(h * PEER_QDIM + half, half)
        q1 = q_ref[:, pl.ds(c0, half)].astype(BF16)
        q2 = q_ref[:, pl.ds(c1, half)].astype(BF16)
        s1 = lax.dot_general(k1_ref[...], q1, dn, preferred_element_type=F32)
        s2 = lax.dot_general(k2_ref[...], q2, dn, preferred_element_type=F32)
        _extract_topk(s1, key_ids, t1_ref, i1_ref, float(PEER_NKEYS))
        _extract_topk(s2, key_ids, t2_ref, i2_ref, float(PEER_NKEYS))
        cands, poss = [], []
        for kind, fixed, lo, hi in _CAND_GROUPS:
            base = (lo // SUBLANES) * SUBLANES
            rng = sub + base
            valid = (rng >= lo) & (rng < hi)
            if kind == "col":
                val = t1_ref[base:base + SUBLANES, :] + t2_ref[fixed:fixed + 1, :]
                pos = rng * PEER_TOPK + fixed
            else:
                val = t1_ref[fixed:fixed + 1, :] + t2_ref[base:base + SUBLANES, :]
                pos = fixed * PEER_TOPK + rng
            cands.append(jnp.where(valid, val, NEG_BIG))
            poss.append(pos.astype(F32))
        cand = jnp.concatenate(cands, axis=0)
        pos = jnp.concatenate(poss, axis=0)
        _extract_topk(cand, pos, ts_ref, tp_ref, float(PEER_TOPK * PEER_TOPK))
        top_s = ts_ref[...]
        top_p = tp_ref[...].astype(I32)
        ri = lax.shift_right_logical(top_p, 4)
        rj = lax.bitwise_and(top_p, PEER_TOPK - 1)
        a_sel = jnp.zeros((PEER_TOPK, tm), F32)
        b_sel = jnp.zeros((PEER_TOPK, tm), F32)
        for r in range(PEER_TOPK):
            a_sel = jnp.where(ri == r, i1_ref[r:r + 1, :], a_sel)
            b_sel = jnp.where(rj == r, i2_ref[r:r + 1, :], b_sel)
        e = jnp.exp(top_s - top_s[0:1, :])
        g = e / jnp.sum(e, axis=0, keepdims=True)
        r0 = pl.multiple_of(h * PEER_TOPK, PEER_TOPK)
        oa_ref[pl.ds(r0, PEER_TOPK), :] = a_sel
        ob_ref[pl.ds(r0, PEER_TOPK), :] = b_sel
        ow_ref[pl.ds(r0, PEER_TOPK), :] = g
        return carry

    lax.fori_loop(0, PEER_HEADS, head, 0)
    a_ref[...] = oa_ref[...].T.astype(I32)
    b_ref[...] = ob_ref[...].T.astype(I32)
    w_ref[...] = ow_ref[...].T


def _route(xn, wq_b, k1_b, k2_b):
    n = xn.shape[0]
    tm = ROUTE_TM
    nslot = PEER_HEADS * PEER_TOPK
    out = lambda: pl.BlockSpec((tm, nslot), lambda i: (i, 0))
    small = lambda: pltpu.VMEM((PEER_TOPK, tm), F32)
    return pl.pallas_call(
        _route_kernel,
        grid=(n // tm,),
        in_specs=[pl.BlockSpec((tm, D_MODEL), lambda i: (i, 0)),
                  pl.BlockSpec((D_MODEL, PEER_HEADS * PEER_QDIM), lambda i: (0, 0)),
                  pl.BlockSpec((PEER_NKEYS, PEER_QDIM // 2), lambda i: (0, 0)),
                  pl.BlockSpec((PEER_NKEYS, PEER_QDIM // 2), lambda i: (0, 0))],
        out_specs=[out(), out(), out()],
        out_shape=[jax.ShapeDtypeStruct((n, nslot), I32),
                   jax.ShapeDtypeStruct((n, nslot), I32),
                   jax.ShapeDtypeStruct((n, nslot), F32)],
        scratch_shapes=[pltpu.VMEM((tm, PEER_HEADS * PEER_QDIM), F32),
                        small(), small(), small(), small(), small(), small(),
                        pltpu.VMEM((nslot, tm), F32), pltpu.VMEM((nslot, tm), F32),
                        pltpu.VMEM((nslot, tm), F32)],
        compiler_params=_cparams(("parallel",)),
        name="peer_route",
    )(xn, wq_b, k1_b, k2_b)


EXP_CHUNK = 2048
EXP_NCHUNK = PEER_NKEYS * PEER_NKEYS // EXP_CHUNK
PAIR = 2 * PEER_NKEYS


def _expu_kernel(xn_ref, ut_ref, a_ref, b_ref, z_ref, buf0_ref, buf1_ref):
    tm = xn_ref.shape[0]
    j = pl.program_id(1)
    nj = pl.num_programs(1) - 1
    groups_per_chunk = EXP_CHUNK // PEER_NKEYS

    def matmul_into(buf_ref):
        buf_ref[...] = jnp.dot(xn_ref[...], ut_ref[...], preferred_element_type=F32)

    def gather_from(buf_ref):
        a0 = (j - 1) * groups_per_chunk
        for g in range(tm // SUBLANES):
            rs = slice(g * SUBLANES, (g + 1) * SUBLANES)
            bi = b_ref[rs, :]
            ai = a_ref[rs, :] - a0
            acc = z_ref[rs, :]
            for t in range(groups_per_chunk):
                got = jnp.take_along_axis(buf_ref[rs, t * PEER_NKEYS:(t + 1) * PEER_NKEYS], bi,
                                          axis=1, mode="promise_in_bounds")
                acc = jnp.where(ai == t, got, acc)
            z_ref[rs, :] = acc

    odd = lax.rem(j, 2) == 1
    inner = jnp.logical_and(j > 0, j < nj)

    @pl.when(j == 0)
    def _():
        z_ref[...] = jnp.zeros_like(z_ref)
        matmul_into(buf0_ref)

    @pl.when(jnp.logical_and(inner, odd))
    def _():
        gather_from(buf0_ref)
        matmul_into(buf1_ref)

    @pl.when(jnp.logical_and(inner, jnp.logical_not(odd)))
    def _():
        gather_from(buf1_ref)
        matmul_into(buf0_ref)

    @pl.when(j == nj)
    def _():
        gather_from(buf1_ref if (EXP_NCHUNK - 1) % 2 == 1 else buf0_ref)


def _expert_u(xn, ut_b, aidx, bidx, tm=512):
    n = xn.shape[0]
    nexp = ut_b.shape[1]
    assert nexp == EXP_NCHUNK * EXP_CHUNK
    nslot = PEER_HEADS * PEER_TOPK
    last = EXP_NCHUNK - 1
    return pl.pallas_call(
        _expu_kernel,
        grid=(n // tm, EXP_NCHUNK + 1),
        in_specs=[pl.BlockSpec((tm, D_MODEL), lambda i, j: (i, 0)),
                  pl.BlockSpec((D_MODEL, EXP_CHUNK), lambda i, j: (0, jnp.minimum(j, last))),
                  pl.BlockSpec((tm, nslot), lambda i, j: (i, 0)),
                  pl.BlockSpec((tm, nslot), lambda i, j: (i, 0))],
        out_specs=pl.BlockSpec((tm, nslot), lambda i, j: (i, 0)),
        out_shape=jax.ShapeDtypeStruct((n, nslot), F32),
        scratch_shapes=[pltpu.VMEM((tm, EXP_CHUNK), F32), pltpu.VMEM((tm, EXP_CHUNK), F32)],
        compiler_params=_cparams(("parallel", "arbitrary")),
        name="peer_expert_u",
    )(xn, ut_b, aidx, bidx)


def _expv_kernel(z_ref, w_ref, a_ref, b_ref, v_ref, h_ref, o_ref, s_ref, acc_ref, c_ref):
    tm = z_ref.shape[0]
    nk = PEER_NKEYS
    j = pl.program_id(1)

    @pl.when(j == 0)
    def _():
        acc_ref[...] = jnp.zeros_like(acc_ref)
        c_ref[...] = w_ref[...] * _gelu(z_ref[...])
        sub = lax.broadcasted_iota(I32, (nk, nk), 0)
        dn = (((1,), (1,)), ((), ()))

        def tok(n, carry):
            arow = a_ref[pl.ds(n, 1), :]
            brow = b_ref[pl.ds(n, 1), :]
            crow = c_ref[pl.ds(n, 1), :]
            oat = jnp.where(arow == sub, 1.0, 0.0).astype(BF16)
            wobt = jnp.where(brow == sub, crow, 0.0).astype(BF16)
            s_ref[pl.ds(pl.multiple_of(n * nk, nk), nk), :] = lax.dot_general(
                oat, wobt, dn, preferred_element_type=F32)
            return carry

        lax.fori_loop(0, tm, tok, 0, unroll=2)

    for pp in range(EXP_CHUNK // PAIR):
        a0 = j * (EXP_CHUNK // nk) + 2 * pp
        lhs = jnp.concatenate([s_ref[pl.ds(a0, tm, stride=nk), :],
                               s_ref[pl.ds(a0 + 1, tm, stride=nk), :]], axis=1).astype(BF16)
        acc_ref[...] += jnp.dot(lhs, v_ref[pp * PAIR:(pp + 1) * PAIR, :],
                                preferred_element_type=F32)

    @pl.when(j == pl.num_programs(1) - 1)
    def _():
        o_ref[...] = h_ref[...] + acc_ref[...]


def _expert_v(zsel, w, aidx, bidx, v_b, h1, tm=256):
    n = zsel.shape[0]
    nexp = v_b.shape[0]
    nslot = PEER_HEADS * PEER_TOPK
    slot = lambda: pl.BlockSpec((tm, nslot), lambda i, j: (i, 0))
    return pl.pallas_call(
        _expv_kernel,
        grid=(n // tm, nexp // EXP_CHUNK),
        in_specs=[slot(), slot(), slot(), slot(),
                  pl.BlockSpec((EXP_CHUNK, D_MODEL), lambda i, j: (j, 0)),
                  pl.BlockSpec((tm, D_MODEL), lambda i, j: (i, 0))],
        out_specs=pl.BlockSpec((tm, D_MODEL), lambda i, j: (i, 0)),
        out_shape=jax.ShapeDtypeStruct((n, D_MODEL), F32),
        scratch_shapes=[pltpu.VMEM((tm * PEER_NKEYS, PEER_NKEYS), F32),
                        pltpu.VMEM((tm, D_MODEL), F32),
                        pltpu.VMEM((tm, nslot), F32)],
        compiler_params=_cparams(("parallel", "arbitrary")),
        name="peer_expert_v",
    )(zsel, w, aidx, bidx, v_b, h1)


def _ple_kernel(h_ref, p_ref, gp_ref, wg_ref, wp_ref, gf_ref, o_ref):
    h = h_ref[...]
    xn = _rms(h, gp_ref[...]).astype(BF16)
    gate = jax.nn.sigmoid(jnp.dot(xn, wg_ref[...], preferred_element_type=F32))
    proj = jnp.dot(p_ref[...].astype(BF16), wp_ref[...], preferred_element_type=F32)
    o_ref[...] = _rms(h + gate * proj, gf_ref[...])


def _ple(h2, p2, g_ple, wg_b, wp_b, g_final, tm=512):
    n = h2.shape[0]
    return pl.pallas_call(
        _ple_kernel,
        grid=(n // tm,),
        in_specs=[pl.BlockSpec((tm, D_MODEL), lambda i: (i, 0)),
                  pl.BlockSpec((tm, PLE_DIM), lambda i: (i, 0)),
                  pl.BlockSpec((1, D_MODEL), lambda i: (0, 0)),
                  pl.BlockSpec((D_MODEL, D_MODEL), lambda i: (0, 0)),
                  pl.BlockSpec((PLE_DIM, D_MODEL), lambda i: (0, 0)),
                  pl.BlockSpec((1, D_MODEL), lambda i: (0, 0))],
        out_specs=pl.BlockSpec((tm, D_MODEL), lambda i: (i, 0)),
        out_shape=jax.ShapeDtypeStruct((n, D_MODEL), F32),
        compiler_params=_cparams(("parallel",)),
        name="ple_final",
    )(h2, p2, g_ple, wg_b, wp_b, g_final)


def _block_diag(w):
    g, d, _ = w.shape
    eye = jnp.eye(g, dtype=w.dtype)
    return (eye[:, None, :, None] * w[:, :, None, :]).reshape(g * d, g * d)


def kernel(x, p, g_mix, w_in, conv_w, conv_b, w_rg, b_rg, w_ig, b_ig, lru_lambda, g_lru_out,
           g_attn_out, w_out, rel_bias, g_ffn, peer_wq, peer_k1, peer_k2, peer_u, peer_v,
           g_ple, w_ple_gate, w_ple_proj, g_final):
    batch, seq, dm = x.shape
    n = batch * seq
    assert w_in.shape[0] == 1, "single-layer problem: the last kernel fuses the final norm"
    row = lambda v: v.reshape(1, -1).astype(F32)
    btab = _bias_tables(rel_bias)
    h = x.reshape(n, dm)
    for i in range(1):
        xl, gate, q, k, v = _inproj(h, row(g_mix[i]), w_in[i].astype(BF16))
        y_lru = _lru(xl, gate, conv_w[i], row(conv_b[i]),
                     _block_diag(w_rg[i]).astype(BF16), row(b_rg[i]),
                     _block_diag(w_ig[i]).astype(BF16), row(b_ig[i]),
                     row(lru_lambda[i]), row(g_lru_out[i]), batch, seq)
        y_att = _attention(q, k, v, btab, batch, seq)
        h1, xn = _outproj(y_lru, y_att, h, row(g_attn_out[i]), w_out[i].astype(BF16),
                          row(g_ffn[i]))
        aidx, bidx, gates = _route(xn, peer_wq[i].astype(BF16), peer_k1[i].astype(BF16),
                                   peer_k2[i].astype(BF16))
        zsel = _expert_u(xn, peer_u[i].T.astype(BF16), aidx, bidx)
        h2 = _expert_v(zsel, gates, aidx, bidx, peer_v[i].astype(BF16), h1)
        h = _ple(h2, p[i].reshape(n, -1), row(g_ple[i]), w_ple_gate[i].astype(BF16),
                 w_ple_proj[i].astype(BF16), g_final.reshape(1, -1))
    return h.reshape(batch, seq, dm)
```

```python
import functools
import math

import jax
import jax.numpy as jnp
import numpy as np
from jax import lax
from jax.experimental import pallas as pl
from jax.experimental.pallas import tpu as pltpu

F32 = jnp.float32
BF16 = jnp.bfloat16
I32 = jnp.int32

D_MODEL = 1024
PLE_DIM = 256
D_LRU = 512
LRU_BLOCKS = 8
CONV_WIDTH = 4
LRU_C = 8.0
D_ATTN = 512
N_ATT_HEADS = 8
HEAD_DIM = 64
DILATED_GROUPS = ((128, 1), (512, 4), (2048, 16))
ATT_BLOCK = 128
REL_BUCKETS = 32
REL_MAX_DIST = 2048
PEER_HEADS = 8
PEER_NKEYS = 128
PEER_QDIM = 256
PEER_TOPK = 16
NORM_EPS = 1e-6

LANES = 128
SUBLANES = 8
NEG_BIG = -1e30
VMEM_LIMIT = 56 * 1024 * 1024


def _cparams(sem):
    return pltpu.CompilerParams(dimension_semantics=sem, vmem_limit_bytes=VMEM_LIMIT)


def _rms(x, g):
    ms = jnp.mean(x * x, axis=-1, keepdims=True)
    return (x * lax.rsqrt(ms + NORM_EPS)) * g


def _gelu(x):
    c = math.sqrt(2.0 / math.pi)
    return x * (0.5 * (1.0 + jnp.tanh(c * (x + 0.044715 * (x * x * x)))))


def _inproj_kernel(x_ref, g_ref, w_ref, xl_ref, gate_ref, q_ref, k_ref, v_ref):
    u = _rms(x_ref[...], g_ref[...]).astype(BF16)
    outs = (xl_ref, gate_ref, q_ref, k_ref, v_ref)
    for i, o_ref in enumerate(outs):
        z = jnp.dot(u, w_ref[:, i * 512:(i + 1) * 512], preferred_element_type=F32)
        if o_ref is q_ref:
            z = z * (HEAD_DIM ** -0.5)
        o_ref[...] = z


def _inproj(x2, g_mix, w_in_b, tm=512):
    n = x2.shape[0]
    blk = lambda: pl.BlockSpec((tm, 512), lambda i: (i, 0))
    return pl.pallas_call(
        _inproj_kernel,
        grid=(n // tm,),
        in_specs=[pl.BlockSpec((tm, D_MODEL), lambda i: (i, 0)),
                  pl.BlockSpec((1, D_MODEL), lambda i: (0, 0)),
                  pl.BlockSpec((D_MODEL, 2560), lambda i: (0, 0))],
        out_specs=[blk() for _ in range(5)],
        out_shape=[jax.ShapeDtypeStruct((n, 512), F32) for _ in range(5)],
        compiler_params=_cparams(("parallel",)),
        name="inproj",
    )(x2, g_mix, w_in_b)


LRU_CHUNK = 64


def _lru_kernel(xl_ref, gate_ref, cw_ref, cb_ref, wrg_ref, brg_ref, wig_ref, big_ref,
                lam_ref, g_ref, o_ref, xp_ref, h_ref):
    T = xl_ref.shape[0]
    tc = LRU_CHUNK
    xp_ref[0:SUBLANES, :] = jnp.zeros((SUBLANES, D_LRU), F32)
    xp_ref[SUBLANES:SUBLANES + T, :] = xl_ref[...]
    h_ref[...] = jnp.zeros_like(h_ref)
    nlam = -lam_ref[...]
    sp = jnp.maximum(nlam, 0.0) + jnp.log1p(jnp.exp(-jnp.abs(nlam)))
    row = lax.broadcasted_iota(I32, (tc, D_LRU), 0)

    def chunk(c, carry):
        t0 = pl.multiple_of(c * tc, tc)
        xa = xp_ref[pl.ds(t0, tc + SUBLANES), :]
        xc = cb_ref[...] + cw_ref[CONV_WIDTH - 1:CONV_WIDTH, :] * xa[SUBLANES:, :]
        for j in range(CONV_WIDTH - 1):
            sh = pltpu.roll(xa, CONV_WIDTH - 1 - j, axis=0)[SUBLANES:, :]
            xc = xc + cw_ref[j:j + 1, :] * sh
        xcb = xc.astype(BF16)
        r = jax.nn.sigmoid(jnp.dot(xcb, wrg_ref[...], preferred_element_type=F32) + brg_ref[...])
        ig = jax.nn.sigmoid(jnp.dot(xcb, wig_ref[...], preferred_element_type=F32) + big_ref[...])
        log_a = (-LRU_C) * r * sp
        a = jnp.exp(log_a)
        b = jnp.sqrt(-jnp.tanh(log_a) * (a * a + 1.0)) * (ig * xc)
        d = 1
        while d < tc:
            keep = row >= d
            a_sh = jnp.where(keep, pltpu.roll(a, d, axis=0), 1.0)
            b_sh = jnp.where(keep, pltpu.roll(b, d, axis=0), 0.0)
            b = a * b_sh + b
            a = a * a_sh
            d *= 2
        h = b + a * h_ref[...]
        h_ref[...] = h[tc - 1:tc, :]
        y = h * _gelu(gate_ref[pl.ds(t0, tc), :])
        o_ref[pl.ds(t0, tc), :] = _rms(y, g_ref[...]).astype(BF16)
        return carry

    lax.fori_loop(0, T // tc, chunk, 0)


def _lru(xl, gate, conv_w, conv_b, wrg_bd, b_rg, wig_bd, b_ig, lam, g_lru, batch, seq):
    vec = lambda: pl.BlockSpec((1, D_LRU), lambda b: (0, 0))
    mat = lambda: pl.BlockSpec((D_LRU, D_LRU), lambda b: (0, 0))
    return pl.pallas_call(
        _lru_kernel,
        grid=(batch,),
        in_specs=[pl.BlockSpec((seq, D_LRU), lambda b: (b, 0)),
                  pl.BlockSpec((seq, D_LRU), lambda b: (b, 0)),
                  pl.BlockSpec((CONV_WIDTH, D_LRU), lambda b: (0, 0)),
                  vec(), mat(), vec(), mat(), vec(), vec(), vec()],
        out_specs=pl.BlockSpec((seq, D_LRU), lambda b: (b, 0)),
        out_shape=jax.ShapeDtypeStruct((batch * seq, D_LRU), BF16),
        scratch_shapes=[pltpu.VMEM((seq + SUBLANES, D_LRU), F32),
                        pltpu.VMEM((1, D_LRU), F32)],
        compiler_params=_cparams(("parallel",)),
        name="rglru",
    )(xl, gate, conv_w, conv_b, wrg_bd, b_rg, wig_bd, b_ig, lam, g_lru)


ATT_UNROLL = 3


def _attend(qf, kf, vf, bias_a, bias_b, is_a):
    kb = kf.astype(BF16)
    vb = vf.astype(BF16)
    dn = (((1,), (1,)), ((), ()))
    outs = []
    for q_h, bias in ((jnp.where(is_a, qf, 0.0), bias_a), (jnp.where(is_a, 0.0, qf), bias_b)):
        s = lax.dot_general(q_h.astype(BF16), kb, dn, preferred_element_type=F32) + bias
        m = jnp.max(s, axis=-1, keepdims=True)
        e = jnp.exp(s - m)
        den = jnp.sum(e, axis=-1, keepdims=True)
        num = jnp.dot(e.astype(BF16), vb, preferred_element_type=F32)
        outs.append((num, m, den))
    (na, ma, da), (nb, mb, db) = outs
    shape = na.shape
    return (jnp.where(is_a, na, nb),
            jnp.where(is_a, jnp.broadcast_to(ma, shape), jnp.broadcast_to(mb, shape)),
            jnp.where(is_a, jnp.broadcast_to(da, shape), jnp.broadcast_to(db, shape)))


def _attn_kernel(q_ref, k_ref, v_ref, bt_ref, o_ref, num_ref, m_ref, den_ref):
    T = q_ref.shape[0]
    blk = ATT_BLOCK
    is_a = lax.broadcasted_iota(I32, (blk, LANES), 1) < HEAD_DIM

    def rows(start, size, d):
        return pl.ds(start, size) if d == 1 else pl.ds(start, size, stride=d)

    def first_block(br, d, r):
        sl = rows(r, blk, d)
        out = _attend(q_ref[sl, :], k_ref[sl, :], v_ref[sl, :],
                      bt_ref[br, 0, :, blk:], bt_ref[br, 1, :, blk:], is_a)
        for ref, val in zip((num_ref, m_ref, den_ref), out):
            ref[br, sl, :] = val

    def later_block(br, d, r, n):
        qs = rows(r + n * (blk * d), blk, d)
        ks = rows(r + (n - 1) * (blk * d), 2 * blk, d)
        out = _attend(q_ref[qs, :], k_ref[ks, :], v_ref[ks, :],
                      bt_ref[br, 0], bt_ref[br, 1], is_a)
        for ref, val in zip((num_ref, m_ref, den_ref), out):
            ref[br, qs, :] = val

    for br, (_, d) in enumerate(DILATED_GROUPS):
        nb = T // (d * blk)

        def firsts(r, carry, br=br, d=d):
            first_block(br, d, r)
            return carry

        def laters(i, carry, br=br, d=d, nb=nb):
            r = lax.div(i, nb - 1)
            later_block(br, d, r, 1 + lax.rem(i, nb - 1))
            return carry

        if d == 1:
            first_block(br, d, 0)
        else:
            lax.fori_loop(0, d, firsts, 0, unroll=ATT_UNROLL)
        if nb > 1:
            lax.fori_loop(0, d * (nb - 1), laters, 0, unroll=ATT_UNROLL)

    def merge(c, carry):
        sl = pl.ds(pl.multiple_of(c * blk, blk), blk)
        m0, m1, m2 = m_ref[0, sl, :], m_ref[1, sl, :], m_ref[2, sl, :]
        mm = jnp.maximum(jnp.maximum(m0, m1), m2)
        w0, w1, w2 = jnp.exp(m0 - mm), jnp.exp(m1 - mm), jnp.exp(m2 - mm)
        num = w0 * num_ref[0, sl, :] + w1 * num_ref[1, sl, :] + w2 * num_ref[2, sl, :]
        den = w0 * den_ref[0, sl, :] + w1 * den_ref[1, sl, :] + w2 * den_ref[2, sl, :]
        o_ref[sl, :] = num / den
        return carry

    lax.fori_loop(0, T // blk, merge, 0)


def _attention(q, k, v, btab, batch, seq):
    npairs = D_ATTN // LANES
    qkv = lambda: pl.BlockSpec((seq, LANES), lambda b, p: (b, p))
    return pl.pallas_call(
        _attn_kernel,
        grid=(batch, npairs),
        in_specs=[qkv(), qkv(), qkv(),
                  pl.BlockSpec((3, 2, ATT_BLOCK, 2 * ATT_BLOCK), lambda b, p: (0, p, 0, 0))],
        out_specs=pl.BlockSpec((seq, LANES), lambda b, p: (b, p)),
        out_shape=jax.ShapeDtypeStruct((batch * seq, D_ATTN), F32),
        scratch_shapes=[pltpu.VMEM((3, seq, LANES), F32) for _ in range(3)],
        compiler_params=_cparams(("parallel", "parallel")),
        name="dilated_attn",
    )(q, k, v, btab)


def _bias_tables(rel_bias):
    tabs = []
    for window, d in DILATED_GROUPS:
        w_sub = window // d
        qi = jnp.arange(ATT_BLOCK)[:, None]
        kj = jnp.arange(2 * ATT_BLOCK)[None, :]
        dist = qi + ATT_BLOCK - kj
        max_exact = REL_BUCKETS // 2
        nn = jnp.maximum(dist * d, 0)
        nf = jnp.maximum(nn, 1).astype(F32)
        large = max_exact + (jnp.log(nf / max_exact) / math.log(REL_MAX_DIST / max_exact)
                             * (REL_BUCKETS - max_exact)).astype(I32)
        large = jnp.minimum(large, REL_BUCKETS - 1)
        bucket = jnp.where(nn < max_exact, nn, large)
        rb = rel_bias.astype(F32)
        bias = jnp.zeros((N_ATT_HEADS,) + bucket.shape, F32)
        for bkt in range(REL_BUCKETS):
            bias = jnp.where(bucket[None] == bkt, rb[bkt][:, None, None], bias)
        ok = (dist >= 0) & (dist <= w_sub)
        tabs.append(jnp.where(ok[None], bias, NEG_BIG))
    return jnp.stack(tabs, axis=0)


def _outproj_kernel(yl_ref, ya_ref, x_ref, ga_ref, w_ref, gf_ref, h_ref, xn_ref):
    ya = _rms(ya_ref[...], ga_ref[...]).astype(BF16)
    acc = jnp.dot(yl_ref[...], w_ref[0:D_LRU, :], preferred_element_type=F32)
    acc = acc + jnp.dot(ya, w_ref[D_LRU:, :], preferred_element_type=F32)
    h = x_ref[...] + acc
    h_ref[...] = h
    xn_ref[...] = _rms(h, gf_ref[...]).astype(BF16)


def _outproj(y_lru, y_att, x2, g_att, w_out_b, g_ffn, tm=512):
    n = x2.shape[0]
    return pl.pallas_call(
        _outproj_kernel,
        grid=(n // tm,),
        in_specs=[pl.BlockSpec((tm, D_LRU), lambda i: (i, 0)),
                  pl.BlockSpec((tm, D_ATTN), lambda i: (i, 0)),
                  pl.BlockSpec((tm, D_MODEL), lambda i: (i, 0)),
                  pl.BlockSpec((1, D_ATTN), lambda i: (0, 0)),
                  pl.BlockSpec((D_MODEL, D_MODEL), lambda i: (0, 0)),
                  pl.BlockSpec((1, D_MODEL), lambda i: (0, 0))],
        out_specs=[pl.BlockSpec((tm, D_MODEL), lambda i: (i, 0)),
                   pl.BlockSpec((tm, D_MODEL), lambda i: (i, 0))],
        out_shape=[jax.ShapeDtypeStruct((n, D_MODEL), F32),
                   jax.ShapeDtypeStruct((n, D_MODEL), BF16)],
        compiler_params=_cparams(("parallel",)),
        name="outproj",
    )(y_lru, y_att, x2, g_att, w_out_b, g_ffn)


ROUTE_TM = 128
_CAND_GROUPS = (("col", 0, 0, 8), ("col", 0, 8, 16), ("row", 0, 1, 8), ("row", 0, 8, 16),
                ("col", 1, 1, 8), ("row", 1, 2, 8), ("col", 2, 2, 5), ("row", 2, 3, 5),
                ("col", 3, 3, 4))


def _tree(op, xs):
    xs = list(xs)
    while len(xs) > 1:
        xs = [op(xs[i], xs[i + 1]) for i in range(0, len(xs) - 1, 2)] + (xs[-1:] if len(xs) % 2 else [])
    return xs[0]


def _extract_topk_keys(chains, sub_f, val_ref, id_ref):
    nv = PEER_NKEYS // SUBLANES
    big = float(PEER_NKEYS)

    def rnd(k, states):
        nxt = []
        for c, parts in enumerate(states):
            m = jnp.max(_tree(jnp.maximum, parts), axis=0, keepdims=True)
            first = _tree(jnp.minimum, [jnp.where(p == m, float(SUBLANES * v), big)
                                        for v, p in enumerate(parts)])
            sel = jnp.min(first + sub_f, axis=0, keepdims=True)
            val_ref[c, pl.ds(k, 1), :] = m
            id_ref[c, pl.ds(k, 1), :] = sel
            off = sel - sub_f
            nxt.append(tuple(jnp.where(off == float(SUBLANES * v), NEG_BIG, p)
                             for v, p in enumerate(parts)))
        return tuple(nxt)

    init = tuple(tuple(s[v * SUBLANES:(v + 1) * SUBLANES, :] for v in range(nv)) for s in chains)
    lax.fori_loop(0, PEER_TOPK, rnd, init)


def _extract_topk(chains, ids, big, val_ref, id_ref):
    def rnd(k, states):
        nxt = []
        for c, s in enumerate(states):
            m = jnp.max(s, axis=0, keepdims=True)
            sel = jnp.min(jnp.where(s == m, ids, big), axis=0, keepdims=True)
            val_ref[c, pl.ds(k, 1), :] = m
            id_ref[c, pl.ds(k, 1), :] = sel
            nxt.append(jnp.where(ids == sel, NEG_BIG, s))
        return tuple(nxt)

    lax.fori_loop(0, PEER_TOPK, rnd, tuple(chains))


def _route_kernel(xn_ref, wqt_ref, k1_ref, k2_ref, a_ref, b_ref, w_ref,
                  qt_ref, tv_ref, ti_ref, cv_ref, cp_ref, oa_ref, ob_ref, ow_ref):
    tm = xn_ref.shape[0]
    half = PEER_QDIM // 2
    dn = (((1,), (1,)), ((), ()))
    qt_ref[...] = lax.dot_general(wqt_ref[...], xn_ref[...], dn, preferred_element_type=F32)
    sub = lax.broadcasted_iota(I32, (SUBLANES, tm), 0)
    sub_f = sub.astype(F32)

    def scores(k_ref, row0):
        qs = qt_ref[pl.ds(pl.multiple_of(row0, half), half), :].astype(BF16)
        return jnp.dot(k_ref[...], qs, preferred_element_type=F32)

    def candidates(c1, c2):
        cands, poss = [], []
        for kind, fixed, lo, hi in _CAND_GROUPS:
            base = (lo // SUBLANES) * SUBLANES
            rng = sub + base
            valid = (rng >= lo) & (rng < hi)
            if kind == "col":
                val = tv_ref[c1, base:base + SUBLANES, :] + tv_ref[c2, fixed:fixed + 1, :]
                pos = rng * PEER_TOPK + fixed
            else:
                val = tv_ref[c1, fixed:fixed + 1, :] + tv_ref[c2, base:base + SUBLANES, :]
                pos = fixed * PEER_TOPK + rng
            cands.append(jnp.where(valid, val, NEG_BIG))
            poss.append(pos.astype(F32))
        return jnp.concatenate(cands, axis=0), jnp.concatenate(poss, axis=0)

    def head_pair(hp, carry):
        for u in range(2):
            row0 = (2 * hp + u) * PEER_QDIM
            _extract_topk_keys([scores(k1_ref, row0), scores(k2_ref, row0 + half)], sub_f,
                               tv_ref.at[pl.ds(2 * u, 2)], ti_ref.at[pl.ds(2 * u, 2)])
        cand0, pos = candidates(0, 1)
        cand1, _ = candidates(2, 3)
        _extract_topk([cand0, cand1], pos, float(PEER_TOPK * PEER_TOPK), cv_ref, cp_ref)
        for u in range(2):
            top_s = cv_ref[u]
            top_p = cp_ref[u].astype(I32)
            ri = lax.shift_right_logical(top_p, 4)
            rj = lax.bitwise_and(top_p, PEER_TOPK - 1)
            a_sel = jnp.zeros((PEER_TOPK, tm), F32)
            b_sel = jnp.zeros((PEER_TOPK, tm), F32)
            for r in range(PEER_TOPK):
                a_sel = jnp.where(ri == r, ti_ref[2 * u, r:r + 1, :], a_sel)
                b_sel = jnp.where(rj == r, ti_ref[2 * u + 1, r:r + 1, :], b_sel)
            e = jnp.exp(top_s - top_s[0:1, :])
            g = e / jnp.sum(e, axis=0, keepdims=True)
            r0 = pl.multiple_of((2 * hp + u) * PEER_TOPK, PEER_TOPK)
            oa_ref[pl.ds(r0, PEER_TOPK), :] = a_sel
            ob_ref[pl.ds(r0, PEER_TOPK), :] = b_sel
            ow_ref[pl.ds(r0, PEER_TOPK), :] = g
        return carry

    lax.fori_loop(0, PEER_HEADS // 2, head_pair, 0)
    a_ref[...] = oa_ref[...].T.astype(I32)
    b_ref[...] = ob_ref[...].T.astype(I32)
    w_ref[...] = ow_ref[...].T


def _route(xn, wqt_b, k1_b, k2_b):
    n = xn.shape[0]
    tm = ROUTE_TM
    nslot = PEER_HEADS * PEER_TOPK
    out = lambda: pl.BlockSpec((tm, nslot), lambda i: (i, 0))
    return pl.pallas_call(
        _route_kernel,
        grid=(n // tm,),
        in_specs=[pl.BlockSpec((tm, D_MODEL), lambda i: (i, 0)),
                  pl.BlockSpec((PEER_HEADS * PEER_QDIM, D_MODEL), lambda i: (0, 0)),
                  pl.BlockSpec((PEER_NKEYS, PEER_QDIM // 2), lambda i: (0, 0)),
                  pl.BlockSpec((PEER_NKEYS, PEER_QDIM // 2), lambda i: (0, 0))],
        out_specs=[out(), out(), out()],
        out_shape=[jax.ShapeDtypeStruct((n, nslot), I32),
                   jax.ShapeDtypeStruct((n, nslot), I32),
                   jax.ShapeDtypeStruct((n, nslot), F32)],
        scratch_shapes=[pltpu.VMEM((PEER_HEADS * PEER_QDIM, tm), F32),
                        pltpu.VMEM((4, PEER_TOPK, tm), F32), pltpu.VMEM((4, PEER_TOPK, tm), F32),
                        pltpu.VMEM((2, PEER_TOPK, tm), F32), pltpu.VMEM((2, PEER_TOPK, tm), F32),
                        pltpu.VMEM((nslot, tm), F32), pltpu.VMEM((nslot, tm), F32),
                        pltpu.VMEM((nslot, tm), F32)],
        compiler_params=_cparams(("parallel",)),
        name="peer_route",
    )(xn, wqt_b, k1_b, k2_b)


EXP_CHUNK = 2048
EXP_NCHUNK = PEER_NKEYS * PEER_NKEYS // EXP_CHUNK
PAIR = 2 * PEER_NKEYS


def _expu_kernel(xn_ref, ut_ref, a_ref, b_ref, z_ref, buf0_ref, buf1_ref):
    tm = xn_ref.shape[0]
    j = pl.program_id(1)
    nj = pl.num_programs(1) - 1
    groups_per_chunk = EXP_CHUNK // PEER_NKEYS

    def matmul_into(buf_ref):
        buf_ref[...] = jnp.dot(xn_ref[...], ut_ref[...], preferred_element_type=F32)

    def gather_from(buf_ref):
        a0 = (j - 1) * groups_per_chunk
        for g in range(tm // SUBLANES):
            rs = slice(g * SUBLANES, (g + 1) * SUBLANES)
            bi = b_ref[rs, :]
            ai = a_ref[rs, :] - a0
            acc = z_ref[rs, :]
            for t in range(groups_per_chunk):
                got = jnp.take_along_axis(buf_ref[rs, t * PEER_NKEYS:(t + 1) * PEER_NKEYS], bi,
                                          axis=1, mode="promise_in_bounds")
                acc = jnp.where(ai == t, got, acc)
            z_ref[rs, :] = acc

    odd = lax.rem(j, 2) == 1
    inner = jnp.logical_and(j > 0, j < nj)

    @pl.when(j == 0)
    def _():
        z_ref[...] = jnp.zeros_like(z_ref)
        matmul_into(buf0_ref)

    @pl.when(jnp.logical_and(inner, odd))
    def _():
        gather_from(buf0_ref)
        matmul_into(buf1_ref)

    @pl.when(jnp.logical_and(inner, jnp.logical_not(odd)))
    def _():
        gather_from(buf1_ref)
        matmul_into(buf0_ref)

    @pl.when(j == nj)
    def _():
        gather_from(buf1_ref if (EXP_NCHUNK - 1) % 2 == 1 else buf0_ref)


def _expert_u(xn, ut_b, aidx, bidx, tm=512):
    n = xn.shape[0]
    nexp = ut_b.shape[1]
    assert nexp == EXP_NCHUNK * EXP_CHUNK
    nslot = PEER_HEADS * PEER_TOPK
    last = EXP_NCHUNK - 1
    return pl.pallas_call(
        _expu_kernel,
        grid=(n // tm, EXP_NCHUNK + 1),
        in_specs=[pl.BlockSpec((tm, D_MODEL), lambda i, j: (i, 0)),
                  pl.BlockSpec((D_MODEL, EXP_CHUNK), lambda i, j: (0, jnp.minimum(j, last))),
                  pl.BlockSpec((tm, nslot), lambda i, j: (i, 0)),
                  pl.BlockSpec((tm, nslot), lambda i, j: (i, 0))],
        out_specs=pl.BlockSpec((tm, nslot), lambda i, j: (i, 0)),
        out_shape=jax.ShapeDtypeStruct((n, nslot), F32),
        scratch_shapes=[pltpu.VMEM((tm, EXP_CHUNK), F32), pltpu.VMEM((tm, EXP_CHUNK), F32)],
        compiler_params=_cparams(("parallel", "arbitrary")),
        name="peer_expert_u",
    )(xn, ut_b, aidx, bidx)


SLAB_PAD = SUBLANES
SLAB_TOKENS = 32


def _expv_kernel(z_ref, w_ref, a_ref, b_ref, v_ref, h_ref, o_ref, s_ref, acc_ref, c_ref):
    tm = z_ref.shape[0]
    nk = PEER_NKEYS
    pitch = tm + SLAB_PAD
    j = pl.program_id(1)

    @pl.when(j == 0)
    def _():
        c_ref[...] = w_ref[...] * _gelu(z_ref[...])
        sub = lax.broadcasted_iota(I32, (nk, nk), 0)
        dn = (((1,), (1,)), ((), ()))

        def toks(g, carry):
            base = g * SLAB_TOKENS
            for t in range(SLAB_TOKENS):
                n = base + t
                arow = a_ref[pl.ds(n, 1), :]
                brow = b_ref[pl.ds(n, 1), :]
                crow = c_ref[pl.ds(n, 1), :]
                oat = jnp.where(arow == sub, 1.0, 0.0).astype(BF16)
                wobt = jnp.where(brow == sub, crow, 0.0).astype(BF16)
                s_tok = lax.dot_general(oat, wobt, dn, preferred_element_type=F32)
                s_ref[pl.ds(n, nk, stride=pitch), :] = s_tok
            return carry

        lax.fori_loop(0, tm // SLAB_TOKENS, toks, 0)

    groups = EXP_CHUNK // nk
    slabs = []
    for t in range(groups):
        r0 = pl.multiple_of((j * groups + t) * pitch, SUBLANES)
        slabs.append(s_ref[pl.ds(r0, tm), :].astype(BF16))
    part = jnp.dot(jnp.concatenate(slabs, axis=1), v_ref[...], preferred_element_type=F32)

    @pl.when(j == 0)
    def _():
        acc_ref[...] = part

    @pl.when(j > 0)
    def _():
        acc_ref[...] += part

    @pl.when(j == pl.num_programs(1) - 1)
    def _():
        o_ref[...] = h_ref[...] + acc_ref[...]


def _expert_v(zsel, w, aidx, bidx, v_b, h1, tm=256):
    n = zsel.shape[0]
    nexp = v_b.shape[0]
    nslot = PEER_HEADS * PEER_TOPK
    slot = lambda: pl.BlockSpec((tm, nslot), lambda i, j: (i, 0))
    return pl.pallas_call(
        _expv_kernel,
        grid=(n // tm, nexp // EXP_CHUNK),
        in_specs=[slot(), slot(), slot(), slot(),
                  pl.BlockSpec((EXP_CHUNK, D_MODEL), lambda i, j: (j, 0)),
                  pl.BlockSpec((tm, D_MODEL), lambda i, j: (i, 0))],
        out_specs=pl.BlockSpec((tm, D_MODEL), lambda i, j: (i, 0)),
        out_shape=jax.ShapeDtypeStruct((n, D_MODEL), F32),
        scratch_shapes=[pltpu.VMEM((PEER_NKEYS * (tm + SLAB_PAD), PEER_NKEYS), F32),
                        pltpu.VMEM((tm, D_MODEL), F32),
                        pltpu.VMEM((tm, nslot), F32)],
        compiler_params=_cparams(("parallel", "arbitrary")),
        name="peer_expert_v",
    )(zsel, w, aidx, bidx, v_b, h1)


def _ple_kernel(h_ref, p_ref, gp_ref, wg_ref, wp_ref, gf_ref, o_ref):
    h = h_ref[...]
    xn = _rms(h, gp_ref[...]).astype(BF16)
    gate = jax.nn.sigmoid(jnp.dot(xn, wg_ref[...], preferred_element_type=F32))
    proj = jnp.dot(p_ref[...].astype(BF16), wp_ref[...], preferred_element_type=F32)
    o_ref[...] = _rms(h + gate * proj, gf_ref[...])


def _ple(h2, p2, g_ple, wg_b, wp_b, g_final, tm=512):
    n = h2.shape[0]
    return pl.pallas_call(
        _ple_kernel,
        grid=(n // tm,),
        in_specs=[pl.BlockSpec((tm, D_MODEL), lambda i: (i, 0)),
                  pl.BlockSpec((tm, PLE_DIM), lambda i: (i, 0)),
                  pl.BlockSpec((1, D_MODEL), lambda i: (0, 0)),
                  pl.BlockSpec((D_MODEL, D_MODEL), lambda i: (0, 0)),
                  pl.BlockSpec((PLE_DIM, D_MODEL), lambda i: (0, 0)),
                  pl.BlockSpec((1, D_MODEL), lambda i: (0, 0))],
        out_specs=pl.BlockSpec((tm, D_MODEL), lambda i: (i, 0)),
        out_shape=jax.ShapeDtypeStruct((n, D_MODEL), F32),
        compiler_params=_cparams(("parallel",)),
        name="ple_final",
    )(h2, p2, g_ple, wg_b, wp_b, g_final)


def _block_diag(w):
    g, d, _ = w.shape
    eye = jnp.eye(g, dtype=w.dtype)
    return (eye[:, None, :, None] * w[:, :, None, :]).reshape(g * d, g * d)


def kernel(x, p, g_mix, w_in, conv_w, conv_b, w_rg, b_rg, w_ig, b_ig, lru_lambda, g_lru_out,
           g_attn_out, w_out, rel_bias, g_ffn, peer_wq, peer_k1, peer_k2, peer_u, peer_v,
           g_ple, w_ple_gate, w_ple_proj, g_final):
    batch, seq, dm = x.shape
    n = batch * seq
    assert w_in.shape[0] == 1, "single-layer problem: the last kernel fuses the final norm"
    row = lambda v: v.reshape(1, -1).astype(F32)
    btab = _bias_tables(rel_bias)
    h = x.reshape(n, dm)
    for i in range(1):
        xl, gate, q, k, v = _inproj(h, row(g_mix[i]), w_in[i].astype(BF16))
        y_lru = _lru(xl, gate, conv_w[i], row(conv_b[i]),
                     _block_diag(w_rg[i]).astype(BF16), row(b_rg[i]),
                     _block_diag(w_ig[i]).astype(BF16), row(b_ig[i]),
                     row(lru_lambda[i]), row(g_lru_out[i]), batch, seq)
        y_att = _attention(q, k, v, btab, batch, seq)
        h1, xn = _outproj(y_lru, y_att, h, row(g_attn_out[i]), w_out[i].astype(BF16),
                          row(g_ffn[i]))
        aidx, bidx, gates = _route(xn, peer_wq[i].T.astype(BF16), peer_k1[i].astype(BF16),
                                   peer_k2[i].astype(BF16))
        zsel = _expert_u(xn, peer_u[i].T.astype(BF16), aidx, bidx)
        h2 = _expert_v(zsel, gates, aidx, bidx, peer_v[i].astype(BF16), h1)
        h = _ple(h2, p[i].reshape(n, -1), row(g_ple[i]), w_ple_gate[i].astype(BF16),
                 w_ple_proj[i].astype(BF16), g_final.reshape(1, -1))
    return h.reshape(batch, seq, dm)
```

```python
import functools
import math

import jax
import jax.numpy as jnp
import numpy as np
from jax import lax
from jax.experimental import pallas as pl
from jax.experimental.pallas import tpu as pltpu

F32 = jnp.float32
BF16 = jnp.bfloat16
I32 = jnp.int32

D_MODEL = 1024
PLE_DIM = 256
D_LRU = 512
LRU_BLOCKS = 8
CONV_WIDTH = 4
LRU_C = 8.0
D_ATTN = 512
N_ATT_HEADS = 8
HEAD_DIM = 64
DILATED_GROUPS = ((128, 1), (512, 4), (2048, 16))
ATT_BLOCK = 128
REL_BUCKETS = 32
REL_MAX_DIST = 2048
PEER_HEADS = 8
PEER_NKEYS = 128
PEER_QDIM = 256
PEER_TOPK = 16
NORM_EPS = 1e-6

LANES = 128
SUBLANES = 8
NEG_BIG = -1e30
VMEM_LIMIT = 56 * 1024 * 1024


def _cparams(sem, flags=None):
    return pltpu.CompilerParams(dimension_semantics=sem, vmem_limit_bytes=VMEM_LIMIT, flags=flags)


def _rms(x, g):
    ms = jnp.mean(x * x, axis=-1, keepdims=True)
    return (x * lax.rsqrt(ms + NORM_EPS)) * g


def _gelu(x):
    c = math.sqrt(2.0 / math.pi)
    return x * (0.5 * (1.0 + jnp.tanh(c * (x + 0.044715 * (x * x * x)))))


def _inproj_kernel(x_ref, g_ref, w_ref, xl_ref, gate_ref, q_ref, k_ref, v_ref):
    u = _rms(x_ref[...], g_ref[...]).astype(BF16)
    outs = (xl_ref, gate_ref, q_ref, k_ref, v_ref)
    for i, o_ref in enumerate(outs):
        z = jnp.dot(u, w_ref[:, i * 512:(i + 1) * 512], preferred_element_type=F32)
        if o_ref is q_ref:
            z = z * (HEAD_DIM ** -0.5)
        o_ref[...] = z


def _inproj(x2, g_mix, w_in_b, tm=512):
    n = x2.shape[0]
    blk = lambda: pl.BlockSpec((tm, 512), lambda i: (i, 0))
    return pl.pallas_call(
        _inproj_kernel,
        grid=(n // tm,),
        in_specs=[pl.BlockSpec((tm, D_MODEL), lambda i: (i, 0)),
                  pl.BlockSpec((1, D_MODEL), lambda i: (0, 0)),
                  pl.BlockSpec((D_MODEL, 2560), lambda i: (0, 0))],
        out_specs=[blk() for _ in range(5)],
        out_shape=[jax.ShapeDtypeStruct((n, 512), F32) for _ in range(5)],
        compiler_params=_cparams(("parallel",)),
        name="inproj",
    )(x2, g_mix, w_in_b)


LRU_CHUNK = 64


def _lru_kernel(xl_ref, gate_ref, cw_ref, cb_ref, wrg_ref, brg_ref, wig_ref, big_ref,
                lam_ref, g_ref, o_ref, xp_ref, h_ref):
    T = xl_ref.shape[0]
    tc = LRU_CHUNK
    xp_ref[0:SUBLANES, :] = jnp.zeros((SUBLANES, D_LRU), F32)
    xp_ref[SUBLANES:SUBLANES + T, :] = xl_ref[...]
    h_ref[...] = jnp.zeros_like(h_ref)
    nlam = -lam_ref[...]
    sp = jnp.maximum(nlam, 0.0) + jnp.log1p(jnp.exp(-jnp.abs(nlam)))
    row = lax.broadcasted_iota(I32, (tc, D_LRU), 0)

    def chunk(c, carry):
        t0 = pl.multiple_of(c * tc, tc)
        xa = xp_ref[pl.ds(t0, tc + SUBLANES), :]
        xc = cb_ref[...] + cw_ref[CONV_WIDTH - 1:CONV_WIDTH, :] * xa[SUBLANES:, :]
        for j in range(CONV_WIDTH - 1):
            sh = pltpu.roll(xa, CONV_WIDTH - 1 - j, axis=0)[SUBLANES:, :]
            xc = xc + cw_ref[j:j + 1, :] * sh
        xcb = xc.astype(BF16)
        r = jax.nn.sigmoid(jnp.dot(xcb, wrg_ref[...], preferred_element_type=F32) + brg_ref[...])
        ig = jax.nn.sigmoid(jnp.dot(xcb, wig_ref[...], preferred_element_type=F32) + big_ref[...])
        log_a = (-LRU_C) * r * sp
        a = jnp.exp(log_a)
        b = jnp.sqrt(-jnp.tanh(log_a) * (a * a + 1.0)) * (ig * xc)
        d = 1
        while d < tc:
            keep = row >= d
            a_sh = jnp.where(keep, pltpu.roll(a, d, axis=0), 1.0)
            b_sh = jnp.where(keep, pltpu.roll(b, d, axis=0), 0.0)
            b = a * b_sh + b
            a = a * a_sh
            d *= 2
        h = b + a * h_ref[...]
        h_ref[...] = h[tc - 1:tc, :]
        y = h * _gelu(gate_ref[pl.ds(t0, tc), :])
        o_ref[pl.ds(t0, tc), :] = _rms(y, g_ref[...]).astype(BF16)
        return carry

    lax.fori_loop(0, T // tc, chunk, 0)


def _lru(xl, gate, conv_w, conv_b, wrg_bd, b_rg, wig_bd, b_ig, lam, g_lru, batch, seq):
    vec = lambda: pl.BlockSpec((1, D_LRU), lambda b: (0, 0))
    mat = lambda: pl.BlockSpec((D_LRU, D_LRU), lambda b: (0, 0))
    return pl.pallas_call(
        _lru_kernel,
        grid=(batch,),
        in_specs=[pl.BlockSpec((seq, D_LRU), lambda b: (b, 0)),
                  pl.BlockSpec((seq, D_LRU), lambda b: (b, 0)),
                  pl.BlockSpec((CONV_WIDTH, D_LRU), lambda b: (0, 0)),
                  vec(), mat(), vec(), mat(), vec(), vec(), vec()],
        out_specs=pl.BlockSpec((seq, D_LRU), lambda b: (b, 0)),
        out_shape=jax.ShapeDtypeStruct((batch * seq, D_LRU), BF16),
        scratch_shapes=[pltpu.VMEM((seq + SUBLANES, D_LRU), F32),
                        pltpu.VMEM((1, D_LRU), F32)],
        compiler_params=_cparams(("parallel",)),
        name="rglru",
    )(xl, gate, conv_w, conv_b, wrg_bd, b_rg, wig_bd, b_ig, lam, g_lru)


ATT_UNROLL = 4


def _attend(qf, kf, vf, bias_a, bias_b, is_a):
    kb = kf.astype(BF16)
    vb = vf.astype(BF16)
    dn = (((1,), (1,)), ((), ()))
    outs = []
    for q_h, bias in ((jnp.where(is_a, qf, 0.0), bias_a), (jnp.where(is_a, 0.0, qf), bias_b)):
        s = lax.dot_general(q_h.astype(BF16), kb, dn, preferred_element_type=F32) + bias
        m = jnp.max(s, axis=-1, keepdims=True)
        e = jnp.exp(s - m)
        den = jnp.sum(e, axis=-1, keepdims=True)
        num = jnp.dot(e.astype(BF16), vb, preferred_element_type=F32)
        outs.append((num, m, den))
    (na, ma, da), (nb, mb, db) = outs
    shape = na.shape
    return (jnp.where(is_a, na, nb),
            jnp.where(is_a, jnp.broadcast_to(ma, shape), jnp.broadcast_to(mb, shape)),
            jnp.where(is_a, jnp.broadcast_to(da, shape), jnp.broadcast_to(db, shape)))


def _attn_kernel(q_ref, k_ref, v_ref, bt_ref, o_ref, num_ref, m_ref, den_ref):
    T = q_ref.shape[0]
    blk = ATT_BLOCK
    is_a = lax.broadcasted_iota(I32, (blk, LANES), 1) < HEAD_DIM

    def rows(start, size, d):
        return pl.ds(start, size) if d == 1 else pl.ds(start, size, stride=d)

    def first_block(br, d, r):
        sl = rows(r, blk, d)
        out = _attend(q_ref[sl, :], k_ref[sl, :], v_ref[sl, :],
                      bt_ref[br, 0, :, blk:], bt_ref[br, 1, :, blk:], is_a)
        for ref, val in zip((num_ref, m_ref, den_ref), out):
            ref[br, sl, :] = val

    def later_block(br, d, r, n):
        qs = rows(r + n * (blk * d), blk, d)
        ks = rows(r + (n - 1) * (blk * d), 2 * blk, d)
        out = _attend(q_ref[qs, :], k_ref[ks, :], v_ref[ks, :],
                      bt_ref[br, 0], bt_ref[br, 1], is_a)
        for ref, val in zip((num_ref, m_ref, den_ref), out):
            ref[br, qs, :] = val

    for br, (_, d) in enumerate(DILATED_GROUPS):
        nb = T // (d * blk)
        def firsts(r, carry, br=br, d=d):
            first_block(br, d, r)
            return carry

        def laters(i, carry, br=br, d=d, nb=nb):
            r = lax.div(i, nb - 1)
            later_block(br, d, r, 1 + lax.rem(i, nb - 1))
            return carry

        if d == 1:
            first_block(br, d, 0)
        else:
            lax.fori_loop(0, d, firsts, 0, unroll=ATT_UNROLL)
        if nb > 1:
            lax.fori_loop(0, d * (nb - 1), laters, 0, unroll=ATT_UNROLL)

    def merge(c, carry):
        sl = pl.ds(pl.multiple_of(c * blk, blk), blk)
        m0, m1, m2 = m_ref[0, sl, :], m_ref[1, sl, :], m_ref[2, sl, :]
        mm = jnp.maximum(jnp.maximum(m0, m1), m2)
        w0, w1, w2 = jnp.exp(m0 - mm), jnp.exp(m1 - mm), jnp.exp(m2 - mm)
        num = w0 * num_ref[0, sl, :] + w1 * num_ref[1, sl, :] + w2 * num_ref[2, sl, :]
        den = w0 * den_ref[0, sl, :] + w1 * den_ref[1, sl, :] + w2 * den_ref[2, sl, :]
        o_ref[sl, :] = num / den
        return carry

    lax.fori_loop(0, T // blk, merge, 0)


def _attention(q, k, v, btab, batch, seq):
    npairs = D_ATTN // LANES
    qkv = lambda: pl.BlockSpec((seq, LANES), lambda b, p: (b, p))
    return pl.pallas_call(
        _attn_kernel,
        grid=(batch, npairs),
        in_specs=[qkv(), qkv(), qkv(),
                  pl.BlockSpec((3, 2, ATT_BLOCK, 2 * ATT_BLOCK), lambda b, p: (0, p, 0, 0))],
        out_specs=pl.BlockSpec((seq, LANES), lambda b, p: (b, p)),
        out_shape=jax.ShapeDtypeStruct((batch * seq, D_ATTN), F32),
        scratch_shapes=[pltpu.VMEM((3, seq, LANES), F32) for _ in range(3)],
        compiler_params=_cparams(("parallel", "parallel")),
        name="dilated_attn",
    )(q, k, v, btab)


def _bias_tables(rel_bias):
    tabs = []
    for window, d in DILATED_GROUPS:
        w_sub = window // d
        qi = jnp.arange(ATT_BLOCK)[:, None]
        kj = jnp.arange(2 * ATT_BLOCK)[None, :]
        dist = qi + ATT_BLOCK - kj
        max_exact = REL_BUCKETS // 2
        nn = jnp.maximum(dist * d, 0)
        nf = jnp.maximum(nn, 1).astype(F32)
        large = max_exact + (jnp.log(nf / max_exact) / math.log(REL_MAX_DIST / max_exact)
                             * (REL_BUCKETS - max_exact)).astype(I32)
        large = jnp.minimum(large, REL_BUCKETS - 1)
        bucket = jnp.where(nn < max_exact, nn, large)
        rb = rel_bias.astype(F32)
        bias = jnp.zeros((N_ATT_HEADS,) + bucket.shape, F32)
        for bkt in range(REL_BUCKETS):
            bias = jnp.where(bucket[None] == bkt, rb[bkt][:, None, None], bias)
        ok = (dist >= 0) & (dist <= w_sub)
        tabs.append(jnp.where(ok[None], bias, NEG_BIG))
    return jnp.stack(tabs, axis=0)


def _outproj_kernel(yl_ref, ya_ref, x_ref, ga_ref, w_ref, gf_ref, h_ref, xn_ref):
    ya = _rms(ya_ref[...], ga_ref[...]).astype(BF16)
    acc = jnp.dot(yl_ref[...], w_ref[0:D_LRU, :], preferred_element_type=F32)
    acc = acc + jnp.dot(ya, w_ref[D_LRU:, :], preferred_element_type=F32)
    h = x_ref[...] + acc
    h_ref[...] = h
    xn_ref[...] = _rms(h, gf_ref[...]).astype(BF16)


def _outproj(y_lru, y_att, x2, g_att, w_out_b, g_ffn, tm=512):
    n = x2.shape[0]
    return pl.pallas_call(
        _outproj_kernel,
        grid=(n // tm,),
        in_specs=[pl.BlockSpec((tm, D_LRU), lambda i: (i, 0)),
                  pl.BlockSpec((tm, D_ATTN), lambda i: (i, 0)),
                  pl.BlockSpec((tm, D_MODEL), lambda i: (i, 0)),
                  pl.BlockSpec((1, D_ATTN), lambda i: (0, 0)),
                  pl.BlockSpec((D_MODEL, D_MODEL), lambda i: (0, 0)),
                  pl.BlockSpec((1, D_MODEL), lambda i: (0, 0))],
        out_specs=[pl.BlockSpec((tm, D_MODEL), lambda i: (i, 0)),
                   pl.BlockSpec((tm, D_MODEL), lambda i: (i, 0))],
        out_shape=[jax.ShapeDtypeStruct((n, D_MODEL), F32),
                   jax.ShapeDtypeStruct((n, D_MODEL), BF16)],
        compiler_params=_cparams(("parallel",)),
        name="outproj",
    )(y_lru, y_att, x2, g_att, w_out_b, g_ffn)


ROUTE_TM = 128
_CAND_GROUPS = (("col", 0, 0, 8), ("col", 0, 8, 16), ("row", 0, 1, 8), ("row", 0, 8, 16),
                ("col", 1, 1, 8), ("row", 1, 2, 8), ("col", 2, 2, 5), ("row", 2, 3, 5),
                ("col", 3, 3, 4))


def _tree(op, xs):
    xs = list(xs)
    while len(xs) > 1:
        xs = [op(xs[i], xs[i + 1]) for i in range(0, len(xs) - 1, 2)] + (xs[-1:] if len(xs) % 2 else [])
    return xs[0]


def _extract_topk_keys(chains, sub_f, val_ref, id_ref):
    nv = PEER_NKEYS // SUBLANES
    big = float(PEER_NKEYS)

    def rnd(k, states):
        nxt = []
        for c, parts in enumerate(states):
            m = jnp.max(_tree(jnp.maximum, parts), axis=0, keepdims=True)
            first = _tree(jnp.minimum, [jnp.where(p == m, float(SUBLANES * v), big)
                                        for v, p in enumerate(parts)])
            sel = jnp.min(first + sub_f, axis=0, keepdims=True)
            val_ref[c, pl.ds(k, 1), :] = m
            id_ref[c, pl.ds(k, 1), :] = sel
            off = sel - sub_f
            nxt.append(tuple(jnp.where(off == float(SUBLANES * v), NEG_BIG, p)
                             for v, p in enumerate(parts)))
        return tuple(nxt)

    init = tuple(tuple(s[v * SUBLANES:(v + 1) * SUBLANES, :] for v in range(nv)) for s in chains)
    lax.fori_loop(0, PEER_TOPK, rnd, init)


def _extract_topk(chains, ids, big, val_ref, id_ref):
    def rnd(k, states):
        nxt = []
        for c, s in enumerate(states):
            m = jnp.max(s, axis=0, keepdims=True)
            sel = jnp.min(jnp.where(s == m, ids, big), axis=0, keepdims=True)
            val_ref[c, pl.ds(k, 1), :] = m
            id_ref[c, pl.ds(k, 1), :] = sel
            nxt.append(jnp.where(ids == sel, NEG_BIG, s))
        return tuple(nxt)

    lax.fori_loop(0, PEER_TOPK, rnd, tuple(chains))


def _route_kernel(xn_ref, wqt_ref, k1_ref, k2_ref, a_ref, b_ref, w_ref,
                  qt_ref, tv_ref, ti_ref, cv_ref, cp_ref, oa_ref, ob_ref, ow_ref):
    tm = xn_ref.shape[0]
    half = PEER_QDIM // 2
    dn = (((1,), (1,)), ((), ()))
    qt_ref[...] = lax.dot_general(wqt_ref[...], xn_ref[...], dn, preferred_element_type=F32)
    sub = lax.broadcasted_iota(I32, (SUBLANES, tm), 0)
    sub_f = sub.astype(F32)

    def scores(k_ref, row0):
        qs = qt_ref[pl.ds(pl.multiple_of(row0, half), half), :].astype(BF16)
        return jnp.dot(k_ref[...], qs, preferred_element_type=F32)

    def candidates(c1, c2):
        cands, poss = [], []
        for kind, fixed, lo, hi in _CAND_GROUPS:
            base = (lo // SUBLANES) * SUBLANES
            rng = sub + base
            valid = (rng >= lo) & (rng < hi)
            if kind == "col":
                val = tv_ref[c1, base:base + SUBLANES, :] + tv_ref[c2, fixed:fixed + 1, :]
                pos = rng * PEER_TOPK + fixed
            else:
                val = tv_ref[c1, fixed:fixed + 1, :] + tv_ref[c2, base:base + SUBLANES, :]
                pos = fixed * PEER_TOPK + rng
            cands.append(jnp.where(valid, val, NEG_BIG))
            poss.append(pos.astype(F32))
        return jnp.concatenate(cands, axis=0), jnp.concatenate(poss, axis=0)

    def head_pair(hp, carry):
        for u in range(2):
            row0 = (2 * hp + u) * PEER_QDIM
            _extract_topk_keys([scores(k1_ref, row0), scores(k2_ref, row0 + half)], sub_f,
                               tv_ref.at[pl.ds(2 * u, 2)], ti_ref.at[pl.ds(2 * u, 2)])
        cand0, pos = candidates(0, 1)
        cand1, _ = candidates(2, 3)
        _extract_topk([cand0, cand1], pos, float(PEER_TOPK * PEER_TOPK), cv_ref, cp_ref)
        for u in range(2):
            top_s = cv_ref[u]
            top_p = cp_ref[u].astype(I32)
            ri = lax.shift_right_logical(top_p, 4)
            rj = lax.bitwise_and(top_p, PEER_TOPK - 1)
            a_sel = jnp.zeros((PEER_TOPK, tm), F32)
            b_sel = jnp.zeros((PEER_TOPK, tm), F32)
            for r in range(PEER_TOPK):
                a_sel = jnp.where(ri == r, ti_ref[2 * u, r:r + 1, :], a_sel)
                b_sel = jnp.where(rj == r, ti_ref[2 * u + 1, r:r + 1, :], b_sel)
            e = jnp.exp(top_s - top_s[0:1, :])
            g = e / jnp.sum(e, axis=0, keepdims=True)
            r0 = pl.multiple_of((2 * hp + u) * PEER_TOPK, PEER_TOPK)
            oa_ref[pl.ds(r0, PEER_TOPK), :] = a_sel
            ob_ref[pl.ds(r0, PEER_TOPK), :] = b_sel
            ow_ref[pl.ds(r0, PEER_TOPK), :] = g
        return carry

    lax.fori_loop(0, PEER_HEADS // 2, head_pair, 0)
    a_ref[...] = oa_ref[...].T.astype(I32)
    b_ref[...] = ob_ref[...].T.astype(I32)
    w_ref[...] = ow_ref[...].T


def _route(xn, wqt_b, k1_b, k2_b):
    n = xn.shape[0]
    tm = ROUTE_TM
    nslot = PEER_HEADS * PEER_TOPK
    out = lambda: pl.BlockSpec((tm, nslot), lambda i: (i, 0))
    return pl.pallas_call(
        _route_kernel,
        grid=(n // tm,),
        in_specs=[pl.BlockSpec((tm, D_MODEL), lambda i: (i, 0)),
                  pl.BlockSpec((PEER_HEADS * PEER_QDIM, D_MODEL), lambda i: (0, 0)),
                  pl.BlockSpec((PEER_NKEYS, PEER_QDIM // 2), lambda i: (0, 0)),
                  pl.BlockSpec((PEER_NKEYS, PEER_QDIM // 2), lambda i: (0, 0))],
        out_specs=[out(), out(), out()],
        out_shape=[jax.ShapeDtypeStruct((n, nslot), I32),
                   jax.ShapeDtypeStruct((n, nslot), I32),
                   jax.ShapeDtypeStruct((n, nslot), F32)],
        scratch_shapes=[pltpu.VMEM((PEER_HEADS * PEER_QDIM, tm), F32),
                        pltpu.VMEM((4, PEER_TOPK, tm), F32), pltpu.VMEM((4, PEER_TOPK, tm), F32),
                        pltpu.VMEM((2, PEER_TOPK, tm), F32), pltpu.VMEM((2, PEER_TOPK, tm), F32),
                        pltpu.VMEM((nslot, tm), F32), pltpu.VMEM((nslot, tm), F32),
                        pltpu.VMEM((nslot, tm), F32)],
        compiler_params=_cparams(("parallel",)),
        name="peer_route",
    )(xn, wqt_b, k1_b, k2_b)


EXP_CHUNK = 2048
EXP_NCHUNK = PEER_NKEYS * PEER_NKEYS // EXP_CHUNK
PAIR = 2 * PEER_NKEYS


def _expu_kernel(xn_ref, ut_ref, a_ref, b_ref, z_ref, buf0_ref, buf1_ref):
    tm = xn_ref.shape[0]
    j = pl.program_id(1)
    nj = pl.num_programs(1) - 1
    groups_per_chunk = EXP_CHUNK // PEER_NKEYS

    def matmul_into(buf_ref):
        buf_ref[...] = jnp.dot(xn_ref[...], ut_ref[...], preferred_element_type=F32)

    def gather_from(buf_ref):
        a0 = (j - 1) * groups_per_chunk
        for g in range(tm // SUBLANES):
            rs = slice(g * SUBLANES, (g + 1) * SUBLANES)
            bi = b_ref[rs, :]
            ai = a_ref[rs, :] - a0
            acc = z_ref[rs, :]
            for t in range(groups_per_chunk):
                got = jnp.take_along_axis(buf_ref[rs, t * PEER_NKEYS:(t + 1) * PEER_NKEYS], bi,
                                          axis=1, mode="promise_in_bounds")
                acc = jnp.where(ai == t, got, acc)
            z_ref[rs, :] = acc

    odd = lax.rem(j, 2) == 1
    inner = jnp.logical_and(j > 0, j < nj)

    @pl.when(j == 0)
    def _():
        z_ref[...] = jnp.zeros_like(z_ref)
        matmul_into(buf0_ref)

    @pl.when(jnp.logical_and(inner, odd))
    def _():
        gather_from(buf0_ref)
        matmul_into(buf1_ref)

    @pl.when(jnp.logical_and(inner, jnp.logical_not(odd)))
    def _():
        gather_from(buf1_ref)
        matmul_into(buf0_ref)

    @pl.when(j == nj)
    def _():
        gather_from(buf1_ref if (EXP_NCHUNK - 1) % 2 == 1 else buf0_ref)


def _expert_u(xn, ut_b, aidx, bidx, tm=512):
    n = xn.shape[0]
    nexp = ut_b.shape[1]
    assert nexp == EXP_NCHUNK * EXP_CHUNK
    nslot = PEER_HEADS * PEER_TOPK
    last = EXP_NCHUNK - 1
    return pl.pallas_call(
        _expu_kernel,
        grid=(n // tm, EXP_NCHUNK + 1),
        in_specs=[pl.BlockSpec((tm, D_MODEL), lambda i, j: (i, 0)),
                  pl.BlockSpec((D_MODEL, EXP_CHUNK), lambda i, j: (0, jnp.minimum(j, last))),
                  pl.BlockSpec((tm, nslot), lambda i, j: (i, 0)),
                  pl.BlockSpec((tm, nslot), lambda i, j: (i, 0))],
        out_specs=pl.BlockSpec((tm, nslot), lambda i, j: (i, 0)),
        out_shape=jax.ShapeDtypeStruct((n, nslot), F32),
        scratch_shapes=[pltpu.VMEM((tm, EXP_CHUNK), F32), pltpu.VMEM((tm, EXP_CHUNK), F32)],
        compiler_params=_cparams(("parallel", "arbitrary")),
        name="peer_expert_u",
    )(xn, ut_b, aidx, bidx)


SLAB_PAD = SUBLANES
SLAB_TOKENS = 32
U32 = jnp.uint32


def _bf16_bits(x):
    return pltpu.bitcast(x.astype(BF16).astype(F32), U32)


def _expv_kernel(z_ref, w_ref, a_ref, b_ref, v_ref, h_ref, o_ref, s_ref, acc_ref, c_ref):
    tm = z_ref.shape[0]
    nk = PEER_NKEYS
    pitch = tm // 2 + SLAB_PAD
    j = pl.program_id(1)

    @pl.when(j == 0)
    def _():
        c_ref[...] = w_ref[...] * _gelu(z_ref[...])
        sub = lax.broadcasted_iota(I32, (nk, nk), 0)
        dn = (((1,), (1,)), ((), ()))

        def slab(n):
            arow = a_ref[pl.ds(n, 1), :]
            brow = b_ref[pl.ds(n, 1), :]
            crow = c_ref[pl.ds(n, 1), :]
            oat = jnp.where(arow == sub, 1.0, 0.0).astype(BF16)
            wobt = jnp.where(brow == sub, crow, 0.0).astype(BF16)
            return lax.dot_general(oat, wobt, dn, preferred_element_type=F32)

        def toks(g, carry):
            base = g * (SLAB_TOKENS // 2)
            for t in range(SLAB_TOKENS // 2):
                row = base + t
                lo = lax.shift_right_logical(_bf16_bits(slab(2 * row)), jnp.uint32(16))
                hi = _bf16_bits(slab(2 * row + 1))
                s_ref[pl.ds(row, nk, stride=pitch), :] = lo | hi
            return carry

        lax.fori_loop(0, tm // SLAB_TOKENS, toks, 0)

    groups = EXP_CHUNK // nk
    slabs = []
    for t in range(groups):
        r0 = pl.multiple_of((j * groups + t) * pitch, SUBLANES)
        slabs.append(pltpu.bitcast(s_ref[pl.ds(r0, tm // 2), :], BF16))
    part = jnp.dot(jnp.concatenate(slabs, axis=1), v_ref[...], preferred_element_type=F32)

    @pl.when(j == 0)
    def _():
        acc_ref[...] = part

    @pl.when(j > 0)
    def _():
        acc_ref[...] += part

    @pl.when(j == pl.num_programs(1) - 1)
    def _():
        o_ref[...] = h_ref[...] + acc_ref[...]


def _expert_v(zsel, w, aidx, bidx, v_b, h1, tm=512):
    n = zsel.shape[0]
    nexp = v_b.shape[0]
    nslot = PEER_HEADS * PEER_TOPK
    slot = lambda: pl.BlockSpec((tm, nslot), lambda i, j: (i, 0))
    return pl.pallas_call(
        _expv_kernel,
        grid=(n // tm, nexp // EXP_CHUNK),
        in_specs=[slot(), slot(), slot(), slot(),
                  pl.BlockSpec((EXP_CHUNK, D_MODEL), lambda i, j: (j, 0)),
                  pl.BlockSpec((tm, D_MODEL), lambda i, j: (i, 0))],
        out_specs=pl.BlockSpec((tm, D_MODEL), lambda i, j: (i, 0)),
        out_shape=jax.ShapeDtypeStruct((n, D_MODEL), F32),
        scratch_shapes=[pltpu.VMEM((PEER_NKEYS * (tm // 2 + SLAB_PAD), PEER_NKEYS), U32),
                        pltpu.VMEM((tm, D_MODEL), F32),
                        pltpu.VMEM((tm, nslot), F32)],
        compiler_params=_cparams(("parallel", "arbitrary")),
        name="peer_expert_v",
    )(zsel, w, aidx, bidx, v_b, h1)


def _ple_kernel(h_ref, p_ref, gp_ref, wg_ref, wp_ref, gf_ref, o_ref):
    h = h_ref[...]
    xn = _rms(h, gp_ref[...]).astype(BF16)
    gate = jax.nn.sigmoid(jnp.dot(xn, wg_ref[...], preferred_element_type=F32))
    proj = jnp.dot(p_ref[...].astype(BF16), wp_ref[...], preferred_element_type=F32)
    o_ref[...] = _rms(h + gate * proj, gf_ref[...])


def _ple(h2, p2, g_ple, wg_b, wp_b, g_final, tm=512):
    n = h2.shape[0]
    return pl.pallas_call(
        _ple_kernel,
        grid=(n // tm,),
        in_specs=[pl.BlockSpec((tm, D_MODEL), lambda i: (i, 0)),
                  pl.BlockSpec((tm, PLE_DIM), lambda i: (i, 0)),
                  pl.BlockSpec((1, D_MODEL), lambda i: (0, 0)),
                  pl.BlockSpec((D_MODEL, D_MODEL), lambda i: (0, 0)),
                  pl.BlockSpec((PLE_DIM, D_MODEL), lambda i: (0, 0)),
                  pl.BlockSpec((1, D_MODEL), lambda i: (0, 0))],
        out_specs=pl.BlockSpec((tm, D_MODEL), lambda i: (i, 0)),
        out_shape=jax.ShapeDtypeStruct((n, D_MODEL), F32),
        compiler_params=_cparams(("parallel",)),
        name="ple_final",
    )(h2, p2, g_ple, wg_b, wp_b, g_final)


def _block_diag(w):
    g, d, _ = w.shape
    eye = jnp.eye(g, dtype=w.dtype)
    return (eye[:, None, :, None] * w[:, :, None, :]).reshape(g * d, g * d)


def kernel(x, p, g_mix, w_in, conv_w, conv_b, w_rg, b_rg, w_ig, b_ig, lru_lambda, g_lru_out,
           g_attn_out, w_out, rel_bias, g_ffn, peer_wq, peer_k1, peer_k2, peer_u, peer_v,
           g_ple, w_ple_gate, w_ple_proj, g_final):
    batch, seq, dm = x.shape
    n = batch * seq
    assert w_in.shape[0] == 1, "single-layer problem: the last kernel fuses the final norm"
    row = lambda v: v.reshape(1, -1).astype(F32)
    btab = _bias_tables(rel_bias)
    h = x.reshape(n, dm)
    for i in range(1):
        xl, gate, q, k, v = _inproj(h, row(g_mix[i]), w_in[i].astype(BF16))
        y_lru = _lru(xl, gate, conv_w[i], row(conv_b[i]),
                     _block_diag(w_rg[i]).astype(BF16), row(b_rg[i]),
                     _block_diag(w_ig[i]).astype(BF16), row(b_ig[i]),
                     row(lru_lambda[i]), row(g_lru_out[i]), batch, seq)
        y_att = _attention(q, k, v, btab, batch, seq)
        h1, xn = _outproj(y_lru, y_att, h, row(g_attn_out[i]), w_out[i].astype(BF16),
                          row(g_ffn[i]))
        aidx, bidx, gates = _route(xn, peer_wq[i].T.astype(BF16), peer_k1[i].astype(BF16),
                                   peer_k2[i].astype(BF16))
        zsel = _expert_u(xn, peer_u[i].T.astype(BF16), aidx, bidx)
        h2 = _expert_v(zsel, gates, aidx, bidx, peer_v[i].astype(BF16), h1)
        h = _ple(h2, p[i].reshape(n, -1), row(g_ple[i]), w_ple_gate[i].astype(BF16),
                 w_ple_proj[i].astype(BF16), g_final.reshape(1, -1))
    return h.reshape(batch, seq, dm)
```

```python
import functools
import math

import jax
import jax.numpy as jnp
import numpy as np
from jax import lax
from jax.experimental import pallas as pl
from jax.experimental.pallas import tpu as pltpu

F32 = jnp.float32
BF16 = jnp.bfloat16
I32 = jnp.int32

D_MODEL = 1024
PLE_DIM = 256
D_LRU = 512
LRU_BLOCKS = 8
CONV_WIDTH = 4
LRU_C = 8.0
D_ATTN = 512
N_ATT_HEADS = 8
HEAD_DIM = 64
DILATED_GROUPS = ((128, 1), (512, 4), (2048, 16))
ATT_BLOCK = 128
REL_BUCKETS = 32
REL_MAX_DIST = 2048
PEER_HEADS = 8
PEER_NKEYS = 128
PEER_QDIM = 256
PEER_TOPK = 16
NORM_EPS = 1e-6

LANES = 128
SUBLANES = 8
NEG_BIG = -1e30
VMEM_LIMIT = 56 * 1024 * 1024


def _cparams(sem, flags=None):
    return pltpu.CompilerParams(dimension_semantics=sem, vmem_limit_bytes=VMEM_LIMIT, flags=flags)


def _rms(x, g):
    ms = jnp.mean(x * x, axis=-1, keepdims=True)
    return (x * lax.rsqrt(ms + NORM_EPS)) * g


def _gelu(x):
    c = math.sqrt(2.0 / math.pi)
    return x * (0.5 * (1.0 + jnp.tanh(c * (x + 0.044715 * (x * x * x)))))


def _inproj_kernel(x_ref, g_ref, w_ref, xl_ref, gate_ref, q_ref, k_ref, v_ref):
    u = _rms(x_ref[...], g_ref[...]).astype(BF16)
    outs = (xl_ref, gate_ref, q_ref, k_ref, v_ref)
    for i, o_ref in enumerate(outs):
        z = jnp.dot(u, w_ref[:, i * 512:(i + 1) * 512], preferred_element_type=F32)
        if o_ref is q_ref:
            z = z * (HEAD_DIM ** -0.5)
        o_ref[...] = z


def _inproj(x2, g_mix, w_in_b, tm=512):
    n = x2.shape[0]
    blk = lambda: pl.BlockSpec((tm, 512), lambda i: (i, 0))
    return pl.pallas_call(
        _inproj_kernel,
        grid=(n // tm,),
        in_specs=[pl.BlockSpec((tm, D_MODEL), lambda i: (i, 0)),
                  pl.BlockSpec((1, D_MODEL), lambda i: (0, 0)),
                  pl.BlockSpec((D_MODEL, 2560), lambda i: (0, 0))],
        out_specs=[blk() for _ in range(5)],
        out_shape=[jax.ShapeDtypeStruct((n, 512), F32) for _ in range(5)],
        compiler_params=_cparams(("parallel",)),
        name="inproj",
    )(x2, g_mix, w_in_b)


LRU_CHUNK = 64


def _lru_kernel(xl_ref, gate_ref, cw_ref, cb_ref, wrg_ref, brg_ref, wig_ref, big_ref,
                lam_ref, g_ref, o_ref, xp_ref, h_ref):
    T = xl_ref.shape[0]
    tc = LRU_CHUNK
    xp_ref[0:SUBLANES, :] = jnp.zeros((SUBLANES, D_LRU), F32)
    xp_ref[SUBLANES:SUBLANES + T, :] = xl_ref[...]
    h_ref[...] = jnp.zeros_like(h_ref)
    nlam = -lam_ref[...]
    sp = jnp.maximum(nlam, 0.0) + jnp.log1p(jnp.exp(-jnp.abs(nlam)))
    row = lax.broadcasted_iota(I32, (tc, D_LRU), 0)

    def chunk(c, carry):
        t0 = pl.multiple_of(c * tc, tc)
        xa = xp_ref[pl.ds(t0, tc + SUBLANES), :]
        xc = cb_ref[...] + cw_ref[CONV_WIDTH - 1:CONV_WIDTH, :] * xa[SUBLANES:, :]
        for j in range(CONV_WIDTH - 1):
            sh = pltpu.roll(xa, CONV_WIDTH - 1 - j, axis=0)[SUBLANES:, :]
            xc = xc + cw_ref[j:j + 1, :] * sh
        xcb = xc.astype(BF16)
        r = jax.nn.sigmoid(jnp.dot(xcb, wrg_ref[...], preferred_element_type=F32) + brg_ref[...])
        ig = jax.nn.sigmoid(jnp.dot(xcb, wig_ref[...], preferred_element_type=F32) + big_ref[...])
        log_a = (-LRU_C) * r * sp
        a = jnp.exp(log_a)
        b = jnp.sqrt(-jnp.tanh(log_a) * (a * a + 1.0)) * (ig * xc)
        d = 1
        while d < tc:
            keep = row >= d
            a_sh = jnp.where(keep, pltpu.roll(a, d, axis=0), 1.0)
            b_sh = jnp.where(keep, pltpu.roll(b, d, axis=0), 0.0)
            b = a * b_sh + b
            a = a * a_sh
            d *= 2
        h = b + a * h_ref[...]
        h_ref[...] = h[tc - 1:tc, :]
        y = h * _gelu(gate_ref[pl.ds(t0, tc), :])
        o_ref[pl.ds(t0, tc), :] = _rms(y, g_ref[...]).astype(BF16)
        return carry

    lax.fori_loop(0, T // tc, chunk, 0)


def _lru(xl, gate, conv_w, conv_b, wrg_bd, b_rg, wig_bd, b_ig, lam, g_lru, batch, seq):
    vec = lambda: pl.BlockSpec((1, D_LRU), lambda b: (0, 0))
    mat = lambda: pl.BlockSpec((D_LRU, D_LRU), lambda b: (0, 0))
    return pl.pallas_call(
        _lru_kernel,
        grid=(batch,),
        in_specs=[pl.BlockSpec((seq, D_LRU), lambda b: (b, 0)),
                  pl.BlockSpec((seq, D_LRU), lambda b: (b, 0)),
                  pl.BlockSpec((CONV_WIDTH, D_LRU), lambda b: (0, 0)),
                  vec(), mat(), vec(), mat(), vec(), vec(), vec()],
        out_specs=pl.BlockSpec((seq, D_LRU), lambda b: (b, 0)),
        out_shape=jax.ShapeDtypeStruct((batch * seq, D_LRU), BF16),
        scratch_shapes=[pltpu.VMEM((seq + SUBLANES, D_LRU), F32),
                        pltpu.VMEM((1, D_LRU), F32)],
        compiler_params=_cparams(("parallel",)),
        name="rglru",
    )(xl, gate, conv_w, conv_b, wrg_bd, b_rg, wig_bd, b_ig, lam, g_lru)


ATT_UNROLL = 8


def _attend(qf, kf, vf, bias_a, bias_b, is_a):
    kb = kf.astype(BF16)
    vb = vf.astype(BF16)
    dn = (((1,), (1,)), ((), ()))
    outs = []
    for q_h, bias in ((jnp.where(is_a, qf, 0.0), bias_a), (jnp.where(is_a, 0.0, qf), bias_b)):
        s = lax.dot_general(q_h.astype(BF16), kb, dn, preferred_element_type=F32) + bias
        m = jnp.max(s, axis=-1, keepdims=True)
        e = jnp.exp(s - m)
        den = jnp.sum(e, axis=-1, keepdims=True)
        num = jnp.dot(e.astype(BF16), vb, preferred_element_type=F32)
        outs.append((num, m, den))
    (na, ma, da), (nb, mb, db) = outs
    shape = na.shape
    return (jnp.where(is_a, na, nb),
            jnp.where(is_a, jnp.broadcast_to(ma, shape), jnp.broadcast_to(mb, shape)),
            jnp.where(is_a, jnp.broadcast_to(da, shape), jnp.broadcast_to(db, shape)))


def _attn_kernel(q_ref, k_ref, v_ref, bt_ref, o_ref, num_ref, m_ref, den_ref):
    T = q_ref.shape[0]
    blk = ATT_BLOCK
    is_a = lax.broadcasted_iota(I32, (blk, LANES), 1) < HEAD_DIM

    def rows(start, size, d):
        return pl.ds(start, size) if d == 1 else pl.ds(start, size, stride=d)

    def first_block(br, d, r):
        sl = rows(r, blk, d)
        pad = jnp.zeros((blk, LANES), F32)
        left = lax.broadcasted_iota(I32, (blk, 2 * blk), 1) < blk
        out = _attend(q_ref[sl, :], jnp.concatenate([pad, k_ref[sl, :]], axis=0),
                      jnp.concatenate([pad, v_ref[sl, :]], axis=0),
                      jnp.where(left, NEG_BIG, bt_ref[br, 0]),
                      jnp.where(left, NEG_BIG, bt_ref[br, 1]), is_a)
        for ref, val in zip((num_ref, m_ref, den_ref), out):
            ref[br, sl, :] = val

    def later_block(br, d, r, n):
        qs = rows(r + n * (blk * d), blk, d)
        ks = rows(r + (n - 1) * (blk * d), 2 * blk, d)
        out = _attend(q_ref[qs, :], k_ref[ks, :], v_ref[ks, :],
                      bt_ref[br, 0], bt_ref[br, 1], is_a)
        for ref, val in zip((num_ref, m_ref, den_ref), out):
            ref[br, qs, :] = val

    for br, (_, d) in enumerate(DILATED_GROUPS):
        nb = T // (d * blk)
        def firsts(r, carry, br=br, d=d):
            first_block(br, d, r)
            return carry

        def laters(i, carry, br=br, d=d, nb=nb):
            r = lax.div(i, nb - 1)
            later_block(br, d, r, 1 + lax.rem(i, nb - 1))
            return carry

        if d == 1:
            first_block(br, d, 0)
        else:
            lax.fori_loop(0, d, firsts, 0, unroll=ATT_UNROLL)
        if nb > 1:
            lax.fori_loop(0, d * (nb - 1), laters, 0, unroll=ATT_UNROLL)

    def merge(c, carry):
        sl = pl.ds(pl.multiple_of(c * blk, blk), blk)
        m0, m1, m2 = m_ref[0, sl, :], m_ref[1, sl, :], m_ref[2, sl, :]
        mm = jnp.maximum(jnp.maximum(m0, m1), m2)
        w0, w1, w2 = jnp.exp(m0 - mm), jnp.exp(m1 - mm), jnp.exp(m2 - mm)
        num = w0 * num_ref[0, sl, :] + w1 * num_ref[1, sl, :] + w2 * num_ref[2, sl, :]
        den = w0 * den_ref[0, sl, :] + w1 * den_ref[1, sl, :] + w2 * den_ref[2, sl, :]
        o_ref[sl, :] = num / den
        return carry

    lax.fori_loop(0, T // blk, merge, 0)


def _attention(q, k, v, btab, batch, seq):
    npairs = D_ATTN // LANES
    qkv = lambda: pl.BlockSpec((seq, LANES), lambda b, p: (b, p))
    return pl.pallas_call(
        _attn_kernel,
        grid=(batch, npairs),
        in_specs=[qkv(), qkv(), qkv(),
                  pl.BlockSpec((3, 2, ATT_BLOCK, 2 * ATT_BLOCK), lambda b, p: (0, p, 0, 0))],
        out_specs=pl.BlockSpec((seq, LANES), lambda b, p: (b, p)),
        out_shape=jax.ShapeDtypeStruct((batch * seq, D_ATTN), F32),
        scratch_shapes=[pltpu.VMEM((3, seq, LANES), F32) for _ in range(3)],
        compiler_params=_cparams(("parallel", "parallel")),
        name="dilated_attn",
    )(q, k, v, btab)


def _bias_tables(rel_bias):
    tabs = []
    for window, d in DILATED_GROUPS:
        w_sub = window // d
        qi = jnp.arange(ATT_BLOCK)[:, None]
        kj = jnp.arange(2 * ATT_BLOCK)[None, :]
        dist = qi + ATT_BLOCK - kj
        max_exact = REL_BUCKETS // 2
        nn = jnp.maximum(dist * d, 0)
        nf = jnp.maximum(nn, 1).astype(F32)
        large = max_exact + (jnp.log(nf / max_exact) / math.log(REL_MAX_DIST / max_exact)
                             * (REL_BUCKETS - max_exact)).astype(I32)
        large = jnp.minimum(large, REL_BUCKETS - 1)
        bucket = jnp.where(nn < max_exact, nn, large)
        rb = rel_bias.astype(F32)
        bias = jnp.zeros((N_ATT_HEADS,) + bucket.shape, F32)
        for bkt in range(REL_BUCKETS):
            bias = jnp.where(bucket[None] == bkt, rb[bkt][:, None, None], bias)
        ok = (dist >= 0) & (dist <= w_sub)
        tabs.append(jnp.where(ok[None], bias, NEG_BIG))
    return jnp.stack(tabs, axis=0)


def _outproj_kernel(yl_ref, ya_ref, x_ref, ga_ref, w_ref, gf_ref, h_ref, xn_ref):
    ya = _rms(ya_ref[...], ga_ref[...]).astype(BF16)
    acc = jnp.dot(yl_ref[...], w_ref[0:D_LRU, :], preferred_element_type=F32)
    acc = acc + jnp.dot(ya, w_ref[D_LRU:, :], preferred_element_type=F32)
    h = x_ref[...] + acc
    h_ref[...] = h
    xn_ref[...] = _rms(h, gf_ref[...]).astype(BF16)


def _outproj(y_lru, y_att, x2, g_att, w_out_b, g_ffn, tm=512):
    n = x2.shape[0]
    return pl.pallas_call(
        _outproj_kernel,
        grid=(n // tm,),
        in_specs=[pl.BlockSpec((tm, D_LRU), lambda i: (i, 0)),
                  pl.BlockSpec((tm, D_ATTN), lambda i: (i, 0)),
                  pl.BlockSpec((tm, D_MODEL), lambda i: (i, 0)),
                  pl.BlockSpec((1, D_ATTN), lambda i: (0, 0)),
                  pl.BlockSpec((D_MODEL, D_MODEL), lambda i: (0, 0)),
                  pl.BlockSpec((1, D_MODEL), lambda i: (0, 0))],
        out_specs=[pl.BlockSpec((tm, D_MODEL), lambda i: (i, 0)),
                   pl.BlockSpec((tm, D_MODEL), lambda i: (i, 0))],
        out_shape=[jax.ShapeDtypeStruct((n, D_MODEL), F32),
                   jax.ShapeDtypeStruct((n, D_MODEL), BF16)],
        compiler_params=_cparams(("parallel",)),
        name="outproj",
    )(y_lru, y_att, x2, g_att, w_out_b, g_ffn)


ROUTE_TM = 128
_CAND_GROUPS = (("col", 0, 0, 8), ("col", 0, 8, 16), ("row", 0, 1, 8), ("row", 0, 8, 16),
                ("col", 1, 1, 8), ("row", 1, 2, 8), ("col", 2, 2, 5), ("row", 2, 3, 5),
                ("col", 3, 3, 4))


def _tree(op, xs):
    xs = list(xs)
    while len(xs) > 1:
        xs = [op(xs[i], xs[i + 1]) for i in range(0, len(xs) - 1, 2)] + (xs[-1:] if len(xs) % 2 else [])
    return xs[0]


def _extract_topk_keys(chains, sub_f, val_ref, id_ref):
    nv = PEER_NKEYS // SUBLANES
    big = float(PEER_NKEYS)

    def rnd(k, states):
        nxt = []
        for c, parts in enumerate(states):
            m = jnp.max(_tree(jnp.maximum, parts), axis=0, keepdims=True)
            first = _tree(jnp.minimum, [jnp.where(p == m, float(SUBLANES * v), big)
                                        for v, p in enumerate(parts)])
            sel = jnp.min(first + sub_f, axis=0, keepdims=True)
            val_ref[c, pl.ds(k, 1), :] = m
            id_ref[c, pl.ds(k, 1), :] = sel
            off = sel - sub_f
            nxt.append(tuple(jnp.where(off == float(SUBLANES * v), NEG_BIG, p)
                             for v, p in enumerate(parts)))
        return tuple(nxt)

    init = tuple(tuple(s[v * SUBLANES:(v + 1) * SUBLANES, :] for v in range(nv)) for s in chains)
    lax.fori_loop(0, PEER_TOPK, rnd, init)


def _extract_topk(chains, ids, big, val_ref, id_ref):
    def rnd(k, states):
        nxt = []
        for c, s in enumerate(states):
            m = jnp.max(s, axis=0, keepdims=True)
            sel = jnp.min(jnp.where(s == m, ids, big), axis=0, keepdims=True)
            val_ref[c, pl.ds(k, 1), :] = m
            id_ref[c, pl.ds(k, 1), :] = sel
            nxt.append(jnp.where(ids == sel, NEG_BIG, s))
        return tuple(nxt)

    lax.fori_loop(0, PEER_TOPK, rnd, tuple(chains))


def _route_kernel(xn_ref, wqt_ref, k1_ref, k2_ref, a_ref, b_ref, w_ref,
                  qt_ref, tv_ref, ti_ref, cv_ref, cp_ref, oa_ref, ob_ref, ow_ref):
    tm = xn_ref.shape[0]
    half = PEER_QDIM // 2
    dn = (((1,), (1,)), ((), ()))
    qt_ref[...] = lax.dot_general(wqt_ref[...], xn_ref[...], dn, preferred_element_type=F32)
    sub = lax.broadcasted_iota(I32, (SUBLANES, tm), 0)
    sub_f = sub.astype(F32)

    def scores(k_ref, row0):
        qs = qt_ref[pl.ds(pl.multiple_of(row0, half), half), :].astype(BF16)
        return jnp.dot(k_ref[...], qs, preferred_element_type=F32)

    def candidates(c1, c2):
        cands, poss = [], []
        for kind, fixed, lo, hi in _CAND_GROUPS:
            base = (lo // SUBLANES) * SUBLANES
            rng = sub + base
            valid = (rng >= lo) & (rng < hi)
            if kind == "col":
                val = tv_ref[c1, base:base + SUBLANES, :] + tv_ref[c2, fixed:fixed + 1, :]
                pos = rng * PEER_TOPK + fixed
            else:
                val = tv_ref[c1, fixed:fixed + 1, :] + tv_ref[c2, base:base + SUBLANES, :]
                pos = fixed * PEER_TOPK + rng
            cands.append(jnp.where(valid, val, NEG_BIG))
            poss.append(pos.astype(F32))
        return jnp.concatenate(cands, axis=0), jnp.concatenate(poss, axis=0)

    def head_pair(hp, carry):
        for u in range(2):
            row0 = (2 * hp + u) * PEER_QDIM
            _extract_topk_keys([scores(k1_ref, row0), scores(k2_ref, row0 + half)], sub_f,
                               tv_ref.at[pl.ds(2 * u, 2)], ti_ref.at[pl.ds(2 * u, 2)])
        cand0, pos = candidates(0, 1)
        cand1, _ = candidates(2, 3)
        _extract_topk([cand0, cand1], pos, float(PEER_TOPK * PEER_TOPK), cv_ref, cp_ref)
        for u in range(2):
            top_s = cv_ref[u]
            top_p = cp_ref[u].astype(I32)
            ri = lax.shift_right_logical(top_p, 4)
            rj = lax.bitwise_and(top_p, PEER_TOPK - 1)
            a_sel = jnp.zeros((PEER_TOPK, tm), F32)
            b_sel = jnp.zeros((PEER_TOPK, tm), F32)
            for r in range(PEER_TOPK):
                a_sel = jnp.where(ri == r, ti_ref[2 * u, r:r + 1, :], a_sel)
                b_sel = jnp.where(rj == r, ti_ref[2 * u + 1, r:r + 1, :], b_sel)
            e = jnp.exp(top_s - top_s[0:1, :])
            g = e / jnp.sum(e, axis=0, keepdims=True)
            r0 = pl.multiple_of((2 * hp + u) * PEER_TOPK, PEER_TOPK)
            oa_ref[pl.ds(r0, PEER_TOPK), :] = a_sel
            ob_ref[pl.ds(r0, PEER_TOPK), :] = b_sel
            ow_ref[pl.ds(r0, PEER_TOPK), :] = g
        return carry

    lax.fori_loop(0, PEER_HEADS // 2, head_pair, 0)
    a_ref[...] = oa_ref[...].T.astype(I32)
    b_ref[...] = ob_ref[...].T.astype(I32)
    w_ref[...] = ow_ref[...].T


def _route(xn, wqt_b, k1_b, k2_b):
    n = xn.shape[0]
    tm = ROUTE_TM
    nslot = PEER_HEADS * PEER_TOPK
    out = lambda: pl.BlockSpec((tm, nslot), lambda i: (i, 0))
    return pl.pallas_call(
        _route_kernel,
        grid=(n // tm,),
        in_specs=[pl.BlockSpec((tm, D_MODEL), lambda i: (i, 0)),
                  pl.BlockSpec((PEER_HEADS * PEER_QDIM, D_MODEL), lambda i: (0, 0)),
                  pl.BlockSpec((PEER_NKEYS, PEER_QDIM // 2), lambda i: (0, 0)),
                  pl.BlockSpec((PEER_NKEYS, PEER_QDIM // 2), lambda i: (0, 0))],
        out_specs=[out(), out(), out()],
        out_shape=[jax.ShapeDtypeStruct((n, nslot), I32),
                   jax.ShapeDtypeStruct((n, nslot), I32),
                   jax.ShapeDtypeStruct((n, nslot), F32)],
        scratch_shapes=[pltpu.VMEM((PEER_HEADS * PEER_QDIM, tm), F32),
                        pltpu.VMEM((4, PEER_TOPK, tm), F32), pltpu.VMEM((4, PEER_TOPK, tm), F32),
                        pltpu.VMEM((2, PEER_TOPK, tm), F32), pltpu.VMEM((2, PEER_TOPK, tm), F32),
                        pltpu.VMEM((nslot, tm), F32), pltpu.VMEM((nslot, tm), F32),
                        pltpu.VMEM((nslot, tm), F32)],
        compiler_params=_cparams(("parallel",)),
        name="peer_route",
    )(xn, wqt_b, k1_b, k2_b)


EXP_CHUNK = 2048
EXP_NCHUNK = PEER_NKEYS * PEER_NKEYS // EXP_CHUNK
PAIR = 2 * PEER_NKEYS


def _expu_kernel(xn_ref, ut_ref, a_ref, b_ref, z_ref, buf0_ref, buf1_ref):
    tm = xn_ref.shape[0]
    j = pl.program_id(1)
    nj = pl.num_programs(1) - 1
    groups_per_chunk = EXP_CHUNK // PEER_NKEYS

    def matmul_into(buf_ref):
        buf_ref[...] = jnp.dot(xn_ref[...], ut_ref[...], preferred_element_type=F32)

    def gather_from(buf_ref):
        a0 = (j - 1) * groups_per_chunk
        for g in range(tm // SUBLANES):
            rs = slice(g * SUBLANES, (g + 1) * SUBLANES)
            bi = b_ref[rs, :]
            ai = a_ref[rs, :] - a0
            acc = z_ref[rs, :]
            for t in range(groups_per_chunk):
                got = jnp.take_along_axis(buf_ref[rs, t * PEER_NKEYS:(t + 1) * PEER_NKEYS], bi,
                                          axis=1, mode="promise_in_bounds")
                acc = jnp.where(ai == t, got, acc)
            z_ref[rs, :] = acc

    odd = lax.rem(j, 2) == 1
    inner = jnp.logical_and(j > 0, j < nj)

    @pl.when(j == 0)
    def _():
        z_ref[...] = jnp.zeros_like(z_ref)
        matmul_into(buf0_ref)

    @pl.when(jnp.logical_and(inner, odd))
    def _():
        gather_from(buf0_ref)
        matmul_into(buf1_ref)

    @pl.when(jnp.logical_and(inner, jnp.logical_not(odd)))
    def _():
        gather_from(buf1_ref)
        matmul_into(buf0_ref)

    @pl.when(j == nj)
    def _():
        gather_from(buf1_ref if (EXP_NCHUNK - 1) % 2 == 1 else buf0_ref)


def _expert_u(xn, ut_b, aidx, bidx, tm=512):
    n = xn.shape[0]
    nexp = ut_b.shape[1]
    assert nexp == EXP_NCHUNK * EXP_CHUNK
    nslot = PEER_HEADS * PEER_TOPK
    last = EXP_NCHUNK - 1
    return pl.pallas_call(
        _expu_kernel,
        grid=(n // tm, EXP_NCHUNK + 1),
        in_specs=[pl.BlockSpec((tm, D_MODEL), lambda i, j: (i, 0)),
                  pl.BlockSpec((D_MODEL, EXP_CHUNK), lambda i, j: (0, jnp.minimum(j, last))),
                  pl.BlockSpec((tm, nslot), lambda i, j: (i, 0)),
                  pl.BlockSpec((tm, nslot), lambda i, j: (i, 0))],
        out_specs=pl.BlockSpec((tm, nslot), lambda i, j: (i, 0)),
        out_shape=jax.ShapeDtypeStruct((n, nslot), F32),
        scratch_shapes=[pltpu.VMEM((tm, EXP_CHUNK), F32), pltpu.VMEM((tm, EXP_CHUNK), F32)],
        compiler_params=_cparams(("parallel", "arbitrary")),
        name="peer_expert_u",
    )(xn, ut_b, aidx, bidx)


SLAB_PAD = SUBLANES
SLAB_TOKENS = 32
U32 = jnp.uint32


def _bf16_bits(x):
    return pltpu.bitcast(x.astype(BF16).astype(F32), U32)


def _expv_kernel(z_ref, w_ref, a_ref, b_ref, v_ref, h_ref, o_ref, s_ref, acc_ref, c_ref):
    tm = z_ref.shape[0]
    nk = PEER_NKEYS
    pitch = tm // 2 + SLAB_PAD
    j = pl.program_id(1)

    @pl.when(j == 0)
    def _():
        c_ref[...] = w_ref[...] * _gelu(z_ref[...])
        sub = lax.broadcasted_iota(I32, (nk, nk), 0)
        dn = (((1,), (1,)), ((), ()))

        def slab(n):
            arow = a_ref[pl.ds(n, 1), :]
            brow = b_ref[pl.ds(n, 1), :]
            crow = c_ref[pl.ds(n, 1), :]
            oat = jnp.where(arow == sub, 1.0, 0.0).astype(BF16)
            wobt = jnp.where(brow == sub, crow, 0.0).astype(BF16)
            return lax.dot_general(oat, wobt, dn, preferred_element_type=F32)

        def toks(g, carry):
            base = g * (SLAB_TOKENS // 2)
            for t in range(SLAB_TOKENS // 2):
                row = base + t
                lo = lax.shift_right_logical(_bf16_bits(slab(2 * row)), jnp.uint32(16))
                hi = _bf16_bits(slab(2 * row + 1))
                s_ref[pl.ds(row, nk, stride=pitch), :] = lo | hi
            return carry

        lax.fori_loop(0, tm // SLAB_TOKENS, toks, 0)

    groups = EXP_CHUNK // nk
    slabs = []
    for t in range(groups):
        r0 = pl.multiple_of((j * groups + t) * pitch, SUBLANES)
        slabs.append(pltpu.bitcast(s_ref[pl.ds(r0, tm // 2), :], BF16))
    part = jnp.dot(jnp.concatenate(slabs, axis=1), v_ref[...], preferred_element_type=F32)

    @pl.when(j == 0)
    def _():
        acc_ref[...] = part

    @pl.when(j > 0)
    def _():
        acc_ref[...] += part

    @pl.when(j == pl.num_programs(1) - 1)
    def _():
        o_ref[...] = h_ref[...] + acc_ref[...]


def _expert_v(zsel, w, aidx, bidx, v_b, h1, tm=512):
    n = zsel.shape[0]
    nexp = v_b.shape[0]
    nslot = PEER_HEADS * PEER_TOPK
    slot = lambda: pl.BlockSpec((tm, nslot), lambda i, j: (i, 0))
    return pl.pallas_call(
        _expv_kernel,
        grid=(n // tm, nexp // EXP_CHUNK),
        in_specs=[slot(), slot(), slot(), slot(),
                  pl.BlockSpec((EXP_CHUNK, D_MODEL), lambda i, j: (j, 0)),
                  pl.BlockSpec((tm, D_MODEL), lambda i, j: (i, 0))],
        out_specs=pl.BlockSpec((tm, D_MODEL), lambda i, j: (i, 0)),
        out_shape=jax.ShapeDtypeStruct((n, D_MODEL), F32),
        scratch_shapes=[pltpu.VMEM((PEER_NKEYS * (tm // 2 + SLAB_PAD), PEER_NKEYS), U32),
                        pltpu.VMEM((tm, D_MODEL), F32),
                        pltpu.VMEM((tm, nslot), F32)],
        compiler_params=_cparams(("parallel", "arbitrary")),
        name="peer_expert_v",
    )(zsel, w, aidx, bidx, v_b, h1)


def _ple_kernel(h_ref, p_ref, gp_ref, wg_ref, wp_ref, gf_ref, o_ref):
    h = h_ref[...]
    xn = _rms(h, gp_ref[...]).astype(BF16)
    gate = jax.nn.sigmoid(jnp.dot(xn, wg_ref[...], preferred_element_type=F32))
    proj = jnp.dot(p_ref[...].astype(BF16), wp_ref[...], preferred_element_type=F32)
    o_ref[...] = _rms(h + gate * proj, gf_ref[...])


def _ple(h2, p2, g_ple, wg_b, wp_b, g_final, tm=512):
    n = h2.shape[0]
    return pl.pallas_call(
        _ple_kernel,
        grid=(n // tm,),
        in_specs=[pl.BlockSpec((tm, D_MODEL), lambda i: (i, 0)),
                  pl.BlockSpec((tm, PLE_DIM), lambda i: (i, 0)),
                  pl.BlockSpec((1, D_MODEL), lambda i: (0, 0)),
                  pl.BlockSpec((D_MODEL, D_MODEL), lambda i: (0, 0)),
                  pl.BlockSpec((PLE_DIM, D_MODEL), lambda i: (0, 0)),
                  pl.BlockSpec((1, D_MODEL), lambda i: (0, 0))],
        out_specs=pl.BlockSpec((tm, D_MODEL), lambda i: (i, 0)),
        out_shape=jax.ShapeDtypeStruct((n, D_MODEL), F32),
        compiler_params=_cparams(("parallel",)),
        name="ple_final",
    )(h2, p2, g_ple, wg_b, wp_b, g_final)


def _block_diag(w):
    g, d, _ = w.shape
    eye = jnp.eye(g, dtype=w.dtype)
    return (eye[:, None, :, None] * w[:, :, None, :]).reshape(g * d, g * d)


def kernel(x, p, g_mix, w_in, conv_w, conv_b, w_rg, b_rg, w_ig, b_ig, lru_lambda, g_lru_out,
           g_attn_out, w_out, rel_bias, g_ffn, peer_wq, peer_k1, peer_k2, peer_u, peer_v,
           g_ple, w_ple_gate, w_ple_proj, g_final):
    batch, seq, dm = x.shape
    n = batch * seq
    assert w_in.shape[0] == 1, "single-layer problem: the last kernel fuses the final norm"
    row = lambda v: v.reshape(1, -1).astype(F32)
    btab = _bias_tables(rel_bias)
    h = x.reshape(n, dm)
    for i in range(1):
        xl, gate, q, k, v = _inproj(h, row(g_mix[i]), w_in[i].astype(BF16))
        y_lru = _lru(xl, gate, conv_w[i], row(conv_b[i]),
                     _block_diag(w_rg[i]).astype(BF16), row(b_rg[i]),
                     _block_diag(w_ig[i]).astype(BF16), row(b_ig[i]),
                     row(lru_lambda[i]), row(g_lru_out[i]), batch, seq)
        y_att = _attention(q, k, v, btab, batch, seq)
        h1, xn = _outproj(y_lru, y_att, h, row(g_attn_out[i]), w_out[i].astype(BF16),
                          row(g_ffn[i]))
        aidx, bidx, gates = _route(xn, peer_wq[i].T.astype(BF16), peer_k1[i].astype(BF16),
                                   peer_k2[i].astype(BF16))
        zsel = _expert_u(xn, peer_u[i].T.astype(BF16), aidx, bidx)
        h2 = _expert_v(zsel, gates, aidx, bidx, peer_v[i].astype(BF16), h1)
        h = _ple(h2, p[i].reshape(n, -1), row(g_ple[i]), w_ple_gate[i].astype(BF16),
                 w_ple_proj[i].astype(BF16), g_final.reshape(1, -1))
    return h.reshape(batch, seq, dm)
```

```python
import functools
import math

import jax
import jax.numpy as jnp
import numpy as np
from jax import lax
from jax.experimental import pallas as pl
from jax.experimental.pallas import tpu as pltpu

F32 = jnp.float32
BF16 = jnp.bfloat16
I32 = jnp.int32

D_MODEL = 1024
PLE_DIM = 256
D_LRU = 512
LRU_BLOCKS = 8
CONV_WIDTH = 4
LRU_C = 8.0
D_ATTN = 512
N_ATT_HEADS = 8
HEAD_DIM = 64
DILATED_GROUPS = ((128, 1), (512, 4), (2048, 16))
ATT_BLOCK = 128
REL_BUCKETS = 32
REL_MAX_DIST = 2048
PEER_HEADS = 8
PEER_NKEYS = 128
PEER_QDIM = 256
PEER_TOPK = 16
NORM_EPS = 1e-6

LANES = 128
SUBLANES = 8
NEG_BIG = -1e30
VMEM_LIMIT = 56 * 1024 * 1024


def _cparams(sem, flags=None):
    return pltpu.CompilerParams(dimension_semantics=sem, vmem_limit_bytes=VMEM_LIMIT, flags=flags)


def _rms(x, g):
    ms = jnp.mean(x * x, axis=-1, keepdims=True)
    return (x * lax.rsqrt(ms + NORM_EPS)) * g


def _gelu(x):
    c = math.sqrt(2.0 / math.pi)
    return x * (0.5 * (1.0 + jnp.tanh(c * (x + 0.044715 * (x * x * x)))))


def _inproj_kernel(x_ref, g_ref, w_ref, xl_ref, gate_ref, q_ref, k_ref, v_ref):
    u = _rms(x_ref[...], g_ref[...]).astype(BF16)
    outs = (xl_ref, gate_ref, q_ref, k_ref, v_ref)
    for i, o_ref in enumerate(outs):
        z = jnp.dot(u, w_ref[:, i * 512:(i + 1) * 512], preferred_element_type=F32)
        if o_ref is q_ref:
            z = z * (HEAD_DIM ** -0.5)
        o_ref[...] = z


def _inproj(x2, g_mix, w_in_b, tm=512):
    n = x2.shape[0]
    blk = lambda: pl.BlockSpec((tm, 512), lambda i: (i, 0))
    return pl.pallas_call(
        _inproj_kernel,
        grid=(n // tm,),
        in_specs=[pl.BlockSpec((tm, D_MODEL), lambda i: (i, 0)),
                  pl.BlockSpec((1, D_MODEL), lambda i: (0, 0)),
                  pl.BlockSpec((D_MODEL, 2560), lambda i: (0, 0))],
        out_specs=[blk() for _ in range(5)],
        out_shape=[jax.ShapeDtypeStruct((n, 512), F32) for _ in range(5)],
        compiler_params=_cparams(("parallel",)),
        name="inproj",
    )(x2, g_mix, w_in_b)


LRU_CHUNK = 64


def _lru_kernel(xl_ref, gate_ref, cw_ref, cb_ref, wrg_ref, brg_ref, wig_ref, big_ref,
                lam_ref, g_ref, o_ref, xp_ref, h_ref):
    T = xl_ref.shape[0]
    tc = LRU_CHUNK
    xp_ref[0:SUBLANES, :] = jnp.zeros((SUBLANES, D_LRU), F32)
    xp_ref[SUBLANES:SUBLANES + T, :] = xl_ref[...]
    h_ref[...] = jnp.zeros_like(h_ref)
    nlam = -lam_ref[...]
    sp = jnp.maximum(nlam, 0.0) + jnp.log1p(jnp.exp(-jnp.abs(nlam)))
    row = lax.broadcasted_iota(I32, (tc, D_LRU), 0)

    def chunk(c, carry):
        t0 = pl.multiple_of(c * tc, tc)
        xa = xp_ref[pl.ds(t0, tc + SUBLANES), :]
        xc = cb_ref[...] + cw_ref[CONV_WIDTH - 1:CONV_WIDTH, :] * xa[SUBLANES:, :]
        for j in range(CONV_WIDTH - 1):
            sh = pltpu.roll(xa, CONV_WIDTH - 1 - j, axis=0)[SUBLANES:, :]
            xc = xc + cw_ref[j:j + 1, :] * sh
        xcb = xc.astype(BF16)
        r = jax.nn.sigmoid(jnp.dot(xcb, wrg_ref[...], preferred_element_type=F32) + brg_ref[...])
        ig = jax.nn.sigmoid(jnp.dot(xcb, wig_ref[...], preferred_element_type=F32) + big_ref[...])
        log_a = (-LRU_C) * r * sp
        a = jnp.exp(log_a)
        b = jnp.sqrt(-jnp.tanh(log_a) * (a * a + 1.0)) * (ig * xc)
        d = 1
        while d < tc:
            keep = row >= d
            a_sh = jnp.where(keep, pltpu.roll(a, d, axis=0), 1.0)
            b_sh = jnp.where(keep, pltpu.roll(b, d, axis=0), 0.0)
            b = a * b_sh + b
            a = a * a_sh
            d *= 2
        h = b + a * h_ref[...]
        h_ref[...] = h[tc - 1:tc, :]
        y = h * _gelu(gate_ref[pl.ds(t0, tc), :])
        o_ref[pl.ds(t0, tc), :] = _rms(y, g_ref[...]).astype(BF16)
        return carry

    lax.fori_loop(0, T // tc, chunk, 0)


def _lru(xl, gate, conv_w, conv_b, wrg_bd, b_rg, wig_bd, b_ig, lam, g_lru, batch, seq):
    vec = lambda: pl.BlockSpec((1, D_LRU), lambda b: (0, 0))
    mat = lambda: pl.BlockSpec((D_LRU, D_LRU), lambda b: (0, 0))
    return pl.pallas_call(
        _lru_kernel,
        grid=(batch,),
        in_specs=[pl.BlockSpec((seq, D_LRU), lambda b: (b, 0)),
                  pl.BlockSpec((seq, D_LRU), lambda b: (b, 0)),
                  pl.BlockSpec((CONV_WIDTH, D_LRU), lambda b: (0, 0)),
                  vec(), mat(), vec(), mat(), vec(), vec(), vec()],
        out_specs=pl.BlockSpec((seq, D_LRU), lambda b: (b, 0)),
        out_shape=jax.ShapeDtypeStruct((batch * seq, D_LRU), BF16),
        scratch_shapes=[pltpu.VMEM((seq + SUBLANES, D_LRU), F32),
                        pltpu.VMEM((1, D_LRU), F32)],
        compiler_params=_cparams(("parallel",)),
        name="rglru",
    )(xl, gate, conv_w, conv_b, wrg_bd, b_rg, wig_bd, b_ig, lam, g_lru)


ATT_UNROLL = 8


def _attend(qf, kf, vf, bias_a, bias_b, is_a):
    kb = kf.astype(BF16)
    vb = vf.astype(BF16)
    dn = (((1,), (1,)), ((), ()))
    outs = []
    for q_h, bias in ((jnp.where(is_a, qf, 0.0), bias_a), (jnp.where(is_a, 0.0, qf), bias_b)):
        s = lax.dot_general(q_h.astype(BF16), kb, dn, preferred_element_type=F32) + bias
        m = jnp.max(s, axis=-1, keepdims=True)
        e = jnp.exp(s - m)
        den = jnp.sum(e, axis=-1, keepdims=True)
        num = jnp.dot(e.astype(BF16), vb, preferred_element_type=F32)
        outs.append((num, m, den))
    (na, ma, da), (nb, mb, db) = outs
    shape = na.shape
    return (jnp.where(is_a, na, nb),
            jnp.where(is_a, jnp.broadcast_to(ma, shape), jnp.broadcast_to(mb, shape)),
            jnp.where(is_a, jnp.broadcast_to(da, shape), jnp.broadcast_to(db, shape)))


def _attn_kernel(q_ref, k_ref, v_ref, bt_ref, o_ref, num_ref, m_ref, den_ref):
    T = q_ref.shape[0]
    blk = ATT_BLOCK
    is_a = lax.broadcasted_iota(I32, (blk, LANES), 1) < HEAD_DIM

    def rows(start, size, d):
        return pl.ds(start, size) if d == 1 else pl.ds(start, size, stride=d)

    def first_block(br, d, r):
        sl = rows(r, blk, d)
        pad = jnp.zeros((blk, LANES), F32)
        left = lax.broadcasted_iota(I32, (blk, 2 * blk), 1) < blk
        out = _attend(q_ref[sl, :], jnp.concatenate([pad, k_ref[sl, :]], axis=0),
                      jnp.concatenate([pad, v_ref[sl, :]], axis=0),
                      jnp.where(left, NEG_BIG, bt_ref[br, 0]),
                      jnp.where(left, NEG_BIG, bt_ref[br, 1]), is_a)
        for ref, val in zip((num_ref, m_ref, den_ref), out):
            ref[br, sl, :] = val

    def later_block(br, d, r, n):
        qs = rows(r + n * (blk * d), blk, d)
        ks = rows(r + (n - 1) * (blk * d), 2 * blk, d)
        out = _attend(q_ref[qs, :], k_ref[ks, :], v_ref[ks, :],
                      bt_ref[br, 0], bt_ref[br, 1], is_a)
        for ref, val in zip((num_ref, m_ref, den_ref), out):
            ref[br, qs, :] = val

    for br, (_, d) in enumerate(DILATED_GROUPS):
        nb = T // (d * blk)
        def firsts(r, carry, br=br, d=d):
            first_block(br, d, r)
            return carry

        def laters(i, carry, br=br, d=d, nb=nb):
            r = lax.div(i, nb - 1)
            later_block(br, d, r, 1 + lax.rem(i, nb - 1))
            return carry

        if d == 1:
            first_block(br, d, 0)
        else:
            lax.fori_loop(0, d, firsts, 0, unroll=ATT_UNROLL)
        if nb > 1:
            lax.fori_loop(0, d * (nb - 1), laters, 0, unroll=ATT_UNROLL)

    def merge(c, carry):
        sl = pl.ds(pl.multiple_of(c * blk, blk), blk)
        m0, m1, m2 = m_ref[0, sl, :], m_ref[1, sl, :], m_ref[2, sl, :]
        mm = jnp.maximum(jnp.maximum(m0, m1), m2)
        w0, w1, w2 = jnp.exp(m0 - mm), jnp.exp(m1 - mm), jnp.exp(m2 - mm)
        num = w0 * num_ref[0, sl, :] + w1 * num_ref[1, sl, :] + w2 * num_ref[2, sl, :]
        den = w0 * den_ref[0, sl, :] + w1 * den_ref[1, sl, :] + w2 * den_ref[2, sl, :]
        o_ref[sl, :] = num / den
        return carry

    lax.fori_loop(0, T // blk, merge, 0)


def _attention(q, k, v, btab, batch, seq):
    npairs = D_ATTN // LANES
    qkv = lambda: pl.BlockSpec((seq, LANES), lambda b, p: (b, p))
    return pl.pallas_call(
        _attn_kernel,
        grid=(batch, npairs),
        in_specs=[qkv(), qkv(), qkv(),
                  pl.BlockSpec((3, 2, ATT_BLOCK, 2 * ATT_BLOCK), lambda b, p: (0, p, 0, 0))],
        out_specs=pl.BlockSpec((seq, LANES), lambda b, p: (b, p)),
        out_shape=jax.ShapeDtypeStruct((batch * seq, D_ATTN), F32),
        scratch_shapes=[pltpu.VMEM((3, seq, LANES), F32) for _ in range(3)],
        compiler_params=_cparams(("parallel", "parallel")),
        name="dilated_attn",
    )(q, k, v, btab)


def _bias_tables(rel_bias):
    tabs = []
    for window, d in DILATED_GROUPS:
        w_sub = window // d
        qi = jnp.arange(ATT_BLOCK)[:, None]
        kj = jnp.arange(2 * ATT_BLOCK)[None, :]
        dist = qi + ATT_BLOCK - kj
        max_exact = REL_BUCKETS // 2
        nn = jnp.maximum(dist * d, 0)
        nf = jnp.maximum(nn, 1).astype(F32)
        large = max_exact + (jnp.log(nf / max_exact) / math.log(REL_MAX_DIST / max_exact)
                             * (REL_BUCKETS - max_exact)).astype(I32)
        large = jnp.minimum(large, REL_BUCKETS - 1)
        bucket = jnp.where(nn < max_exact, nn, large)
        rb = rel_bias.astype(F32)
        bias = jnp.zeros((N_ATT_HEADS,) + bucket.shape, F32)
        for bkt in range(REL_BUCKETS):
            bias = jnp.where(bucket[None] == bkt, rb[bkt][:, None, None], bias)
        ok = (dist >= 0) & (dist <= w_sub)
        tabs.append(jnp.where(ok[None], bias, NEG_BIG))
    return jnp.stack(tabs, axis=0)


def _outproj_kernel(yl_ref, ya_ref, x_ref, ga_ref, w_ref, gf_ref, h_ref, xn_ref):
    ya = _rms(ya_ref[...], ga_ref[...]).astype(BF16)
    acc = jnp.dot(yl_ref[...], w_ref[0:D_LRU, :], preferred_element_type=F32)
    acc = acc + jnp.dot(ya, w_ref[D_LRU:, :], preferred_element_type=F32)
    h = x_ref[...] + acc
    h_ref[...] = h
    xn_ref[...] = _rms(h, gf_ref[...]).astype(BF16)


def _outproj(y_lru, y_att, x2, g_att, w_out_b, g_ffn, tm=512):
    n = x2.shape[0]
    return pl.pallas_call(
        _outproj_kernel,
        grid=(n // tm,),
        in_specs=[pl.BlockSpec((tm, D_LRU), lambda i: (i, 0)),
                  pl.BlockSpec((tm, D_ATTN), lambda i: (i, 0)),
                  pl.BlockSpec((tm, D_MODEL), lambda i: (i, 0)),
                  pl.BlockSpec((1, D_ATTN), lambda i: (0, 0)),
                  pl.BlockSpec((D_MODEL, D_MODEL), lambda i: (0, 0)),
                  pl.BlockSpec((1, D_MODEL), lambda i: (0, 0))],
        out_specs=[pl.BlockSpec((tm, D_MODEL), lambda i: (i, 0)),
                   pl.BlockSpec((tm, D_MODEL), lambda i: (i, 0))],
        out_shape=[jax.ShapeDtypeStruct((n, D_MODEL), F32),
                   jax.ShapeDtypeStruct((n, D_MODEL), BF16)],
        compiler_params=_cparams(("parallel",)),
        name="outproj",
    )(y_lru, y_att, x2, g_att, w_out_b, g_ffn)


ROUTE_TM = 128
_CAND_GROUPS = (("col", 0, 0, 8), ("col", 0, 8, 16), ("row", 0, 1, 8), ("row", 0, 8, 16),
                ("col", 1, 1, 8), ("row", 1, 2, 8), ("col", 2, 2, 5), ("row", 2, 3, 5),
                ("col", 3, 3, 4))


def _tree(op, xs):
    xs = list(xs)
    while len(xs) > 1:
        xs = [op(xs[i], xs[i + 1]) for i in range(0, len(xs) - 1, 2)] + (xs[-1:] if len(xs) % 2 else [])
    return xs[0]


def _extract_topk_keys(chains, sub_f, val_ref, id_ref):
    nv = PEER_NKEYS // SUBLANES
    big = float(PEER_NKEYS)

    def mask_out(parts, sel):
        off = sel - sub_f
        return tuple(jnp.where(off == float(SUBLANES * v), NEG_BIG, p) for v, p in enumerate(parts))

    def pick(c, k, parts):
        m = jnp.max(_tree(jnp.maximum, parts), axis=0, keepdims=True)
        first = _tree(jnp.minimum, [jnp.where(p == m, float(SUBLANES * v), big)
                                    for v, p in enumerate(parts)])
        sel = jnp.min(first + sub_f, axis=0, keepdims=True)
        val_ref[c, pl.ds(k, 1), :] = m
        id_ref[c, pl.ds(k, 1), :] = sel
        return sel

    def rnd(k, carry):
        states, pending = carry
        nxt, nxt_pending = [], []
        for c, parts in enumerate(states):
            if c % 2:
                parts = mask_out(parts, pending[c // 2])
                nxt_pending.append(pick(c, k, parts))
                nxt.append(parts)
            else:
                nxt.append(mask_out(parts, pick(c, k, parts)))
        return tuple(nxt), tuple(nxt_pending)

    init = tuple(tuple(s[v * SUBLANES:(v + 1) * SUBLANES, :] for v in range(nv)) for s in chains)
    no_sel = jnp.full((1, sub_f.shape[1]), -1.0, F32)
    lax.fori_loop(0, PEER_TOPK, rnd, (init, tuple(no_sel for _ in range(len(chains) // 2))))


def _extract_topk(chains, ids, big, val_ref, id_ref):
    def rnd(k, states):
        nxt = []
        for c, s in enumerate(states):
            m = jnp.max(s, axis=0, keepdims=True)
            sel = jnp.min(jnp.where(s == m, ids, big), axis=0, keepdims=True)
            val_ref[c, pl.ds(k, 1), :] = m
            id_ref[c, pl.ds(k, 1), :] = sel
            nxt.append(jnp.where(ids == sel, NEG_BIG, s))
        return tuple(nxt)

    lax.fori_loop(0, PEER_TOPK, rnd, tuple(chains))


def _route_kernel(xn_ref, wqt_ref, k1_ref, k2_ref, a_ref, b_ref, w_ref,
                  qt_ref, tv_ref, ti_ref, cv_ref, cp_ref, oa_ref, ob_ref, ow_ref):
    tm = xn_ref.shape[0]
    half = PEER_QDIM // 2
    dn = (((1,), (1,)), ((), ()))
    qt_ref[...] = lax.dot_general(wqt_ref[...], xn_ref[...], dn, preferred_element_type=F32)
    sub = lax.broadcasted_iota(I32, (SUBLANES, tm), 0)
    sub_f = sub.astype(F32)

    def scores(k_ref, row0):
        qs = qt_ref[pl.ds(pl.multiple_of(row0, half), half), :].astype(BF16)
        return jnp.dot(k_ref[...], qs, preferred_element_type=F32)

    def candidates(c1, c2):
        cands, poss = [], []
        for kind, fixed, lo, hi in _CAND_GROUPS:
            base = (lo // SUBLANES) * SUBLANES
            rng = sub + base
            valid = (rng >= lo) & (rng < hi)
            if kind == "col":
                val = tv_ref[c1, base:base + SUBLANES, :] + tv_ref[c2, fixed:fixed + 1, :]
                pos = rng * PEER_TOPK + fixed
            else:
                val = tv_ref[c1, fixed:fixed + 1, :] + tv_ref[c2, base:base + SUBLANES, :]
                pos = fixed * PEER_TOPK + rng
            cands.append(jnp.where(valid, val, NEG_BIG))
            poss.append(pos.astype(F32))
        return jnp.concatenate(cands, axis=0), jnp.concatenate(poss, axis=0)

    def head_pair(hp, carry):
        for u in range(2):
            row0 = (2 * hp + u) * PEER_QDIM
            _extract_topk_keys([scores(k1_ref, row0), scores(k2_ref, row0 + half)], sub_f,
                               tv_ref.at[pl.ds(2 * u, 2)], ti_ref.at[pl.ds(2 * u, 2)])
        cand0, pos = candidates(0, 1)
        cand1, _ = candidates(2, 3)
        _extract_topk([cand0, cand1], pos, float(PEER_TOPK * PEER_TOPK), cv_ref, cp_ref)
        for u in range(2):
            top_s = cv_ref[u]
            top_p = cp_ref[u].astype(I32)
            ri = lax.shift_right_logical(top_p, 4)
            rj = lax.bitwise_and(top_p, PEER_TOPK - 1)
            a_sel = jnp.zeros((PEER_TOPK, tm), F32)
            b_sel = jnp.zeros((PEER_TOPK, tm), F32)
            for r in range(PEER_TOPK):
                a_sel = jnp.where(ri == r, ti_ref[2 * u, r:r + 1, :], a_sel)
                b_sel = jnp.where(rj == r, ti_ref[2 * u + 1, r:r + 1, :], b_sel)
            e = jnp.exp(top_s - top_s[0:1, :])
            g = e / jnp.sum(e, axis=0, keepdims=True)
            r0 = pl.multiple_of((2 * hp + u) * PEER_TOPK, PEER_TOPK)
            oa_ref[pl.ds(r0, PEER_TOPK), :] = a_sel
            ob_ref[pl.ds(r0, PEER_TOPK), :] = b_sel
            ow_ref[pl.ds(r0, PEER_TOPK), :] = g
        return carry

    lax.fori_loop(0, PEER_HEADS // 2, head_pair, 0)
    a_ref[...] = oa_ref[...].T.astype(I32)
    b_ref[...] = ob_ref[...].T.astype(I32)
    w_ref[...] = ow_ref[...].T


def _route(xn, wqt_b, k1_b, k2_b):
    n = xn.shape[0]
    tm = ROUTE_TM
    nslot = PEER_HEADS * PEER_TOPK
    out = lambda: pl.BlockSpec((tm, nslot), lambda i: (i, 0))
    return pl.pallas_call(
        _route_kernel,
        grid=(n // tm,),
        in_specs=[pl.BlockSpec((tm, D_MODEL), lambda i: (i, 0)),
                  pl.BlockSpec((PEER_HEADS * PEER_QDIM, D_MODEL), lambda i: (0, 0)),
                  pl.BlockSpec((PEER_NKEYS, PEER_QDIM // 2), lambda i: (0, 0)),
                  pl.BlockSpec((PEER_NKEYS, PEER_QDIM // 2), lambda i: (0, 0))],
        out_specs=[out(), out(), out()],
        out_shape=[jax.ShapeDtypeStruct((n, nslot), I32),
                   jax.ShapeDtypeStruct((n, nslot), I32),
                   jax.ShapeDtypeStruct((n, nslot), F32)],
        scratch_shapes=[pltpu.VMEM((PEER_HEADS * PEER_QDIM, tm), F32),
                        pltpu.VMEM((4, PEER_TOPK, tm), F32), pltpu.VMEM((4, PEER_TOPK, tm), F32),
                        pltpu.VMEM((2, PEER_TOPK, tm), F32), pltpu.VMEM((2, PEER_TOPK, tm), F32),
                        pltpu.VMEM((nslot, tm), F32), pltpu.VMEM((nslot, tm), F32),
                        pltpu.VMEM((nslot, tm), F32)],
        compiler_params=_cparams(("parallel",)),
        name="peer_route",
    )(xn, wqt_b, k1_b, k2_b)


RK_TM = 1024
RK_G = RK_TM // LANES
RK_NACC = 4
_CAND_CELLS = tuple((i, j) for i in range(PEER_TOPK) for j in range(PEER_TOPK)
                    if (i + 1) * (j + 1) <= PEER_TOPK)


def _extract_rows(ref, chain_ids, ids, big, nrounds, emit):
    n = len(ids)
    tile = lambda r: slice(r * RK_G, (r + 1) * RK_G)

    def fold(op, terms):
        accs = []
        for r, t in enumerate(terms):
            if r < RK_NACC:
                accs.append(t)
            else:
                accs[r % RK_NACC] = op(accs[r % RK_NACC], t)
        return _tree(op, accs)

    def rnd(k, carry):
        for c in chain_ids:
            m = fold(jnp.maximum, (ref[c, tile(r), :] for r in range(n)))
            sel = fold(jnp.minimum,
                       (jnp.where(ref[c, tile(r), :] == m, ids[r], big) for r in range(n)))
            emit(c, k, m, sel)
            for r in range(n):
                ref[c, tile(r), :] = jnp.where(sel == ids[r], NEG_BIG, ref[c, tile(r), :])
        return carry

    lax.fori_loop(0, nrounds, rnd, 0)


def _route2_kernel(xn_ref, wqt_ref, k1_ref, k2_ref, a_ref, b_ref, w_ref,
                   qt_ref, s_ref, tv_ref, ti_ref, cand_ref, cv_ref, cp_ref, oa_ref, ob_ref, ow_ref):
    half = PEER_QDIM // 2
    dn = (((1,), (1,)), ((), ()))
    qt_ref[...] = lax.dot_general(wqt_ref[...], xn_ref[...], dn, preferred_element_type=F32)
    key_ids = [float(r) for r in range(PEER_NKEYS)]
    cell_ids = [float(i * PEER_TOPK + j) for i, j in _CAND_CELLS]

    def head(h, carry):
        for c, k_ref in enumerate((k1_ref, k2_ref)):
            row0 = pl.multiple_of(h * PEER_QDIM + c * half, half)
            s = jnp.dot(k_ref[...], qt_ref[pl.ds(row0, half), :].astype(BF16),
                        preferred_element_type=F32)
            for g in range(RK_G):
                s_ref[c, pl.ds(g, PEER_NKEYS, stride=RK_G), :] = s[:, g * LANES:(g + 1) * LANES]

        def emit1(c, k, m, sel):
            tv_ref[c, k] = m
            ti_ref[c, k] = sel

        _extract_rows(s_ref, (0, 1), key_ids, float(PEER_NKEYS), PEER_TOPK, emit1)

        for n, (i, j) in enumerate(_CAND_CELLS):
            cand_ref[0, n * RK_G:(n + 1) * RK_G, :] = tv_ref[0, i] + tv_ref[1, j]

        def emit2(c, k, m, sel):
            cv_ref[k] = m
            cp_ref[k] = sel

        _extract_rows(cand_ref, (0,), cell_ids, float(PEER_TOPK * PEER_TOPK), PEER_TOPK, emit2)

        top = [cv_ref[k] for k in range(PEER_TOPK)]
        es = [jnp.exp(t - top[0]) for t in top]
        inv = 1.0 / _tree(jnp.add, es)
        for k in range(PEER_TOPK):
            pos = cp_ref[k].astype(I32)
            ri = lax.shift_right_logical(pos, 4)
            rj = lax.bitwise_and(pos, PEER_TOPK - 1)
            a_sel = jnp.zeros((RK_G, LANES), F32)
            b_sel = jnp.zeros((RK_G, LANES), F32)
            for r in range(PEER_TOPK):
                a_sel = jnp.where(ri == r, ti_ref[0, r], a_sel)
                b_sel = jnp.where(rj == r, ti_ref[1, r], b_sel)
            rows = pl.ds(pl.multiple_of((h * PEER_TOPK + k) * RK_G, RK_G), RK_G)
            oa_ref[rows, :] = a_sel
            ob_ref[rows, :] = b_sel
            ow_ref[rows, :] = es[k] * inv
        return carry

    lax.fori_loop(0, PEER_HEADS, head, 0)
    nslot = PEER_HEADS * PEER_TOPK
    for g in range(RK_G):
        rows = pl.ds(g, nslot, stride=RK_G)
        tok = slice(g * LANES, (g + 1) * LANES)
        a_ref[tok, :] = oa_ref[rows, :].T.astype(I32)
        b_ref[tok, :] = ob_ref[rows, :].T.astype(I32)
        w_ref[tok, :] = ow_ref[rows, :].T


def _route2(xn, wqt_b, k1_b, k2_b):
    n = xn.shape[0]
    tm = RK_TM
    nslot = PEER_HEADS * PEER_TOPK
    out = lambda: pl.BlockSpec((tm, nslot), lambda i: (i, 0))
    vregs = lambda *lead: pltpu.VMEM(lead + (RK_G, LANES), F32)
    tiles = lambda *lead: pltpu.VMEM(lead[:-1] + (lead[-1] * RK_G, LANES), F32)
    return pl.pallas_call(
        _route2_kernel,
        grid=(n // tm,),
        in_specs=[pl.BlockSpec((tm, D_MODEL), lambda i: (i, 0)),
                  pl.BlockSpec((PEER_HEADS * PEER_QDIM, D_MODEL), lambda i: (0, 0)),
                  pl.BlockSpec((PEER_NKEYS, PEER_QDIM // 2), lambda i: (0, 0)),
                  pl.BlockSpec((PEER_NKEYS, PEER_QDIM // 2), lambda i: (0, 0))],
        out_specs=[out(), out(), out()],
        out_shape=[jax.ShapeDtypeStruct((n, nslot), I32),
                   jax.ShapeDtypeStruct((n, nslot), I32),
                   jax.ShapeDtypeStruct((n, nslot), F32)],
        scratch_shapes=[pltpu.VMEM((PEER_HEADS * PEER_QDIM, tm), F32),
                        tiles(2, PEER_NKEYS), vregs(2, PEER_TOPK), vregs(2, PEER_TOPK),
                        tiles(1, len(_CAND_CELLS)), vregs(PEER_TOPK), vregs(PEER_TOPK),
                        tiles(nslot), tiles(nslot), tiles(nslot)],
        compiler_params=_cparams(("parallel",)),
        name="peer_route",
    )(xn, wqt_b, k1_b, k2_b)


EXP_CHUNK = 2048
EXP_NCHUNK = PEER_NKEYS * PEER_NKEYS // EXP_CHUNK
PAIR = 2 * PEER_NKEYS


def _expu_kernel(xn_ref, ut_ref, a_ref, b_ref, z_ref, buf0_ref, buf1_ref):
    tm = xn_ref.shape[0]
    j = pl.program_id(1)
    nj = pl.num_programs(1) - 1
    groups_per_chunk = EXP_CHUNK // PEER_NKEYS

    def matmul_into(buf_ref):
        buf_ref[...] = jnp.dot(xn_ref[...], ut_ref[...], preferred_element_type=F32)

    def gather_from(buf_ref):
        a0 = (j - 1) * groups_per_chunk
        for g in range(tm // SUBLANES):
            rs = slice(g * SUBLANES, (g + 1) * SUBLANES)
            bi = b_ref[rs, :]
            ai = a_ref[rs, :] - a0
            acc = z_ref[rs, :]
            for t in range(groups_per_chunk):
                got = jnp.take_along_axis(buf_ref[rs, t * PEER_NKEYS:(t + 1) * PEER_NKEYS], bi,
                                          axis=1, mode="promise_in_bounds")
                acc = jnp.where(ai == t, got, acc)
            z_ref[rs, :] = acc

    odd = lax.rem(j, 2) == 1
    inner = jnp.logical_and(j > 0, j < nj)

    @pl.when(j == 0)
    def _():
        z_ref[...] = jnp.zeros_like(z_ref)
        matmul_into(buf0_ref)

    @pl.when(jnp.logical_and(inner, odd))
    def _():
        gather_from(buf0_ref)
        matmul_into(buf1_ref)

    @pl.when(jnp.logical_and(inner, jnp.logical_not(odd)))
    def _():
        gather_from(buf1_ref)
        matmul_into(buf0_ref)

    @pl.when(j == nj)
    def _():
        gather_from(buf1_ref if (EXP_NCHUNK - 1) % 2 == 1 else buf0_ref)


def _expert_u(xn, ut_b, aidx, bidx, tm=512):
    n = xn.shape[0]
    nexp = ut_b.shape[1]
    assert nexp == EXP_NCHUNK * EXP_CHUNK
    nslot = PEER_HEADS * PEER_TOPK
    last = EXP_NCHUNK - 1
    return pl.pallas_call(
        _expu_kernel,
        grid=(n // tm, EXP_NCHUNK + 1),
        in_specs=[pl.BlockSpec((tm, D_MODEL), lambda i, j: (i, 0)),
                  pl.BlockSpec((D_MODEL, EXP_CHUNK), lambda i, j: (0, jnp.minimum(j, last))),
                  pl.BlockSpec((tm, nslot), lambda i, j: (i, 0)),
                  pl.BlockSpec((tm, nslot), lambda i, j: (i, 0))],
        out_specs=pl.BlockSpec((tm, nslot), lambda i, j: (i, 0)),
        out_shape=jax.ShapeDtypeStruct((n, nslot), F32),
        scratch_shapes=[pltpu.VMEM((tm, EXP_CHUNK), F32), pltpu.VMEM((tm, EXP_CHUNK), F32)],
        compiler_params=_cparams(("parallel", "arbitrary")),
        name="peer_expert_u",
    )(xn, ut_b, aidx, bidx)


SLAB_PAD = SUBLANES
SLAB_TOKENS = 32
U32 = jnp.uint32


def _bf16_bits(x):
    return pltpu.bitcast(x.astype(BF16).astype(F32), U32)


def _expv_kernel(z_ref, w_ref, a_ref, b_ref, v_ref, h_ref, o_ref, s_ref, acc_ref, c_ref):
    tm = z_ref.shape[0]
    nk = PEER_NKEYS
    pitch = tm // 2 + SLAB_PAD
    j = pl.program_id(1)

    @pl.when(j == 0)
    def _():
        c_ref[...] = w_ref[...] * _gelu(z_ref[...])
        sub = lax.broadcasted_iota(I32, (nk, nk), 0)
        dn = (((1,), (1,)), ((), ()))

        def slab(n):
            arow = a_ref[pl.ds(n, 1), :]
            brow = b_ref[pl.ds(n, 1), :]
            crow = c_ref[pl.ds(n, 1), :]
            oat = jnp.where(arow == sub, 1.0, 0.0).astype(BF16)
            wobt = jnp.where(brow == sub, crow, 0.0).astype(BF16)
            return lax.dot_general(oat, wobt, dn, preferred_element_type=F32)

        def toks(g, carry):
            base = g * (SLAB_TOKENS // 2)
            for t in range(SLAB_TOKENS // 2):
                row = base + t
                lo = lax.shift_right_logical(_bf16_bits(slab(2 * row)), jnp.uint32(16))
                hi = _bf16_bits(slab(2 * row + 1))
                s_ref[pl.ds(row, nk, stride=pitch), :] = lo | hi
            return carry

        lax.fori_loop(0, tm // SLAB_TOKENS, toks, 0)

    groups = EXP_CHUNK // nk
    slabs = []
    for t in range(groups):
        r0 = pl.multiple_of((j * groups + t) * pitch, SUBLANES)
        slabs.append(pltpu.bitcast(s_ref[pl.ds(r0, tm // 2), :], BF16))
    part = jnp.dot(jnp.concatenate(slabs, axis=1), v_ref[...], preferred_element_type=F32)

    @pl.when(j == 0)
    def _():
        acc_ref[...] = part

    @pl.when(j > 0)
    def _():
        acc_ref[...] += part

    @pl.when(j == pl.num_programs(1) - 1)
    def _():
        o_ref[...] = h_ref[...] + acc_ref[...]


def _expert_v(zsel, w, aidx, bidx, v_b, h1, tm=512):
    n = zsel.shape[0]
    nexp = v_b.shape[0]
    nslot = PEER_HEADS * PEER_TOPK
    slot = lambda: pl.BlockSpec((tm, nslot), lambda i, j: (i, 0))
    return pl.pallas_call(
        _expv_kernel,
        grid=(n // tm, nexp // EXP_CHUNK),
        in_specs=[slot(), slot(), slot(), slot(),
                  pl.BlockSpec((EXP_CHUNK, D_MODEL), lambda i, j: (j, 0)),
                  pl.BlockSpec((tm, D_MODEL), lambda i, j: (i, 0))],
        out_specs=pl.BlockSpec((tm, D_MODEL), lambda i, j: (i, 0)),
        out_shape=jax.ShapeDtypeStruct((n, D_MODEL), F32),
        scratch_shapes=[pltpu.VMEM((PEER_NKEYS * (tm // 2 + SLAB_PAD), PEER_NKEYS), U32),
                        pltpu.VMEM((tm, D_MODEL), F32),
                        pltpu.VMEM((tm, nslot), F32)],
        compiler_params=_cparams(("parallel", "arbitrary")),
        name="peer_expert_v",
    )(zsel, w, aidx, bidx, v_b, h1)


def _ple_kernel(h_ref, p_ref, gp_ref, wg_ref, wp_ref, gf_ref, o_ref):
    h = h_ref[...]
    xn = _rms(h, gp_ref[...]).astype(BF16)
    gate = jax.nn.sigmoid(jnp.dot(xn, wg_ref[...], preferred_element_type=F32))
    proj = jnp.dot(p_ref[...].astype(BF16), wp_ref[...], preferred_element_type=F32)
    o_ref[...] = _rms(h + gate * proj, gf_ref[...])


def _ple(h2, p2, g_ple, wg_b, wp_b, g_final, tm=512):
    n = h2.shape[0]
    return pl.pallas_call(
        _ple_kernel,
        grid=(n // tm,),
        in_specs=[pl.BlockSpec((tm, D_MODEL), lambda i: (i, 0)),
                  pl.BlockSpec((tm, PLE_DIM), lambda i: (i, 0)),
                  pl.BlockSpec((1, D_MODEL), lambda i: (0, 0)),
                  pl.BlockSpec((D_MODEL, D_MODEL), lambda i: (0, 0)),
                  pl.BlockSpec((PLE_DIM, D_MODEL), lambda i: (0, 0)),
                  pl.BlockSpec((1, D_MODEL), lambda i: (0, 0))],
        out_specs=pl.BlockSpec((tm, D_MODEL), lambda i: (i, 0)),
        out_shape=jax.ShapeDtypeStruct((n, D_MODEL), F32),
        compiler_params=_cparams(("parallel",)),
        name="ple_final",
    )(h2, p2, g_ple, wg_b, wp_b, g_final)


def _block_diag(w):
    g, d, _ = w.shape
    eye = jnp.eye(g, dtype=w.dtype)
    return (eye[:, None, :, None] * w[:, :, None, :]).reshape(g * d, g * d)


def kernel(x, p, g_mix, w_in, conv_w, conv_b, w_rg, b_rg, w_ig, b_ig, lru_lambda, g_lru_out,
           g_attn_out, w_out, rel_bias, g_ffn, peer_wq, peer_k1, peer_k2, peer_u, peer_v,
           g_ple, w_ple_gate, w_ple_proj, g_final):
    batch, seq, dm = x.shape
    n = batch * seq
    assert w_in.shape[0] == 1, "single-layer problem: the last kernel fuses the final norm"
    row = lambda v: v.reshape(1, -1).astype(F32)
    btab = _bias_tables(rel_bias)
    h = x.reshape(n, dm)
    for i in range(1):
        xl, gate, q, k, v = _inproj(h, row(g_mix[i]), w_in[i].astype(BF16))
        y_lru = _lru(xl, gate, conv_w[i], row(conv_b[i]),
                     _block_diag(w_rg[i]).astype(BF16), row(b_rg[i]),
                     _block_diag(w_ig[i]).astype(BF16), row(b_ig[i]),
                     row(lru_lambda[i]), row(g_lru_out[i]), batch, seq)
        y_att = _attention(q, k, v, btab, batch, seq)
        h1, xn = _outproj(y_lru, y_att, h, row(g_attn_out[i]), w_out[i].astype(BF16),
                          row(g_ffn[i]))
        aidx, bidx, gates = _route2(xn, peer_wq[i].T.astype(BF16), peer_k1[i].astype(BF16),
                                   peer_k2[i].astype(BF16))
        zsel = _expert_u(xn, peer_u[i].T.astype(BF16), aidx, bidx)
        h2 = _expert_v(zsel, gates, aidx, bidx, peer_v[i].astype(BF16), h1)
        h = _ple(h2, p[i].reshape(n, -1), row(g_ple[i]), w_ple_gate[i].astype(BF16),
                 w_ple_proj[i].astype(BF16), g_final.reshape(1, -1))
    return h.reshape(batch, seq, dm)
```

```python
import functools
import math

import jax
import jax.numpy as jnp
import numpy as np
from jax import lax
from jax.experimental import pallas as pl
from jax.experimental.pallas import tpu as pltpu

F32 = jnp.float32
BF16 = jnp.bfloat16
I32 = jnp.int32

D_MODEL = 1024
PLE_DIM = 256
D_LRU = 512
LRU_BLOCKS = 8
CONV_WIDTH = 4
LRU_C = 8.0
D_ATTN = 512
N_ATT_HEADS = 8
HEAD_DIM = 64
DILATED_GROUPS = ((128, 1), (512, 4), (2048, 16))
ATT_BLOCK = 128
REL_BUCKETS = 32
REL_MAX_DIST = 2048
PEER_HEADS = 8
PEER_NKEYS = 128
PEER_QDIM = 256
PEER_TOPK = 16
NORM_EPS = 1e-6

LANES = 128
SUBLANES = 8
NEG_BIG = -1e30
VMEM_LIMIT = 56 * 1024 * 1024


def _cparams(sem, flags=None):
    return pltpu.CompilerParams(dimension_semantics=sem, vmem_limit_bytes=VMEM_LIMIT, flags=flags)


def _rms(x, g):
    ms = jnp.mean(x * x, axis=-1, keepdims=True)
    return (x * lax.rsqrt(ms + NORM_EPS)) * g


def _gelu(x):
    c = math.sqrt(2.0 / math.pi)
    return x * (0.5 * (1.0 + jnp.tanh(c * (x + 0.044715 * (x * x * x)))))


def _inproj_kernel(x_ref, g_ref, w_ref, xl_ref, gate_ref, q_ref, k_ref, v_ref):
    u = _rms(x_ref[...], g_ref[...]).astype(BF16)
    outs = (xl_ref, gate_ref, q_ref, k_ref, v_ref)
    for i, o_ref in enumerate(outs):
        z = jnp.dot(u, w_ref[:, i * 512:(i + 1) * 512], preferred_element_type=F32)
        if o_ref is q_ref:
            z = z * (HEAD_DIM ** -0.5)
        o_ref[...] = z


def _inproj(x2, g_mix, w_in_b, tm=512):
    n = x2.shape[0]
    blk = lambda: pl.BlockSpec((tm, 512), lambda i: (i, 0))
    return pl.pallas_call(
        _inproj_kernel,
        grid=(n // tm,),
        in_specs=[pl.BlockSpec((tm, D_MODEL), lambda i: (i, 0)),
                  pl.BlockSpec((1, D_MODEL), lambda i: (0, 0)),
                  pl.BlockSpec((D_MODEL, 2560), lambda i: (0, 0))],
        out_specs=[blk() for _ in range(5)],
        out_shape=[jax.ShapeDtypeStruct((n, 512), F32) for _ in range(5)],
        compiler_params=_cparams(("parallel",)),
        name="inproj",
    )(x2, g_mix, w_in_b)


LRU_CHUNK = 64


def _lru_kernel(xl_ref, gate_ref, cw_ref, cb_ref, wrg_ref, brg_ref, wig_ref, big_ref,
                lam_ref, g_ref, o_ref, xp_ref, h_ref):
    T = xl_ref.shape[0]
    tc = LRU_CHUNK
    xp_ref[0:SUBLANES, :] = jnp.zeros((SUBLANES, D_LRU), F32)
    xp_ref[SUBLANES:SUBLANES + T, :] = xl_ref[...]
    h_ref[...] = jnp.zeros_like(h_ref)
    nlam = -lam_ref[...]
    sp = jnp.maximum(nlam, 0.0) + jnp.log1p(jnp.exp(-jnp.abs(nlam)))
    row = lax.broadcasted_iota(I32, (tc, D_LRU), 0)

    def chunk(c, carry):
        t0 = pl.multiple_of(c * tc, tc)
        xa = xp_ref[pl.ds(t0, tc + SUBLANES), :]
        xc = cb_ref[...] + cw_ref[CONV_WIDTH - 1:CONV_WIDTH, :] * xa[SUBLANES:, :]
        for j in range(CONV_WIDTH - 1):
            sh = pltpu.roll(xa, CONV_WIDTH - 1 - j, axis=0)[SUBLANES:, :]
            xc = xc + cw_ref[j:j + 1, :] * sh
        xcb = xc.astype(BF16)
        r = jax.nn.sigmoid(jnp.dot(xcb, wrg_ref[...], preferred_element_type=F32) + brg_ref[...])
        ig = jax.nn.sigmoid(jnp.dot(xcb, wig_ref[...], preferred_element_type=F32) + big_ref[...])
        log_a = (-LRU_C) * r * sp
        a = jnp.exp(log_a)
        b = jnp.sqrt(-jnp.tanh(log_a) * (a * a + 1.0)) * (ig * xc)
        d = 1
        while d < tc:
            keep = row >= d
            a_sh = jnp.where(keep, pltpu.roll(a, d, axis=0), 1.0)
            b_sh = jnp.where(keep, pltpu.roll(b, d, axis=0), 0.0)
            b = a * b_sh + b
            a = a * a_sh
            d *= 2
        h = b + a * h_ref[...]
        h_ref[...] = h[tc - 1:tc, :]
        y = h * _gelu(gate_ref[pl.ds(t0, tc), :])
        o_ref[pl.ds(t0, tc), :] = _rms(y, g_ref[...]).astype(BF16)
        return carry

    lax.fori_loop(0, T // tc, chunk, 0)


def _lru(xl, gate, conv_w, conv_b, wrg_bd, b_rg, wig_bd, b_ig, lam, g_lru, batch, seq):
    vec = lambda: pl.BlockSpec((1, D_LRU), lambda b: (0, 0))
    mat = lambda: pl.BlockSpec((D_LRU, D_LRU), lambda b: (0, 0))
    return pl.pallas_call(
        _lru_kernel,
        grid=(batch,),
        in_specs=[pl.BlockSpec((seq, D_LRU), lambda b: (b, 0)),
                  pl.BlockSpec((seq, D_LRU), lambda b: (b, 0)),
                  pl.BlockSpec((CONV_WIDTH, D_LRU), lambda b: (0, 0)),
                  vec(), mat(), vec(), mat(), vec(), vec(), vec()],
        out_specs=pl.BlockSpec((seq, D_LRU), lambda b: (b, 0)),
        out_shape=jax.ShapeDtypeStruct((batch * seq, D_LRU), BF16),
        scratch_shapes=[pltpu.VMEM((seq + SUBLANES, D_LRU), F32),
                        pltpu.VMEM((1, D_LRU), F32)],
        compiler_params=_cparams(("parallel",)),
        name="rglru",
    )(xl, gate, conv_w, conv_b, wrg_bd, b_rg, wig_bd, b_ig, lam, g_lru)


ATT_UNROLL = 8


def _attend(qf, kf, vf, bias_a, bias_b, is_a):
    kb = kf.astype(BF16)
    vb = vf.astype(BF16)
    dn = (((1,), (1,)), ((), ()))
    outs = []
    for q_h, bias in ((jnp.where(is_a, qf, 0.0), bias_a), (jnp.where(is_a, 0.0, qf), bias_b)):
        s = lax.dot_general(q_h.astype(BF16), kb, dn, preferred_element_type=F32) + bias
        m = jnp.max(s, axis=-1, keepdims=True)
        e = jnp.exp(s - m)
        den = jnp.sum(e, axis=-1, keepdims=True)
        num = jnp.dot(e.astype(BF16), vb, preferred_element_type=F32)
        outs.append((num, m, den))
    (na, ma, da), (nb, mb, db) = outs
    shape = na.shape
    return (jnp.where(is_a, na, nb),
            jnp.where(is_a, jnp.broadcast_to(ma, shape), jnp.broadcast_to(mb, shape)),
            jnp.where(is_a, jnp.broadcast_to(da, shape), jnp.broadcast_to(db, shape)))


def _attn_kernel(q_ref, k_ref, v_ref, bt_ref, o_ref, num_ref, m_ref, den_ref):
    T = q_ref.shape[0]
    blk = ATT_BLOCK
    is_a = lax.broadcasted_iota(I32, (blk, LANES), 1) < HEAD_DIM

    def rows(start, size, d):
        return pl.ds(start, size) if d == 1 else pl.ds(start, size, stride=d)

    def first_block(br, d, r):
        sl = rows(r, blk, d)
        pad = jnp.zeros((blk, LANES), F32)
        left = lax.broadcasted_iota(I32, (blk, 2 * blk), 1) < blk
        out = _attend(q_ref[sl, :], jnp.concatenate([pad, k_ref[sl, :]], axis=0),
                      jnp.concatenate([pad, v_ref[sl, :]], axis=0),
                      jnp.where(left, NEG_BIG, bt_ref[br, 0]),
                      jnp.where(left, NEG_BIG, bt_ref[br, 1]), is_a)
        for ref, val in zip((num_ref, m_ref, den_ref), out):
            ref[br, sl, :] = val

    def later_block(br, d, r, n):
        qs = rows(r + n * (blk * d), blk, d)
        ks = rows(r + (n - 1) * (blk * d), 2 * blk, d)
        out = _attend(q_ref[qs, :], k_ref[ks, :], v_ref[ks, :],
                      bt_ref[br, 0], bt_ref[br, 1], is_a)
        for ref, val in zip((num_ref, m_ref, den_ref), out):
            ref[br, qs, :] = val

    for br, (_, d) in enumerate(DILATED_GROUPS):
        nb = T // (d * blk)
        def firsts(r, carry, br=br, d=d):
            first_block(br, d, r)
            return carry

        def laters(i, carry, br=br, d=d, nb=nb):
            r = lax.div(i, nb - 1)
            later_block(br, d, r, 1 + lax.rem(i, nb - 1))
            return carry

        if d == 1:
            first_block(br, d, 0)
        else:
            lax.fori_loop(0, d, firsts, 0, unroll=ATT_UNROLL)
        if nb > 1:
            lax.fori_loop(0, d * (nb - 1), laters, 0, unroll=ATT_UNROLL)

    def merge(c, carry):
        sl = pl.ds(pl.multiple_of(c * blk, blk), blk)
        m0, m1, m2 = m_ref[0, sl, :], m_ref[1, sl, :], m_ref[2, sl, :]
        mm = jnp.maximum(jnp.maximum(m0, m1), m2)
        w0, w1, w2 = jnp.exp(m0 - mm), jnp.exp(m1 - mm), jnp.exp(m2 - mm)
        num = w0 * num_ref[0, sl, :] + w1 * num_ref[1, sl, :] + w2 * num_ref[2, sl, :]
        den = w0 * den_ref[0, sl, :] + w1 * den_ref[1, sl, :] + w2 * den_ref[2, sl, :]
        o_ref[sl, :] = num / den
        return carry

    lax.fori_loop(0, T // blk, merge, 0)


def _attention(q, k, v, btab, batch, seq):
    npairs = D_ATTN // LANES
    qkv = lambda: pl.BlockSpec((seq, LANES), lambda b, p: (b, p))
    return pl.pallas_call(
        _attn_kernel,
        grid=(batch, npairs),
        in_specs=[qkv(), qkv(), qkv(),
                  pl.BlockSpec((3, 2, ATT_BLOCK, 2 * ATT_BLOCK), lambda b, p: (0, p, 0, 0))],
        out_specs=pl.BlockSpec((seq, LANES), lambda b, p: (b, p)),
        out_shape=jax.ShapeDtypeStruct((batch * seq, D_ATTN), F32),
        scratch_shapes=[pltpu.VMEM((3, seq, LANES), F32) for _ in range(3)],
        compiler_params=_cparams(("parallel", "parallel")),
        name="dilated_attn",
    )(q, k, v, btab)


def _bias_tables(rel_bias):
    tabs = []
    for window, d in DILATED_GROUPS:
        w_sub = window // d
        qi = jnp.arange(ATT_BLOCK)[:, None]
        kj = jnp.arange(2 * ATT_BLOCK)[None, :]
        dist = qi + ATT_BLOCK - kj
        max_exact = REL_BUCKETS // 2
        nn = jnp.maximum(dist * d, 0)
        nf = jnp.maximum(nn, 1).astype(F32)
        large = max_exact + (jnp.log(nf / max_exact) / math.log(REL_MAX_DIST / max_exact)
                             * (REL_BUCKETS - max_exact)).astype(I32)
        large = jnp.minimum(large, REL_BUCKETS - 1)
        bucket = jnp.where(nn < max_exact, nn, large)
        rb = rel_bias.astype(F32)
        bias = jnp.zeros((N_ATT_HEADS,) + bucket.shape, F32)
        for bkt in range(REL_BUCKETS):
            bias = jnp.where(bucket[None] == bkt, rb[bkt][:, None, None], bias)
        ok = (dist >= 0) & (dist <= w_sub)
        tabs.append(jnp.where(ok[None], bias, NEG_BIG))
    return jnp.stack(tabs, axis=0)


def _outproj_kernel(yl_ref, ya_ref, x_ref, ga_ref, w_ref, gf_ref, h_ref, xn_ref):
    ya = _rms(ya_ref[...], ga_ref[...]).astype(BF16)
    acc = jnp.dot(yl_ref[...], w_ref[0:D_LRU, :], preferred_element_type=F32)
    acc = acc + jnp.dot(ya, w_ref[D_LRU:, :], preferred_element_type=F32)
    h = x_ref[...] + acc
    h_ref[...] = h
    xn_ref[...] = _rms(h, gf_ref[...]).astype(BF16)


def _outproj(y_lru, y_att, x2, g_att, w_out_b, g_ffn, tm=512):
    n = x2.shape[0]
    return pl.pallas_call(
        _outproj_kernel,
        grid=(n // tm,),
        in_specs=[pl.BlockSpec((tm, D_LRU), lambda i: (i, 0)),
                  pl.BlockSpec((tm, D_ATTN), lambda i: (i, 0)),
                  pl.BlockSpec((tm, D_MODEL), lambda i: (i, 0)),
                  pl.BlockSpec((1, D_ATTN), lambda i: (0, 0)),
                  pl.BlockSpec((D_MODEL, D_MODEL), lambda i: (0, 0)),
                  pl.BlockSpec((1, D_MODEL), lambda i: (0, 0))],
        out_specs=[pl.BlockSpec((tm, D_MODEL), lambda i: (i, 0)),
                   pl.BlockSpec((tm, D_MODEL), lambda i: (i, 0))],
        out_shape=[jax.ShapeDtypeStruct((n, D_MODEL), F32),
                   jax.ShapeDtypeStruct((n, D_MODEL), BF16)],
        compiler_params=_cparams(("parallel",)),
        name="outproj",
    )(y_lru, y_att, x2, g_att, w_out_b, g_ffn)


def _tree(op, xs):
    xs = list(xs)
    while len(xs) > 1:
        xs = [op(xs[i], xs[i + 1]) for i in range(0, len(xs) - 1, 2)] + (xs[-1:] if len(xs) % 2 else [])
    return xs[0]


RK_TM = 1024
RK_G = RK_TM // LANES
RK_NACC = 4
_CAND_CELLS = tuple((i, j) for i in range(PEER_TOPK) for j in range(PEER_TOPK)
                    if (i + 1) * (j + 1) <= PEER_TOPK)


def _extract_rows(ref, chain_ids, ids, big, nrounds, emit):
    n = len(ids)
    tile = lambda r: slice(r * RK_G, (r + 1) * RK_G)

    def fold(op, terms):
        accs = []
        for r, t in enumerate(terms):
            if r < RK_NACC:
                accs.append(t)
            else:
                accs[r % RK_NACC] = op(accs[r % RK_NACC], t)
        return _tree(op, accs)

    def rnd(k, carry):
        for c in chain_ids:
            m = fold(jnp.maximum, (ref[c, tile(r), :] for r in range(n)))
            sel = fold(jnp.minimum,
                       (jnp.where(ref[c, tile(r), :] == m, ids[r], big) for r in range(n)))
            emit(c, k, m, sel)
            for r in range(n):
                ref[c, tile(r), :] = jnp.where(sel == ids[r], NEG_BIG, ref[c, tile(r), :])
        return carry

    lax.fori_loop(0, nrounds, rnd, 0)


def _route_kernel(xn_ref, wqt_ref, k1_ref, k2_ref, a_ref, b_ref, w_ref,
                   qt_ref, s_ref, tv_ref, ti_ref, cand_ref, cv_ref, cp_ref, oa_ref, ob_ref, ow_ref):
    half = PEER_QDIM // 2
    dn = (((1,), (1,)), ((), ()))
    qt_ref[...] = lax.dot_general(wqt_ref[...], xn_ref[...], dn, preferred_element_type=F32)
    key_ids = [float(r) for r in range(PEER_NKEYS)]
    cell_ids = [float(i * PEER_TOPK + j) for i, j in _CAND_CELLS]

    def head(h, carry):
        for c, k_ref in enumerate((k1_ref, k2_ref)):
            row0 = pl.multiple_of(h * PEER_QDIM + c * half, half)
            s = jnp.dot(k_ref[...], qt_ref[pl.ds(row0, half), :].astype(BF16),
                        preferred_element_type=F32)
            for g in range(RK_G):
                s_ref[c, pl.ds(g, PEER_NKEYS, stride=RK_G), :] = s[:, g * LANES:(g + 1) * LANES]

        def emit1(c, k, m, sel):
            tv_ref[c, k] = m
            ti_ref[c, k] = sel

        _extract_rows(s_ref, (0, 1), key_ids, float(PEER_NKEYS), PEER_TOPK, emit1)

        for n, (i, j) in enumerate(_CAND_CELLS):
            cand_ref[0, n * RK_G:(n + 1) * RK_G, :] = tv_ref[0, i] + tv_ref[1, j]

        def emit2(c, k, m, sel):
            cv_ref[k] = m
            cp_ref[k] = sel

        _extract_rows(cand_ref, (0,), cell_ids, float(PEER_TOPK * PEER_TOPK), PEER_TOPK, emit2)

        top = [cv_ref[k] for k in range(PEER_TOPK)]
        es = [jnp.exp(t - top[0]) for t in top]
        inv = 1.0 / _tree(jnp.add, es)
        for k in range(PEER_TOPK):
            pos = cp_ref[k].astype(I32)
            ri = lax.shift_right_logical(pos, 4)
            rj = lax.bitwise_and(pos, PEER_TOPK - 1)
            a_sel = jnp.zeros((RK_G, LANES), F32)
            b_sel = jnp.zeros((RK_G, LANES), F32)
            for r in range(PEER_TOPK):
                a_sel = jnp.where(ri == r, ti_ref[0, r], a_sel)
                b_sel = jnp.where(rj == r, ti_ref[1, r], b_sel)
            rows = pl.ds(pl.multiple_of((h * PEER_TOPK + k) * RK_G, RK_G), RK_G)
            oa_ref[rows, :] = a_sel
            ob_ref[rows, :] = b_sel
            ow_ref[rows, :] = es[k] * inv
        return carry

    lax.fori_loop(0, PEER_HEADS, head, 0)
    nslot = PEER_HEADS * PEER_TOPK
    for g in range(RK_G):
        rows = pl.ds(g, nslot, stride=RK_G)
        tok = slice(g * LANES, (g + 1) * LANES)
        a_ref[tok, :] = oa_ref[rows, :].T.astype(I32)
        b_ref[tok, :] = ob_ref[rows, :].T.astype(I32)
        w_ref[tok, :] = ow_ref[rows, :].T


def _route(xn, wqt_b, k1_b, k2_b):
    n = xn.shape[0]
    tm = RK_TM
    nslot = PEER_HEADS * PEER_TOPK
    out = lambda: pl.BlockSpec((tm, nslot), lambda i: (i, 0))
    vregs = lambda *lead: pltpu.VMEM(lead + (RK_G, LANES), F32)
    tiles = lambda *lead: pltpu.VMEM(lead[:-1] + (lead[-1] * RK_G, LANES), F32)
    return pl.pallas_call(
        _route_kernel,
        grid=(n // tm,),
        in_specs=[pl.BlockSpec((tm, D_MODEL), lambda i: (i, 0)),
                  pl.BlockSpec((PEER_HEADS * PEER_QDIM, D_MODEL), lambda i: (0, 0)),
                  pl.BlockSpec((PEER_NKEYS, PEER_QDIM // 2), lambda i: (0, 0)),
                  pl.BlockSpec((PEER_NKEYS, PEER_QDIM // 2), lambda i: (0, 0))],
        out_specs=[out(), out(), out()],
        out_shape=[jax.ShapeDtypeStruct((n, nslot), I32),
                   jax.ShapeDtypeStruct((n, nslot), I32),
                   jax.ShapeDtypeStruct((n, nslot), F32)],
        scratch_shapes=[pltpu.VMEM((PEER_HEADS * PEER_QDIM, tm), F32),
                        tiles(2, PEER_NKEYS), vregs(2, PEER_TOPK), vregs(2, PEER_TOPK),
                        tiles(1, len(_CAND_CELLS)), vregs(PEER_TOPK), vregs(PEER_TOPK),
                        tiles(nslot), tiles(nslot), tiles(nslot)],
        compiler_params=_cparams(("parallel",)),
        name="peer_route",
    )(xn, wqt_b, k1_b, k2_b)


EXP_CHUNK = 2048
EXP_NCHUNK = PEER_NKEYS * PEER_NKEYS // EXP_CHUNK
PAIR = 2 * PEER_NKEYS


def _expu_kernel(xn_ref, u_ref, a_ref, b_ref, z_ref, buf0_ref, buf1_ref):
    tm = xn_ref.shape[0]
    j = pl.program_id(1)
    nj = pl.num_programs(1) - 1
    groups_per_chunk = EXP_CHUNK // PEER_NKEYS

    def matmul_into(buf_ref):
        buf_ref[...] = lax.dot_general(xn_ref[...], u_ref[...], (((1,), (1,)), ((), ())),
                                       preferred_element_type=F32)

    def gather_from(buf_ref):
        a0 = (j - 1) * groups_per_chunk
        for g in range(tm // SUBLANES):
            rs = slice(g * SUBLANES, (g + 1) * SUBLANES)
            bi = b_ref[rs, :]
            ai = a_ref[rs, :] - a0
            acc = z_ref[rs, :]
            for t in range(groups_per_chunk):
                got = jnp.take_along_axis(buf_ref[rs, t * PEER_NKEYS:(t + 1) * PEER_NKEYS], bi,
                                          axis=1, mode="promise_in_bounds")
                acc = jnp.where(ai == t, got, acc)
            z_ref[rs, :] = acc

    odd = lax.rem(j, 2) == 1
    inner = jnp.logical_and(j > 0, j < nj)

    @pl.when(j == 0)
    def _():
        z_ref[...] = jnp.zeros_like(z_ref)
        matmul_into(buf0_ref)

    @pl.when(jnp.logical_and(inner, odd))
    def _():
        gather_from(buf0_ref)
        matmul_into(buf1_ref)

    @pl.when(jnp.logical_and(inner, jnp.logical_not(odd)))
    def _():
        gather_from(buf1_ref)
        matmul_into(buf0_ref)

    @pl.when(j == nj)
    def _():
        gather_from(buf1_ref if (EXP_NCHUNK - 1) % 2 == 1 else buf0_ref)


def _expert_u(xn, u_b, aidx, bidx, tm=1024):
    n = xn.shape[0]
    nexp = u_b.shape[0]
    assert nexp == EXP_NCHUNK * EXP_CHUNK
    nslot = PEER_HEADS * PEER_TOPK
    last = EXP_NCHUNK - 1
    return pl.pallas_call(
        _expu_kernel,
        grid=(n // tm, EXP_NCHUNK + 1),
        in_specs=[pl.BlockSpec((tm, D_MODEL), lambda i, j: (i, 0)),
                  pl.BlockSpec((EXP_CHUNK, D_MODEL), lambda i, j: (jnp.minimum(j, last), 0)),
                  pl.BlockSpec((tm, nslot), lambda i, j: (i, 0)),
                  pl.BlockSpec((tm, nslot), lambda i, j: (i, 0))],
        out_specs=pl.BlockSpec((tm, nslot), lambda i, j: (i, 0)),
        out_shape=jax.ShapeDtypeStruct((n, nslot), F32),
        scratch_shapes=[pltpu.VMEM((tm, EXP_CHUNK), F32), pltpu.VMEM((tm, EXP_CHUNK), F32)],
        compiler_params=_cparams(("parallel", "arbitrary")),
        name="peer_expert_u",
    )(xn, u_b, aidx, bidx)


SLAB_PAD = SUBLANES
SLAB_TOKENS = 64
U32 = jnp.uint32


def _bf16_bits(x):
    return pltpu.bitcast(x.astype(BF16).astype(F32), U32)


def _expv_kernel(z_ref, w_ref, a_ref, b_ref, v_ref, h_ref, o_ref, s_ref, acc_ref, c_ref):
    tm = z_ref.shape[0]
    nk = PEER_NKEYS
    pitch = tm // 2 + SLAB_PAD
    j = pl.program_id(1)

    @pl.when(j == 0)
    def _():
        c_ref[...] = w_ref[...] * _gelu(z_ref[...])
        sub = lax.broadcasted_iota(I32, (nk, nk), 0)
        dn = (((1,), (1,)), ((), ()))

        def slab(n):
            arow = a_ref[pl.ds(n, 1), :]
            brow = b_ref[pl.ds(n, 1), :]
            crow = c_ref[pl.ds(n, 1), :]
            oat = jnp.where(arow == sub, 1.0, 0.0).astype(BF16)
            wobt = jnp.where(brow == sub, crow, 0.0).astype(BF16)
            return lax.dot_general(oat, wobt, dn, preferred_element_type=F32)

        def toks(g, carry):
            base = g * (SLAB_TOKENS // 2)
            for t in range(SLAB_TOKENS // 2):
                row = base + t
                lo = lax.shift_right_logical(_bf16_bits(slab(2 * row)), jnp.uint32(16))
                hi = _bf16_bits(slab(2 * row + 1))
                s_ref[pl.ds(row, nk, stride=pitch), :] = lo | hi
            return carry

        lax.fori_loop(0, tm // SLAB_TOKENS, toks, 0)

    groups = EXP_CHUNK // nk
    slabs = []
    for t in range(groups):
        r0 = pl.multiple_of((j * groups + t) * pitch, SUBLANES)
        slabs.append(pltpu.bitcast(s_ref[pl.ds(r0, tm // 2), :], BF16))
    part = jnp.dot(jnp.concatenate(slabs, axis=1), v_ref[...], preferred_element_type=F32)

    @pl.when(j == 0)
    def _():
        acc_ref[...] = part

    @pl.when(j > 0)
    def _():
        acc_ref[...] += part

    @pl.when(j == pl.num_programs(1) - 1)
    def _():
        o_ref[...] = h_ref[...] + acc_ref[...]


def _expert_v(zsel, w, aidx, bidx, v_b, h1, tm=512):
    n = zsel.shape[0]
    nexp = v_b.shape[0]
    nslot = PEER_HEADS * PEER_TOPK
    slot = lambda: pl.BlockSpec((tm, nslot), lambda i, j: (i, 0))
    return pl.pallas_call(
        _expv_kernel,
        grid=(n // tm, nexp // EXP_CHUNK),
        in_specs=[slot(), slot(), slot(), slot(),
                  pl.BlockSpec((EXP_CHUNK, D_MODEL), lambda i, j: (j, 0)),
                  pl.BlockSpec((tm, D_MODEL), lambda i, j: (i, 0))],
        out_specs=pl.BlockSpec((tm, D_MODEL), lambda i, j: (i, 0)),
        out_shape=jax.ShapeDtypeStruct((n, D_MODEL), F32),
        scratch_shapes=[pltpu.VMEM((PEER_NKEYS * (tm // 2 + SLAB_PAD), PEER_NKEYS), U32),
                        pltpu.VMEM((tm, D_MODEL), F32),
                        pltpu.VMEM((tm, nslot), F32)],
        compiler_params=_cparams(("parallel", "arbitrary")),
        name="peer_expert_v",
    )(zsel, w, aidx, bidx, v_b, h1)


def _ple_kernel(h_ref, p_ref, gp_ref, wg_ref, wp_ref, gf_ref, o_ref):
    h = h_ref[...]
    xn = _rms(h, gp_ref[...]).astype(BF16)
    gate = jax.nn.sigmoid(jnp.dot(xn, wg_ref[...], preferred_element_type=F32))
    proj = jnp.dot(p_ref[...].astype(BF16), wp_ref[...], preferred_element_type=F32)
    o_ref[...] = _rms(h + gate * proj, gf_ref[...])


def _ple(h2, p2, g_ple, wg_b, wp_b, g_final, tm=512):
    n = h2.shape[0]
    return pl.pallas_call(
        _ple_kernel,
        grid=(n // tm,),
        in_specs=[pl.BlockSpec((tm, D_MODEL), lambda i: (i, 0)),
                  pl.BlockSpec((tm, PLE_DIM), lambda i: (i, 0)),
                  pl.BlockSpec((1, D_MODEL), lambda i: (0, 0)),
                  pl.BlockSpec((D_MODEL, D_MODEL), lambda i: (0, 0)),
                  pl.BlockSpec((PLE_DIM, D_MODEL), lambda i: (0, 0)),
                  pl.BlockSpec((1, D_MODEL), lambda i: (0, 0))],
        out_specs=pl.BlockSpec((tm, D_MODEL), lambda i: (i, 0)),
        out_shape=jax.ShapeDtypeStruct((n, D_MODEL), F32),
        compiler_params=_cparams(("parallel",)),
        name="ple_final",
    )(h2, p2, g_ple, wg_b, wp_b, g_final)


def _block_diag(w):
    g, d, _ = w.shape
    eye = jnp.eye(g, dtype=w.dtype)
    return (eye[:, None, :, None] * w[:, :, None, :]).reshape(g * d, g * d)


def kernel(x, p, g_mix, w_in, conv_w, conv_b, w_rg, b_rg, w_ig, b_ig, lru_lambda, g_lru_out,
           g_attn_out, w_out, rel_bias, g_ffn, peer_wq, peer_k1, peer_k2, peer_u, peer_v,
           g_ple, w_ple_gate, w_ple_proj, g_final):
    batch, seq, dm = x.shape
    n = batch * seq
    assert w_in.shape[0] == 1, "single-layer problem: the last kernel fuses the final norm"
    row = lambda v: v.reshape(1, -1).astype(F32)
    btab = _bias_tables(rel_bias)
    h = x.reshape(n, dm)
    for i in range(1):
        xl, gate, q, k, v = _inproj(h, row(g_mix[i]), w_in[i].astype(BF16))
        y_lru = _lru(xl, gate, conv_w[i], row(conv_b[i]),
                     _block_diag(w_rg[i]).astype(BF16), row(b_rg[i]),
                     _block_diag(w_ig[i]).astype(BF16), row(b_ig[i]),
                     row(lru_lambda[i]), row(g_lru_out[i]), batch, seq)
        y_att = _attention(q, k, v, btab, batch, seq)
        h1, xn = _outproj(y_lru, y_att, h, row(g_attn_out[i]), w_out[i].astype(BF16),
                          row(g_ffn[i]))
        aidx, bidx, gates = _route(xn, peer_wq[i].T.astype(BF16), peer_k1[i].astype(BF16),
                                   peer_k2[i].astype(BF16))
        zsel = _expert_u(xn, peer_u[i].astype(BF16), aidx, bidx)
        h2 = _expert_v(zsel, gates, aidx, bidx, peer_v[i].astype(BF16), h1)
        h = _ple(h2, p[i].reshape(n, -1), row(g_ple[i]), w_ple_gate[i].astype(BF16),
                 w_ple_proj[i].astype(BF16), g_final.reshape(1, -1))
    return h.reshape(batch, seq, dm)
```

```python
import functools
import math

import jax
import jax.numpy as jnp
import numpy as np
from jax import lax
from jax.experimental import pallas as pl
from jax.experimental.pallas import tpu as pltpu

F32 = jnp.float32
BF16 = jnp.bfloat16
I32 = jnp.int32

D_MODEL = 1024
PLE_DIM = 256
D_LRU = 512
LRU_BLOCKS = 8
CONV_WIDTH = 4
LRU_C = 8.0
D_ATTN = 512
N_ATT_HEADS = 8
HEAD_DIM = 64
DILATED_GROUPS = ((128, 1), (512, 4), (2048, 16))
ATT_BLOCK = 128
REL_BUCKETS = 32
REL_MAX_DIST = 2048
PEER_HEADS = 8
PEER_NKEYS = 128
PEER_QDIM = 256
PEER_TOPK = 16
NORM_EPS = 1e-6

LANES = 128
SUBLANES = 8
NEG_BIG = -1e30
VMEM_LIMIT = 56 * 1024 * 1024


def _cparams(sem, flags=None):
    return pltpu.CompilerParams(dimension_semantics=sem, vmem_limit_bytes=VMEM_LIMIT, flags=flags)


def _rms(x, g):
    ms = jnp.mean(x * x, axis=-1, keepdims=True)
    return (x * lax.rsqrt(ms + NORM_EPS)) * g


def _gelu(x):
    c = math.sqrt(2.0 / math.pi)
    return x * (0.5 * (1.0 + jnp.tanh(c * (x + 0.044715 * (x * x * x)))))


def _inproj_kernel(x_ref, g_ref, w_ref, xl_ref, gate_ref, q_ref, k_ref, v_ref):
    u = _rms(x_ref[...], g_ref[...]).astype(BF16)
    outs = (xl_ref, gate_ref, q_ref, k_ref, v_ref)
    for i, o_ref in enumerate(outs):
        z = jnp.dot(u, w_ref[:, i * 512:(i + 1) * 512], preferred_element_type=F32)
        if o_ref is q_ref:
            z = z * (HEAD_DIM ** -0.5)
        o_ref[...] = z


def _inproj(x2, g_mix, w_in_b, tm=512):
    n = x2.shape[0]
    blk = lambda: pl.BlockSpec((tm, 512), lambda i: (i, 0))
    return pl.pallas_call(
        _inproj_kernel,
        grid=(n // tm,),
        in_specs=[pl.BlockSpec((tm, D_MODEL), lambda i: (i, 0)),
                  pl.BlockSpec((1, D_MODEL), lambda i: (0, 0)),
                  pl.BlockSpec((D_MODEL, 2560), lambda i: (0, 0))],
        out_specs=[blk() for _ in range(5)],
        out_shape=[jax.ShapeDtypeStruct((n, 512), F32) for _ in range(5)],
        compiler_params=_cparams(("parallel",)),
        name="inproj",
    )(x2, g_mix, w_in_b)


LRU_CHUNK = 64
LRU_UNROLL = 4


def _lru_kernel(xl_ref, gate_ref, cw_ref, cb_ref, wrg_ref, brg_ref, wig_ref, big_ref,
                lam_ref, g_ref, o_ref, xp_ref, h_ref):
    T = xl_ref.shape[0]
    tc = LRU_CHUNK
    xp_ref[0:SUBLANES, :] = jnp.zeros((SUBLANES, D_LRU), F32)
    xp_ref[SUBLANES:SUBLANES + T, :] = xl_ref[...]
    h_ref[...] = jnp.zeros_like(h_ref)
    nlam = -lam_ref[...]
    sp = jnp.maximum(nlam, 0.0) + jnp.log1p(jnp.exp(-jnp.abs(nlam)))
    sub_row = lax.broadcasted_iota(I32, (tc // SUBLANES, SUBLANES, D_LRU), 1)

    def chunk(c, carry):
        t0 = pl.multiple_of(c * tc, tc)
        xa = xp_ref[pl.ds(t0, tc + SUBLANES), :]
        xc = cb_ref[...] + cw_ref[CONV_WIDTH - 1:CONV_WIDTH, :] * xa[SUBLANES:, :]
        for j in range(CONV_WIDTH - 1):
            sh = pltpu.roll(xa, CONV_WIDTH - 1 - j, axis=0)[SUBLANES:, :]
            xc = xc + cw_ref[j:j + 1, :] * sh
        xcb = xc.astype(BF16)
        r = jax.nn.sigmoid(jnp.dot(xcb, wrg_ref[...], preferred_element_type=F32) + brg_ref[...])
        ig = jax.nn.sigmoid(jnp.dot(xcb, wig_ref[...], preferred_element_type=F32) + big_ref[...])
        log_a = (-LRU_C) * r * sp
        a = jnp.exp(log_a)
        om = -jnp.tanh(log_a) * (a * a + 1.0)
        root = jnp.where(om > 0.0, om * lax.rsqrt(om), 0.0)
        b = root * (ig * xc)
        a = a.reshape(tc // SUBLANES, SUBLANES, D_LRU)
        b = b.reshape(tc // SUBLANES, SUBLANES, D_LRU)
        d = 1
        while d < SUBLANES:
            keep = sub_row >= d
            a_sh = jnp.where(keep, pltpu.roll(a, d, axis=1), 1.0)
            b_sh = jnp.where(keep, pltpu.roll(b, d, axis=1), 0.0)
            b = a * b_sh + b
            a = a * a_sh
            d *= 2
        last = h_ref[...]
        hs = []
        for g in range(tc // SUBLANES):
            hg = b[g] + a[g] * last
            hs.append(hg)
            last = hg[SUBLANES - 1:SUBLANES, :]
        h = jnp.concatenate(hs, axis=0)
        h_ref[...] = last
        y = h * _gelu(gate_ref[pl.ds(t0, tc), :])
        o_ref[pl.ds(t0, tc), :] = _rms(y, g_ref[...]).astype(BF16)
        return carry

    lax.fori_loop(0, T // tc, chunk, 0, unroll=LRU_UNROLL)


def _lru(xl, gate, conv_w, conv_b, wrg_bd, b_rg, wig_bd, b_ig, lam, g_lru, batch, seq):
    vec = lambda: pl.BlockSpec((1, D_LRU), lambda b: (0, 0))
    mat = lambda: pl.BlockSpec((D_LRU, D_LRU), lambda b: (0, 0))
    return pl.pallas_call(
        _lru_kernel,
        grid=(batch,),
        in_specs=[pl.BlockSpec((seq, D_LRU), lambda b: (b, 0)),
                  pl.BlockSpec((seq, D_LRU), lambda b: (b, 0)),
                  pl.BlockSpec((CONV_WIDTH, D_LRU), lambda b: (0, 0)),
                  vec(), mat(), vec(), mat(), vec(), vec(), vec()],
        out_specs=pl.BlockSpec((seq, D_LRU), lambda b: (b, 0)),
        out_shape=jax.ShapeDtypeStruct((batch * seq, D_LRU), BF16),
        scratch_shapes=[pltpu.VMEM((seq + SUBLANES, D_LRU), F32),
                        pltpu.VMEM((1, D_LRU), F32)],
        compiler_params=_cparams(("parallel",)),
        name="rglru",
    )(xl, gate, conv_w, conv_b, wrg_bd, b_rg, wig_bd, b_ig, lam, g_lru)


ATT_UNROLL = 8


def _attend(qf, kf, vf, bias_a, bias_b, is_a):
    kb = kf.astype(BF16)
    vb = vf.astype(BF16)
    dn = (((1,), (1,)), ((), ()))
    outs = []
    for q_h, bias in ((jnp.where(is_a, qf, 0.0), bias_a), (jnp.where(is_a, 0.0, qf), bias_b)):
        s = lax.dot_general(q_h.astype(BF16), kb, dn, preferred_element_type=F32) + bias
        m = jnp.max(s, axis=-1, keepdims=True)
        e = jnp.exp(s - m)
        den = jnp.sum(e, axis=-1, keepdims=True)
        num = jnp.dot(e.astype(BF16), vb, preferred_element_type=F32)
        outs.append((num, m, den))
    (na, ma, da), (nb, mb, db) = outs
    shape = na.shape
    return (jnp.where(is_a, na, nb),
            jnp.where(is_a, jnp.broadcast_to(ma, shape), jnp.broadcast_to(mb, shape)),
            jnp.where(is_a, jnp.broadcast_to(da, shape), jnp.broadcast_to(db, shape)))


def _attn_kernel(q_ref, k_ref, v_ref, bt_ref, o_ref, num_ref, m_ref, den_ref):
    T = q_ref.shape[0]
    blk = ATT_BLOCK
    is_a = lax.broadcasted_iota(I32, (blk, LANES), 1) < HEAD_DIM

    def rows(start, size, d):
        return pl.ds(start, size) if d == 1 else pl.ds(start, size, stride=d)

    def first_block(br, d, r):
        sl = rows(r, blk, d)
        pad = jnp.zeros((blk, LANES), F32)
        left = lax.broadcasted_iota(I32, (blk, 2 * blk), 1) < blk
        out = _attend(q_ref[sl, :], jnp.concatenate([pad, k_ref[sl, :]], axis=0),
                      jnp.concatenate([pad, v_ref[sl, :]], axis=0),
                      jnp.where(left, NEG_BIG, bt_ref[br, 0]),
                      jnp.where(left, NEG_BIG, bt_ref[br, 1]), is_a)
        for ref, val in zip((num_ref, m_ref, den_ref), out):
            ref[br, sl, :] = val

    def later_block(br, d, r, n):
        qs = rows(r + n * (blk * d), blk, d)
        ks = rows(r + (n - 1) * (blk * d), 2 * blk, d)
        out = _attend(q_ref[qs, :], k_ref[ks, :], v_ref[ks, :],
                      bt_ref[br, 0], bt_ref[br, 1], is_a)
        for ref, val in zip((num_ref, m_ref, den_ref), out):
            ref[br, qs, :] = val

    for br, (_, d) in enumerate(DILATED_GROUPS):
        nb = T // (d * blk)
        def firsts(r, carry, br=br, d=d):
            first_block(br, d, r)
            return carry

        def laters(i, carry, br=br, d=d, nb=nb):
            r = lax.div(i, nb - 1)
            later_block(br, d, r, 1 + lax.rem(i, nb - 1))
            return carry

        if d == 1:
            first_block(br, d, 0)
        else:
            lax.fori_loop(0, d, firsts, 0, unroll=ATT_UNROLL)
        if nb > 1:
            lax.fori_loop(0, d * (nb - 1), laters, 0, unroll=ATT_UNROLL)

    def merge(c, carry):
        sl = pl.ds(pl.multiple_of(c * blk, blk), blk)
        m0, m1, m2 = m_ref[0, sl, :], m_ref[1, sl, :], m_ref[2, sl, :]
        mm = jnp.maximum(jnp.maximum(m0, m1), m2)
        w0, w1, w2 = jnp.exp(m0 - mm), jnp.exp(m1 - mm), jnp.exp(m2 - mm)
        num = w0 * num_ref[0, sl, :] + w1 * num_ref[1, sl, :] + w2 * num_ref[2, sl, :]
        den = w0 * den_ref[0, sl, :] + w1 * den_ref[1, sl, :] + w2 * den_ref[2, sl, :]
        o_ref[sl, :] = num / den
        return carry

    lax.fori_loop(0, T // blk, merge, 0)


def _attention(q, k, v, btab, batch, seq):
    npairs = D_ATTN // LANES
    qkv = lambda: pl.BlockSpec((seq, LANES), lambda b, p: (b, p))
    return pl.pallas_call(
        _attn_kernel,
        grid=(batch, npairs),
        in_specs=[qkv(), qkv(), qkv(),
                  pl.BlockSpec((3, 2, ATT_BLOCK, 2 * ATT_BLOCK), lambda b, p: (0, p, 0, 0))],
        out_specs=pl.BlockSpec((seq, LANES), lambda b, p: (b, p)),
        out_shape=jax.ShapeDtypeStruct((batch * seq, D_ATTN), F32),
        scratch_shapes=[pltpu.VMEM((3, seq, LANES), F32) for _ in range(3)],
        compiler_params=_cparams(("parallel", "parallel")),
        name="dilated_attn",
    )(q, k, v, btab)


def _bias_tables(rel_bias):
    tabs = []
    for window, d in DILATED_GROUPS:
        w_sub = window // d
        qi = jnp.arange(ATT_BLOCK)[:, None]
        kj = jnp.arange(2 * ATT_BLOCK)[None, :]
        dist = qi + ATT_BLOCK - kj
        max_exact = REL_BUCKETS // 2
        nn = jnp.maximum(dist * d, 0)
        nf = jnp.maximum(nn, 1).astype(F32)
        large = max_exact + (jnp.log(nf / max_exact) / math.log(REL_MAX_DIST / max_exact)
                             * (REL_BUCKETS - max_exact)).astype(I32)
        large = jnp.minimum(large, REL_BUCKETS - 1)
        bucket = jnp.where(nn < max_exact, nn, large)
        rb = rel_bias.astype(F32)
        bias = jnp.zeros((N_ATT_HEADS,) + bucket.shape, F32)
        for bkt in range(REL_BUCKETS):
            bias = jnp.where(bucket[None] == bkt, rb[bkt][:, None, None], bias)
        ok = (dist >= 0) & (dist <= w_sub)
        tabs.append(jnp.where(ok[None], bias, NEG_BIG))
    return jnp.stack(tabs, axis=0)


def _outproj_kernel(yl_ref, ya_ref, x_ref, ga_ref, w_ref, gf_ref, h_ref, xn_ref):
    ya = _rms(ya_ref[...], ga_ref[...]).astype(BF16)
    acc = jnp.dot(yl_ref[...], w_ref[0:D_LRU, :], preferred_element_type=F32)
    acc = acc + jnp.dot(ya, w_ref[D_LRU:, :], preferred_element_type=F32)
    h = x_ref[...] + acc
    h_ref[...] = h
    xn_ref[...] = _rms(h, gf_ref[...]).astype(BF16)


def _outproj(y_lru, y_att, x2, g_att, w_out_b, g_ffn, tm=512):
    n = x2.shape[0]
    return pl.pallas_call(
        _outproj_kernel,
        grid=(n // tm,),
        in_specs=[pl.BlockSpec((tm, D_LRU), lambda i: (i, 0)),
                  pl.BlockSpec((tm, D_ATTN), lambda i: (i, 0)),
                  pl.BlockSpec((tm, D_MODEL), lambda i: (i, 0)),
                  pl.BlockSpec((1, D_ATTN), lambda i: (0, 0)),
                  pl.BlockSpec((D_MODEL, D_MODEL), lambda i: (0, 0)),
                  pl.BlockSpec((1, D_MODEL), lambda i: (0, 0))],
        out_specs=[pl.BlockSpec((tm, D_MODEL), lambda i: (i, 0)),
                   pl.BlockSpec((tm, D_MODEL), lambda i: (i, 0))],
        out_shape=[jax.ShapeDtypeStruct((n, D_MODEL), F32),
                   jax.ShapeDtypeStruct((n, D_MODEL), BF16)],
        compiler_params=_cparams(("parallel",)),
        name="outproj",
    )(y_lru, y_att, x2, g_att, w_out_b, g_ffn)


def _tree(op, xs):
    xs = list(xs)
    while len(xs) > 1:
        xs = [op(xs[i], xs[i + 1]) for i in range(0, len(xs) - 1, 2)] + (xs[-1:] if len(xs) % 2 else [])
    return xs[0]


RK_TM = 1024
RK_G = RK_TM // LANES
RK_NACC = 4
_CAND_CELLS = tuple((i, j) for i in range(PEER_TOPK) for j in range(PEER_TOPK)
                    if (i + 1) * (j + 1) <= PEER_TOPK)


def _extract_rows(ref, chain_ids, ids, big, nrounds, emit):
    n = len(ids)
    tile = lambda r: slice(r * RK_G, (r + 1) * RK_G)

    def fold(op, terms):
        accs = []
        for r, t in enumerate(terms):
            if r < RK_NACC:
                accs.append(t)
            else:
                accs[r % RK_NACC] = op(accs[r % RK_NACC], t)
        return _tree(op, accs)

    def rnd(k, carry):
        for c in chain_ids:
            m = fold(jnp.maximum, (ref[c, tile(r), :] for r in range(n)))
            sel = fold(jnp.minimum,
                       (jnp.where(ref[c, tile(r), :] == m, ids[r], big) for r in range(n)))
            emit(c, k, m, sel)
            for r in range(n):
                ref[c, tile(r), :] = jnp.where(sel == ids[r], NEG_BIG, ref[c, tile(r), :])
        return carry

    lax.fori_loop(0, nrounds, rnd, 0)


def _route_kernel(xn_ref, wqt_ref, k1_ref, k2_ref, a_ref, b_ref, w_ref,
                   qt_ref, s_ref, tv_ref, ti_ref, cand_ref, cv_ref, cp_ref, oa_ref, ob_ref, ow_ref):
    half = PEER_QDIM // 2
    dn = (((1,), (1,)), ((), ()))
    qt_ref[...] = lax.dot_general(wqt_ref[...], xn_ref[...], dn, preferred_element_type=F32)
    key_ids = [float(r) for r in range(PEER_NKEYS)]
    cell_ids = [float(i * PEER_TOPK + j) for i, j in _CAND_CELLS]

    def head(h, carry):
        for c, k_ref in enumerate((k1_ref, k2_ref)):
            row0 = pl.multiple_of(h * PEER_QDIM + c * half, half)
            s = jnp.dot(k_ref[...], qt_ref[pl.ds(row0, half), :].astype(BF16),
                        preferred_element_type=F32)
            for g in range(RK_G):
                s_ref[c, pl.ds(g, PEER_NKEYS, stride=RK_G), :] = s[:, g * LANES:(g + 1) * LANES]

        def emit1(c, k, m, sel):
            tv_ref[c, k] = m
            ti_ref[c, k] = sel

        _extract_rows(s_ref, (0, 1), key_ids, float(PEER_NKEYS), PEER_TOPK, emit1)

        for n, (i, j) in enumerate(_CAND_CELLS):
            cand_ref[0, n * RK_G:(n + 1) * RK_G, :] = tv_ref[0, i] + tv_ref[1, j]

        def emit2(c, k, m, sel):
            cv_ref[k] = m
            cp_ref[k] = sel

        _extract_rows(cand_ref, (0,), cell_ids, float(PEER_TOPK * PEER_TOPK), PEER_TOPK, emit2)

        top = [cv_ref[k] for k in range(PEER_TOPK)]
        es = [jnp.exp(t - top[0]) for t in top]
        inv = 1.0 / _tree(jnp.add, es)
        for k in range(PEER_TOPK):
            pos = cp_ref[k].astype(I32)
            ri = lax.shift_right_logical(pos, 4)
            rj = lax.bitwise_and(pos, PEER_TOPK - 1)
            a_sel = jnp.zeros((RK_G, LANES), F32)
            b_sel = jnp.zeros((RK_G, LANES), F32)
            for r in range(PEER_TOPK):
                a_sel = jnp.where(ri == r, ti_ref[0, r], a_sel)
                b_sel = jnp.where(rj == r, ti_ref[1, r], b_sel)
            rows = pl.ds(pl.multiple_of((h * PEER_TOPK + k) * RK_G, RK_G), RK_G)
            oa_ref[rows, :] = a_sel
            ob_ref[rows, :] = b_sel
            ow_ref[rows, :] = es[k] * inv
        return carry

    lax.fori_loop(0, PEER_HEADS, head, 0)
    nslot = PEER_HEADS * PEER_TOPK
    for g in range(RK_G):
        rows = pl.ds(g, nslot, stride=RK_G)
        tok = slice(g * LANES, (g + 1) * LANES)
        a_ref[tok, :] = oa_ref[rows, :].T.astype(I32)
        b_ref[tok, :] = ob_ref[rows, :].T.astype(I32)
        w_ref[tok, :] = ow_ref[rows, :].T


def _route(xn, wqt_b, k1_b, k2_b):
    n = xn.shape[0]
    tm = RK_TM
    nslot = PEER_HEADS * PEER_TOPK
    out = lambda: pl.BlockSpec((tm, nslot), lambda i: (i, 0))
    vregs = lambda *lead: pltpu.VMEM(lead + (RK_G, LANES), F32)
    tiles = lambda *lead: pltpu.VMEM(lead[:-1] + (lead[-1] * RK_G, LANES), F32)
    return pl.pallas_call(
        _route_kernel,
        grid=(n // tm,),
        in_specs=[pl.BlockSpec((tm, D_MODEL), lambda i: (i, 0)),
                  pl.BlockSpec((PEER_HEADS * PEER_QDIM, D_MODEL), lambda i: (0, 0)),
                  pl.BlockSpec((PEER_NKEYS, PEER_QDIM // 2), lambda i: (0, 0)),
                  pl.BlockSpec((PEER_NKEYS, PEER_QDIM // 2), lambda i: (0, 0))],
        out_specs=[out(), out(), out()],
        out_shape=[jax.ShapeDtypeStruct((n, nslot), I32),
                   jax.ShapeDtypeStruct((n, nslot), I32),
                   jax.ShapeDtypeStruct((n, nslot), F32)],
        scratch_shapes=[pltpu.VMEM((PEER_HEADS * PEER_QDIM, tm), F32),
                        tiles(2, PEER_NKEYS), vregs(2, PEER_TOPK), vregs(2, PEER_TOPK),
                        tiles(1, len(_CAND_CELLS)), vregs(PEER_TOPK), vregs(PEER_TOPK),
                        tiles(nslot), tiles(nslot), tiles(nslot)],
        compiler_params=_cparams(("parallel",)),
        name="peer_route",
    )(xn, wqt_b, k1_b, k2_b)


EXP_CHUNK = 2048
EXP_NCHUNK = PEER_NKEYS * PEER_NKEYS // EXP_CHUNK
EXPV_CHUNK = 4096


def _expu_kernel(xn_ref, u_ref, a_ref, b_ref, z_ref, buf0_ref, buf1_ref):
    tm = xn_ref.shape[0]
    j = pl.program_id(1)
    nj = pl.num_programs(1) - 1
    groups_per_chunk = EXP_CHUNK // PEER_NKEYS

    def matmul_into(buf_ref):
        buf_ref[...] = lax.dot_general(xn_ref[...], u_ref[...], (((1,), (1,)), ((), ())),
                                       preferred_element_type=F32)

    def gather_from(buf_ref):
        a0 = (j - 1) * groups_per_chunk
        for g in range(tm // SUBLANES):
            rs = slice(g * SUBLANES, (g + 1) * SUBLANES)
            bi = b_ref[rs, :]
            ai = a_ref[rs, :] - a0
            acc = z_ref[rs, :]
            for t in range(groups_per_chunk):
                got = jnp.take_along_axis(buf_ref[rs, t * PEER_NKEYS:(t + 1) * PEER_NKEYS], bi,
                                          axis=1, mode="promise_in_bounds")
                acc = jnp.where(ai == t, got, acc)
            z_ref[rs, :] = acc

    odd = lax.rem(j, 2) == 1
    inner = jnp.logical_and(j > 0, j < nj)

    @pl.when(j == 0)
    def _():
        z_ref[...] = jnp.zeros_like(z_ref)
        matmul_into(buf0_ref)

    @pl.when(jnp.logical_and(inner, odd))
    def _():
        gather_from(buf0_ref)
        matmul_into(buf1_ref)

    @pl.when(jnp.logical_and(inner, jnp.logical_not(odd)))
    def _():
        gather_from(buf1_ref)
        matmul_into(buf0_ref)

    @pl.when(j == nj)
    def _():
        gather_from(buf1_ref if (EXP_NCHUNK - 1) % 2 == 1 else buf0_ref)


def _expert_u(xn, u_b, aidx, bidx, tm=1024):
    n = xn.shape[0]
    nexp = u_b.shape[0]
    assert nexp == EXP_NCHUNK * EXP_CHUNK
    nslot = PEER_HEADS * PEER_TOPK
    last = EXP_NCHUNK - 1
    return pl.pallas_call(
        _expu_kernel,
        grid=(n // tm, EXP_NCHUNK + 1),
        in_specs=[pl.BlockSpec((tm, D_MODEL), lambda i, j: (i, 0)),
                  pl.BlockSpec((EXP_CHUNK, D_MODEL), lambda i, j: (jnp.minimum(j, last), 0)),
                  pl.BlockSpec((tm, nslot), lambda i, j: (i, 0)),
                  pl.BlockSpec((tm, nslot), lambda i, j: (i, 0))],
        out_specs=pl.BlockSpec((tm, nslot), lambda i, j: (i, 0)),
        out_shape=jax.ShapeDtypeStruct((n, nslot), F32),
        scratch_shapes=[pltpu.VMEM((tm, EXP_CHUNK), F32), pltpu.VMEM((tm, EXP_CHUNK), F32)],
        compiler_params=_cparams(("parallel", "arbitrary")),
        name="peer_expert_u",
    )(xn, u_b, aidx, bidx)


SLAB_PAD = SUBLANES
SLAB_TOKENS = 64
U32 = jnp.uint32


def _bf16_bits(x):
    return pltpu.bitcast(x.astype(BF16).astype(F32), U32)


def _expv_kernel(z_ref, w_ref, a_ref, b_ref, v_ref, h_ref, o_ref, s_ref, acc_ref, c_ref):
    tm = z_ref.shape[0]
    nk = PEER_NKEYS
    pitch = tm // 2 + SLAB_PAD
    j = pl.program_id(1)

    @pl.when(j == 0)
    def _():
        c_ref[...] = w_ref[...] * _gelu(z_ref[...])
        sub = lax.broadcasted_iota(I32, (nk, nk), 0)
        dn = (((1,), (1,)), ((), ()))

        def slab(n):
            arow = a_ref[pl.ds(n, 1), :]
            brow = b_ref[pl.ds(n, 1), :]
            crow = c_ref[pl.ds(n, 1), :]
            oat = jnp.where(arow == sub, 1.0, 0.0).astype(BF16)
            wobt = jnp.where(brow == sub, crow, 0.0).astype(BF16)
            return lax.dot_general(oat, wobt, dn, preferred_element_type=F32)

        def toks(g, carry):
            base = g * (SLAB_TOKENS // 2)
            for t in range(SLAB_TOKENS // 2):
                row = base + t
                lo = lax.shift_right_logical(_bf16_bits(slab(2 * row)), jnp.uint32(16))
                hi = _bf16_bits(slab(2 * row + 1))
                s_ref[pl.ds(row, nk, stride=pitch), :] = lo | hi
            return carry

        lax.fori_loop(0, tm // SLAB_TOKENS, toks, 0)

    groups = EXPV_CHUNK // nk
    slabs = []
    for t in range(groups):
        r0 = pl.multiple_of((j * groups + t) * pitch, SUBLANES)
        slabs.append(pltpu.bitcast(s_ref[pl.ds(r0, tm // 2), :], BF16))
    part = jnp.dot(jnp.concatenate(slabs, axis=1), v_ref[...], preferred_element_type=F32)

    @pl.when(j == 0)
    def _():
        acc_ref[...] = part

    @pl.when(j > 0)
    def _():
        acc_ref[...] += part

    @pl.when(j == pl.num_programs(1) - 1)
    def _():
        o_ref[...] = h_ref[...] + acc_ref[...]


def _expert_v(zsel, w, aidx, bidx, v_b, h1, tm=512):
    n = zsel.shape[0]
    nexp = v_b.shape[0]
    nslot = PEER_HEADS * PEER_TOPK
    slot = lambda: pl.BlockSpec((tm, nslot), lambda i, j: (i, 0))
    return pl.pallas_call(
        _expv_kernel,
        grid=(n // tm, nexp // EXPV_CHUNK),
        in_specs=[slot(), slot(), slot(), slot(),
                  pl.BlockSpec((EXPV_CHUNK, D_MODEL), lambda i, j: (j, 0)),
                  pl.BlockSpec((tm, D_MODEL), lambda i, j: (i, 0))],
        out_specs=pl.BlockSpec((tm, D_MODEL), lambda i, j: (i, 0)),
        out_shape=jax.ShapeDtypeStruct((n, D_MODEL), F32),
        scratch_shapes=[pltpu.VMEM((PEER_NKEYS * (tm // 2 + SLAB_PAD), PEER_NKEYS), U32),
                        pltpu.VMEM((tm, D_MODEL), F32),
                        pltpu.VMEM((tm, nslot), F32)],
        compiler_params=_cparams(("parallel", "arbitrary")),
        name="peer_expert_v",
    )(zsel, w, aidx, bidx, v_b, h1)


def _ple_kernel(h_ref, p_ref, gp_ref, wg_ref, wp_ref, gf_ref, o_ref):
    h = h_ref[...]
    xn = _rms(h, gp_ref[...]).astype(BF16)
    gate = jax.nn.sigmoid(jnp.dot(xn, wg_ref[...], preferred_element_type=F32))
    proj = jnp.dot(p_ref[...].astype(BF16), wp_ref[...], preferred_element_type=F32)
    o_ref[...] = _rms(h + gate * proj, gf_ref[...])


def _ple(h2, p2, g_ple, wg_b, wp_b, g_final, tm=512):
    n = h2.shape[0]
    return pl.pallas_call(
        _ple_kernel,
        grid=(n // tm,),
        in_specs=[pl.BlockSpec((tm, D_MODEL), lambda i: (i, 0)),
                  pl.BlockSpec((tm, PLE_DIM), lambda i: (i, 0)),
                  pl.BlockSpec((1, D_MODEL), lambda i: (0, 0)),
                  pl.BlockSpec((D_MODEL, D_MODEL), lambda i: (0, 0)),
                  pl.BlockSpec((PLE_DIM, D_MODEL), lambda i: (0, 0)),
                  pl.BlockSpec((1, D_MODEL), lambda i: (0, 0))],
        out_specs=pl.BlockSpec((tm, D_MODEL), lambda i: (i, 0)),
        out_shape=jax.ShapeDtypeStruct((n, D_MODEL), F32),
        compiler_params=_cparams(("parallel",)),
        name="ple_final",
    )(h2, p2, g_ple, wg_b, wp_b, g_final)


def _block_diag(w):
    g, d, _ = w.shape
    eye = jnp.eye(g, dtype=w.dtype)
    return (eye[:, None, :, None] * w[:, :, None, :]).reshape(g * d, g * d)


def kernel(x, p, g_mix, w_in, conv_w, conv_b, w_rg, b_rg, w_ig, b_ig, lru_lambda, g_lru_out,
           g_attn_out, w_out, rel_bias, g_ffn, peer_wq, peer_k1, peer_k2, peer_u, peer_v,
           g_ple, w_ple_gate, w_ple_proj, g_final):
    batch, seq, dm = x.shape
    n = batch * seq
    assert w_in.shape[0] == 1, "single-layer problem: the last kernel fuses the final norm"
    row = lambda v: v.reshape(1, -1).astype(F32)
    btab = _bias_tables(rel_bias)
    h = x.reshape(n, dm)
    for i in range(1):
        xl, gate, q, k, v = _inproj(h, row(g_mix[i]), w_in[i].astype(BF16))
        y_lru = _lru(xl, gate, conv_w[i], row(conv_b[i]),
                     _block_diag(w_rg[i]).astype(BF16), row(b_rg[i]),
                     _block_diag(w_ig[i]).astype(BF16), row(b_ig[i]),
                     row(lru_lambda[i]), row(g_lru_out[i]), batch, seq)
        y_att = _attention(q, k, v, btab, batch, seq)
        h1, xn = _outproj(y_lru, y_att, h, row(g_attn_out[i]), w_out[i].astype(BF16),
                          row(g_ffn[i]))
        aidx, bidx, gates = _route(xn, peer_wq[i].T.astype(BF16), peer_k1[i].astype(BF16),
                                   peer_k2[i].astype(BF16))
        zsel = _expert_u(xn, peer_u[i].astype(BF16), aidx, bidx)
        h2 = _expert_v(zsel, gates, aidx, bidx, peer_v[i].astype(BF16), h1)
        h = _ple(h2, p[i].reshape(n, -1), row(g_ple[i]), w_ple_gate[i].astype(BF16),
                 w_ple_proj[i].astype(BF16), g_final.reshape(1, -1))
    return h.reshape(batch, seq, dm)
```

```python
import functools
import math

import jax
import jax.numpy as jnp
import numpy as np
from jax import lax
from jax.experimental import pallas as pl
from jax.experimental.pallas import tpu as pltpu

F32 = jnp.float32
BF16 = jnp.bfloat16
I32 = jnp.int32

D_MODEL = 1024
PLE_DIM = 256
D_LRU = 512
LRU_BLOCKS = 8
CONV_WIDTH = 4
LRU_C = 8.0
D_ATTN = 512
N_ATT_HEADS = 8
HEAD_DIM = 64
DILATED_GROUPS = ((128, 1), (512, 4), (2048, 16))
ATT_BLOCK = 128
REL_BUCKETS = 32
REL_MAX_DIST = 2048
PEER_HEADS = 8
PEER_NKEYS = 128
PEER_QDIM = 256
PEER_TOPK = 16
NORM_EPS = 1e-6

LANES = 128
SUBLANES = 8
NEG_BIG = -1e30
VMEM_LIMIT = 56 * 1024 * 1024


def _cparams(sem, flags=None):
    return pltpu.CompilerParams(dimension_semantics=sem, vmem_limit_bytes=VMEM_LIMIT, flags=flags)


def _rms(x, g):
    ms = jnp.mean(x * x, axis=-1, keepdims=True)
    return (x * lax.rsqrt(ms + NORM_EPS)) * g


def _gelu(x):
    c = math.sqrt(2.0 / math.pi)
    return x * (0.5 * (1.0 + jnp.tanh(c * (x + 0.044715 * (x * x * x)))))


def _inproj_kernel(x_ref, g_ref, w_ref, xl_ref, gate_ref, q_ref, k_ref, v_ref):
    u = _rms(x_ref[...], g_ref[...]).astype(BF16)
    outs = (xl_ref, gate_ref, q_ref, k_ref, v_ref)
    for i, o_ref in enumerate(outs):
        z = jnp.dot(u, w_ref[:, i * 512:(i + 1) * 512], preferred_element_type=F32)
        if o_ref is q_ref:
            z = z * (HEAD_DIM ** -0.5)
        o_ref[...] = z


def _inproj(x2, g_mix, w_in_b, tm=1024):
    n = x2.shape[0]
    blk = lambda: pl.BlockSpec((tm, 512), lambda i: (i, 0))
    return pl.pallas_call(
        _inproj_kernel,
        grid=(n // tm,),
        in_specs=[pl.BlockSpec((tm, D_MODEL), lambda i: (i, 0)),
                  pl.BlockSpec((1, D_MODEL), lambda i: (0, 0)),
                  pl.BlockSpec((D_MODEL, 2560), lambda i: (0, 0))],
        out_specs=[blk() for _ in range(5)],
        out_shape=[jax.ShapeDtypeStruct((n, 512), F32) for _ in range(5)],
        compiler_params=_cparams(("parallel",)),
        name="inproj",
    )(x2, g_mix, w_in_b)


LRU_CHUNK = 64
LRU_UNROLL = 4


def _lru_kernel(xl_ref, gate_ref, cw_ref, cb_ref, wrg_ref, brg_ref, wig_ref, big_ref,
                lam_ref, g_ref, o_ref, xp_ref, h_ref):
    T = xl_ref.shape[0]
    tc = LRU_CHUNK
    xp_ref[0:SUBLANES, :] = jnp.zeros((SUBLANES, D_LRU), F32)
    xp_ref[SUBLANES:SUBLANES + T, :] = xl_ref[...]
    h_ref[...] = jnp.zeros_like(h_ref)
    nlam = -lam_ref[...]
    sp = jnp.maximum(nlam, 0.0) + jnp.log1p(jnp.exp(-jnp.abs(nlam)))
    sub_row = lax.broadcasted_iota(I32, (tc // SUBLANES, SUBLANES, D_LRU), 1)

    def chunk(c, carry):
        t0 = pl.multiple_of(c * tc, tc)
        xa = xp_ref[pl.ds(t0, tc + SUBLANES), :]
        xc = cb_ref[...] + cw_ref[CONV_WIDTH - 1:CONV_WIDTH, :] * xa[SUBLANES:, :]
        for j in range(CONV_WIDTH - 1):
            sh = pltpu.roll(xa, CONV_WIDTH - 1 - j, axis=0)[SUBLANES:, :]
            xc = xc + cw_ref[j:j + 1, :] * sh
        xcb = xc.astype(BF16)
        r = jax.nn.sigmoid(jnp.dot(xcb, wrg_ref[...], preferred_element_type=F32) + brg_ref[...])
        ig = jax.nn.sigmoid(jnp.dot(xcb, wig_ref[...], preferred_element_type=F32) + big_ref[...])
        log_a = (-LRU_C) * r * sp
        a = jnp.exp(log_a)
        om = -jnp.tanh(log_a) * (a * a + 1.0)
        root = jnp.where(om > 0.0, om * lax.rsqrt(om), 0.0)
        b = root * (ig * xc)
        a = a.reshape(tc // SUBLANES, SUBLANES, D_LRU)
        b = b.reshape(tc // SUBLANES, SUBLANES, D_LRU)
        d = 1
        while d < SUBLANES:
            keep = sub_row >= d
            a_sh = jnp.where(keep, pltpu.roll(a, d, axis=1), 1.0)
            b_sh = jnp.where(keep, pltpu.roll(b, d, axis=1), 0.0)
            b = a * b_sh + b
            a = a * a_sh
            d *= 2
        last = h_ref[...]
        hs = []
        for g in range(tc // SUBLANES):
            hg = b[g] + a[g] * last
            hs.append(hg)
            last = hg[SUBLANES - 1:SUBLANES, :]
        h = jnp.concatenate(hs, axis=0)
        h_ref[...] = last
        y = h * _gelu(gate_ref[pl.ds(t0, tc), :])
        o_ref[pl.ds(t0, tc), :] = _rms(y, g_ref[...]).astype(BF16)
        return carry

    lax.fori_loop(0, T // tc, chunk, 0, unroll=LRU_UNROLL)


def _lru(xl, gate, conv_w, conv_b, wrg_bd, b_rg, wig_bd, b_ig, lam, g_lru, batch, seq):
    vec = lambda: pl.BlockSpec((1, D_LRU), lambda b: (0, 0))
    mat = lambda: pl.BlockSpec((D_LRU, D_LRU), lambda b: (0, 0))
    return pl.pallas_call(
        _lru_kernel,
        grid=(batch,),
        in_specs=[pl.BlockSpec((seq, D_LRU), lambda b: (b, 0)),
                  pl.BlockSpec((seq, D_LRU), lambda b: (b, 0)),
                  pl.BlockSpec((CONV_WIDTH, D_LRU), lambda b: (0, 0)),
                  vec(), mat(), vec(), mat(), vec(), vec(), vec()],
        out_specs=pl.BlockSpec((seq, D_LRU), lambda b: (b, 0)),
        out_shape=jax.ShapeDtypeStruct((batch * seq, D_LRU), BF16),
        scratch_shapes=[pltpu.VMEM((seq + SUBLANES, D_LRU), F32),
                        pltpu.VMEM((1, D_LRU), F32)],
        compiler_params=_cparams(("parallel",)),
        name="rglru",
    )(xl, gate, conv_w, conv_b, wrg_bd, b_rg, wig_bd, b_ig, lam, g_lru)


ATT_UNROLL = 8


def _attend(qf, kf, vf, bias_a, bias_b, is_a):
    kb = kf.astype(BF16)
    vb = vf.astype(BF16)
    dn = (((1,), (1,)), ((), ()))
    outs = []
    for q_h, bias in ((jnp.where(is_a, qf, 0.0), bias_a), (jnp.where(is_a, 0.0, qf), bias_b)):
        s = lax.dot_general(q_h.astype(BF16), kb, dn, preferred_element_type=F32) + bias
        m = jnp.max(s, axis=-1, keepdims=True)
        e = jnp.exp(s - m)
        den = jnp.sum(e, axis=-1, keepdims=True)
        num = jnp.dot(e.astype(BF16), vb, preferred_element_type=F32)
        outs.append((num, m, den))
    (na, ma, da), (nb, mb, db) = outs
    shape = na.shape
    return (jnp.where(is_a, na, nb),
            jnp.where(is_a, jnp.broadcast_to(ma, shape), jnp.broadcast_to(mb, shape)),
            jnp.where(is_a, jnp.broadcast_to(da, shape), jnp.broadcast_to(db, shape)))


def _attn_kernel(q_ref, k_ref, v_ref, bt_ref, o_ref, num_ref, m_ref, den_ref):
    T = q_ref.shape[0]
    blk = ATT_BLOCK
    is_a = lax.broadcasted_iota(I32, (blk, LANES), 1) < HEAD_DIM

    def rows(start, size, d):
        return pl.ds(start, size) if d == 1 else pl.ds(start, size, stride=d)

    def first_block(br, d, r):
        sl = rows(r, blk, d)
        pad = jnp.zeros((blk, LANES), F32)
        left = lax.broadcasted_iota(I32, (blk, 2 * blk), 1) < blk
        out = _attend(q_ref[sl, :], jnp.concatenate([pad, k_ref[sl, :]], axis=0),
                      jnp.concatenate([pad, v_ref[sl, :]], axis=0),
                      jnp.where(left, NEG_BIG, bt_ref[br, 0]),
                      jnp.where(left, NEG_BIG, bt_ref[br, 1]), is_a)
        for ref, val in zip((num_ref, m_ref, den_ref), out):
            ref[br, sl, :] = val

    def later_block(br, d, r, n):
        qs = rows(r + n * (blk * d), blk, d)
        ks = rows(r + (n - 1) * (blk * d), 2 * blk, d)
        out = _attend(q_ref[qs, :], k_ref[ks, :], v_ref[ks, :],
                      bt_ref[br, 0], bt_ref[br, 1], is_a)
        for ref, val in zip((num_ref, m_ref, den_ref), out):
            ref[br, qs, :] = val

    for br, (_, d) in enumerate(DILATED_GROUPS):
        nb = T // (d * blk)
        def firsts(r, carry, br=br, d=d):
            first_block(br, d, r)
            return carry

        def laters(i, carry, br=br, d=d, nb=nb):
            r = lax.div(i, nb - 1)
            later_block(br, d, r, 1 + lax.rem(i, nb - 1))
            return carry

        if d == 1:
            first_block(br, d, 0)
        else:
            lax.fori_loop(0, d, firsts, 0, unroll=ATT_UNROLL)
        if nb > 1:
            lax.fori_loop(0, d * (nb - 1), laters, 0, unroll=ATT_UNROLL)

    def merge(c, carry):
        sl = pl.ds(pl.multiple_of(c * blk, blk), blk)
        m0, m1, m2 = m_ref[0, sl, :], m_ref[1, sl, :], m_ref[2, sl, :]
        mm = jnp.maximum(jnp.maximum(m0, m1), m2)
        w0, w1, w2 = jnp.exp(m0 - mm), jnp.exp(m1 - mm), jnp.exp(m2 - mm)
        num = w0 * num_ref[0, sl, :] + w1 * num_ref[1, sl, :] + w2 * num_ref[2, sl, :]
        den = w0 * den_ref[0, sl, :] + w1 * den_ref[1, sl, :] + w2 * den_ref[2, sl, :]
        o_ref[sl, :] = num / den
        return carry

    lax.fori_loop(0, T // blk, merge, 0)


def _attention(q, k, v, btab, batch, seq):
    npairs = D_ATTN // LANES
    qkv = lambda: pl.BlockSpec((seq, LANES), lambda b, p: (b, p))
    return pl.pallas_call(
        _attn_kernel,
        grid=(batch, npairs),
        in_specs=[qkv(), qkv(), qkv(),
                  pl.BlockSpec((3, 2, ATT_BLOCK, 2 * ATT_BLOCK), lambda b, p: (0, p, 0, 0))],
        out_specs=pl.BlockSpec((seq, LANES), lambda b, p: (b, p)),
        out_shape=jax.ShapeDtypeStruct((batch * seq, D_ATTN), F32),
        scratch_shapes=[pltpu.VMEM((3, seq, LANES), F32) for _ in range(3)],
        compiler_params=_cparams(("parallel", "parallel")),
        name="dilated_attn",
    )(q, k, v, btab)


def _bias_tables(rel_bias):
    tabs = []
    for window, d in DILATED_GROUPS:
        w_sub = window // d
        qi = jnp.arange(ATT_BLOCK)[:, None]
        kj = jnp.arange(2 * ATT_BLOCK)[None, :]
        dist = qi + ATT_BLOCK - kj
        max_exact = REL_BUCKETS // 2
        nn = jnp.maximum(dist * d, 0)
        nf = jnp.maximum(nn, 1).astype(F32)
        large = max_exact + (jnp.log(nf / max_exact) / math.log(REL_MAX_DIST / max_exact)
                             * (REL_BUCKETS - max_exact)).astype(I32)
        large = jnp.minimum(large, REL_BUCKETS - 1)
        bucket = jnp.where(nn < max_exact, nn, large)
        rb = rel_bias.astype(F32)
        bias = jnp.zeros((N_ATT_HEADS,) + bucket.shape, F32)
        for bkt in range(REL_BUCKETS):
            bias = jnp.where(bucket[None] == bkt, rb[bkt][:, None, None], bias)
        ok = (dist >= 0) & (dist <= w_sub)
        tabs.append(jnp.where(ok[None], bias, NEG_BIG))
    return jnp.stack(tabs, axis=0)


def _outproj_kernel(yl_ref, ya_ref, x_ref, ga_ref, w_ref, gf_ref, h_ref, xn_ref):
    ya = _rms(ya_ref[...], ga_ref[...]).astype(BF16)
    acc = jnp.dot(yl_ref[...], w_ref[0:D_LRU, :], preferred_element_type=F32)
    acc = acc + jnp.dot(ya, w_ref[D_LRU:, :], preferred_element_type=F32)
    h = x_ref[...] + acc
    h_ref[...] = h
    xn_ref[...] = _rms(h, gf_ref[...]).astype(BF16)


def _outproj(y_lru, y_att, x2, g_att, w_out_b, g_ffn, tm=1024):
    n = x2.shape[0]
    return pl.pallas_call(
        _outproj_kernel,
        grid=(n // tm,),
        in_specs=[pl.BlockSpec((tm, D_LRU), lambda i: (i, 0)),
                  pl.BlockSpec((tm, D_ATTN), lambda i: (i, 0)),
                  pl.BlockSpec((tm, D_MODEL), lambda i: (i, 0)),
                  pl.BlockSpec((1, D_ATTN), lambda i: (0, 0)),
                  pl.BlockSpec((D_MODEL, D_MODEL), lambda i: (0, 0)),
                  pl.BlockSpec((1, D_MODEL), lambda i: (0, 0))],
        out_specs=[pl.BlockSpec((tm, D_MODEL), lambda i: (i, 0)),
                   pl.BlockSpec((tm, D_MODEL), lambda i: (i, 0))],
        out_shape=[jax.ShapeDtypeStruct((n, D_MODEL), F32),
                   jax.ShapeDtypeStruct((n, D_MODEL), BF16)],
        compiler_params=_cparams(("parallel",)),
        name="outproj",
    )(y_lru, y_att, x2, g_att, w_out_b, g_ffn)


def _tree(op, xs):
    xs = list(xs)
    while len(xs) > 1:
        xs = [op(xs[i], xs[i + 1]) for i in range(0, len(xs) - 1, 2)] + (xs[-1:] if len(xs) % 2 else [])
    return xs[0]


RK_TM = 1024
RK_G = RK_TM // LANES
RK_NACC = 4
_CAND_CELLS = tuple((i, j) for i in range(PEER_TOPK) for j in range(PEER_TOPK)
                    if (i + 1) * (j + 1) <= PEER_TOPK)


def _extract_rows(ref, chain_ids, ids, big, nrounds, emit):
    n = len(ids)
    tile = lambda r: slice(r * RK_G, (r + 1) * RK_G)

    def fold(op, terms):
        accs = []
        for r, t in enumerate(terms):
            if r < RK_NACC:
                accs.append(t)
            else:
                accs[r % RK_NACC] = op(accs[r % RK_NACC], t)
        return _tree(op, accs)

    def rnd(k, carry):
        for c in chain_ids:
            m = fold(jnp.maximum, (ref[c, tile(r), :] for r in range(n)))
            sel = fold(jnp.minimum,
                       (jnp.where(ref[c, tile(r), :] == m, ids[r], big) for r in range(n)))
            emit(c, k, m, sel)
            for r in range(n):
                ref[c, tile(r), :] = jnp.where(sel == ids[r], NEG_BIG, ref[c, tile(r), :])
        return carry

    lax.fori_loop(0, nrounds, rnd, 0)


def _route_kernel(xn_ref, wqt_ref, k1_ref, k2_ref, a_ref, b_ref, w_ref,
                  qt_ref, s_ref, tv_ref, ti_ref, cand_ref, cv_ref, cp_ref, oa_ref, ob_ref, ow_ref):
    half = PEER_QDIM // 2
    dn = (((1,), (1,)), ((), ()))
    qt_ref[...] = lax.dot_general(wqt_ref[...], xn_ref[...], dn, preferred_element_type=F32)
    key_ids = [float(r) for r in range(PEER_NKEYS)]
    cell_ids = [float(i * PEER_TOPK + j) for i, j in _CAND_CELLS]

    def head(h, carry):
        for c, k_ref in enumerate((k1_ref, k2_ref)):
            row0 = pl.multiple_of(h * PEER_QDIM + c * half, half)
            s = jnp.dot(k_ref[...], qt_ref[pl.ds(row0, half), :].astype(BF16),
                        preferred_element_type=F32)
            for g in range(RK_G):
                s_ref[c, pl.ds(g, PEER_NKEYS, stride=RK_G), :] = s[:, g * LANES:(g + 1) * LANES]

        def emit1(c, k, m, sel):
            tv_ref[c, k] = m
            ti_ref[c, k] = sel

        _extract_rows(s_ref, (0, 1), key_ids, float(PEER_NKEYS), PEER_TOPK, emit1)

        for n, (i, j) in enumerate(_CAND_CELLS):
            cand_ref[0, n * RK_G:(n + 1) * RK_G, :] = tv_ref[0, i] + tv_ref[1, j]

        def emit2(c, k, m, sel):
            cv_ref[k] = m
            cp_ref[k] = sel

        _extract_rows(cand_ref, (0,), cell_ids, float(PEER_TOPK * PEER_TOPK), PEER_TOPK, emit2)

        top = [cv_ref[k] for k in range(PEER_TOPK)]
        es = [jnp.exp(t - top[0]) for t in top]
        inv = 1.0 / _tree(jnp.add, es)
        for k in range(PEER_TOPK):
            pos = cp_ref[k].astype(I32)
            ri = lax.shift_right_logical(pos, 4)
            rj = lax.bitwise_and(pos, PEER_TOPK - 1)
            a_sel = jnp.zeros((RK_G, LANES), F32)
            b_sel = jnp.zeros((RK_G, LANES), F32)
            for r in range(PEER_TOPK):
                a_sel = jnp.where(ri == r, ti_ref[0, r], a_sel)
                b_sel = jnp.where(rj == r, ti_ref[1, r], b_sel)
            rows = pl.ds(pl.multiple_of((h * PEER_TOPK + k) * RK_G, RK_G), RK_G)
            oa_ref[rows, :] = a_sel
            ob_ref[rows, :] = b_sel
            ow_ref[rows, :] = es[k] * inv
        return carry

    lax.fori_loop(0, PEER_HEADS, head, 0)
    nslot = PEER_HEADS * PEER_TOPK
    for g in range(RK_G):
        rows = pl.ds(g, nslot, stride=RK_G)
        tok = slice(g * LANES, (g + 1) * LANES)
        a_ref[tok, :] = oa_ref[rows, :].T.astype(I32)
        b_ref[tok, :] = ob_ref[rows, :].T.astype(I32)
        w_ref[tok, :] = ow_ref[rows, :].T


def _route(xn, wqt_b, k1_b, k2_b):
    n = xn.shape[0]
    tm = RK_TM
    nslot = PEER_HEADS * PEER_TOPK
    out = lambda: pl.BlockSpec((tm, nslot), lambda i: (i, 0))
    vregs = lambda *lead: pltpu.VMEM(lead + (RK_G, LANES), F32)
    tiles = lambda *lead: pltpu.VMEM(lead[:-1] + (lead[-1] * RK_G, LANES), F32)
    return pl.pallas_call(
        _route_kernel,
        grid=(n // tm,),
        in_specs=[pl.BlockSpec((tm, D_MODEL), lambda i: (i, 0)),
                  pl.BlockSpec((PEER_HEADS * PEER_QDIM, D_MODEL), lambda i: (0, 0)),
                  pl.BlockSpec((PEER_NKEYS, PEER_QDIM // 2), lambda i: (0, 0)),
                  pl.BlockSpec((PEER_NKEYS, PEER_QDIM // 2), lambda i: (0, 0))],
        out_specs=[out(), out(), out()],
        out_shape=[jax.ShapeDtypeStruct((n, nslot), I32),
                   jax.ShapeDtypeStruct((n, nslot), I32),
                   jax.ShapeDtypeStruct((n, nslot), F32)],
        scratch_shapes=[pltpu.VMEM((PEER_HEADS * PEER_QDIM, tm), F32),
                        tiles(2, PEER_NKEYS), vregs(2, PEER_TOPK), vregs(2, PEER_TOPK),
                        tiles(1, len(_CAND_CELLS)), vregs(PEER_TOPK), vregs(PEER_TOPK),
                        tiles(nslot), tiles(nslot), tiles(nslot)],
        compiler_params=_cparams(("parallel",)),
        name="peer_route",
    )(xn, wqt_b, k1_b, k2_b)


EXP_CHUNK = 2048
EXP_NCHUNK = PEER_NKEYS * PEER_NKEYS // EXP_CHUNK
EXPV_CHUNK = 4096


def _expu_kernel(xn_ref, u_ref, a_ref, b_ref, z_ref, buf0_ref, buf1_ref):
    tm = xn_ref.shape[0]
    i = pl.program_id(0)
    j = pl.program_id(1)
    nt = pl.num_programs(0) - 1
    assert EXP_NCHUNK % 2 == 0
    groups_per_chunk = EXP_CHUNK // PEER_NKEYS

    def matmul_into(buf_ref):
        buf_ref[...] = lax.dot_general(xn_ref[...], u_ref[...], (((1,), (1,)), ((), ())),
                                       preferred_element_type=F32)

    def gather_from(buf_ref):
        a0 = lax.rem(j + EXP_NCHUNK - 1, EXP_NCHUNK) * groups_per_chunk
        for g in range(tm // SUBLANES):
            rs = slice(g * SUBLANES, (g + 1) * SUBLANES)
            bi = b_ref[rs, :]
            ai = a_ref[rs, :] - a0
            acc = z_ref[rs, :]
            for t in range(groups_per_chunk):
                got = jnp.take_along_axis(buf_ref[rs, t * PEER_NKEYS:(t + 1) * PEER_NKEYS], bi,
                                          axis=1, mode="promise_in_bounds")
                acc = jnp.where(ai == t, got, acc)
            z_ref[rs, :] = acc

    odd = lax.rem(j, 2) == 1
    real = i < nt
    first = jnp.logical_and(i == 0, j == 0)

    @pl.when(first)
    def _():
        matmul_into(buf0_ref)

    @pl.when(jnp.logical_and(real, j == 1))
    def _():
        z_ref[...] = jnp.zeros_like(z_ref)

    @pl.when(jnp.logical_and(real, odd))
    def _():
        gather_from(buf0_ref)
        matmul_into(buf1_ref)

    @pl.when(jnp.logical_and(jnp.logical_and(real, jnp.logical_not(odd)), jnp.logical_not(first)))
    def _():
        gather_from(buf1_ref)
        matmul_into(buf0_ref)

    @pl.when(jnp.logical_and(i == nt, j == 0))
    def _():
        gather_from(buf1_ref)


def _expert_u(xn, u_b, aidx, bidx, tm=1024):
    n = xn.shape[0]
    nexp = u_b.shape[0]
    assert nexp == EXP_NCHUNK * EXP_CHUNK
    nslot = PEER_HEADS * PEER_TOPK
    nt = n // tm
    lag = lambda i, j: (jnp.clip(jnp.where(j == 0, i - 1, i), 0, nt - 1), 0)
    return pl.pallas_call(
        _expu_kernel,
        grid=(nt + 1, EXP_NCHUNK),
        in_specs=[pl.BlockSpec((tm, D_MODEL), lambda i, j: (jnp.minimum(i, nt - 1), 0)),
                  pl.BlockSpec((EXP_CHUNK, D_MODEL), lambda i, j: (jnp.where(i == nt, 0, j), 0)),
                  pl.BlockSpec((tm, nslot), lag),
                  pl.BlockSpec((tm, nslot), lag)],
        out_specs=pl.BlockSpec((tm, nslot), lag),
        out_shape=jax.ShapeDtypeStruct((n, nslot), F32),
        scratch_shapes=[pltpu.VMEM((tm, EXP_CHUNK), F32), pltpu.VMEM((tm, EXP_CHUNK), F32)],
        compiler_params=_cparams(("arbitrary", "arbitrary")),
        name="peer_expert_u",
    )(xn, u_b, aidx, bidx)


SLAB_PAD = SUBLANES
SLAB_TOKENS = 256
U32 = jnp.uint32


def _bf16_bits(x):
    return pltpu.bitcast(x.astype(BF16).astype(F32), U32)


def _expv_kernel(z_ref, w_ref, a_ref, b_ref, v_ref, h_ref, o_ref, s_ref, acc_ref, c_ref):
    tm = z_ref.shape[0]
    nk = PEER_NKEYS
    pitch = tm // 2 + SLAB_PAD
    j = pl.program_id(1)

    @pl.when(j == 0)
    def _():
        c_ref[...] = w_ref[...] * _gelu(z_ref[...])
        sub = lax.broadcasted_iota(I32, (nk, nk), 0)
        dn = (((1,), (1,)), ((), ()))

        def slab(n):
            arow = a_ref[pl.ds(n, 1), :]
            brow = b_ref[pl.ds(n, 1), :]
            crow = c_ref[pl.ds(n, 1), :]
            oat = jnp.where(arow == sub, 1.0, 0.0).astype(BF16)
            wobt = jnp.where(brow == sub, crow, 0.0).astype(BF16)
            return lax.dot_general(oat, wobt, dn, preferred_element_type=F32)

        def toks(g, carry):
            base = g * (SLAB_TOKENS // 2)
            for t in range(SLAB_TOKENS // 2):
                row = base + t
                lo = lax.shift_right_logical(_bf16_bits(slab(2 * row)), jnp.uint32(16))
                hi = _bf16_bits(slab(2 * row + 1))
                s_ref[pl.ds(row, nk, stride=pitch), :] = lo | hi
            return carry

        lax.fori_loop(0, tm // SLAB_TOKENS, toks, 0)

    groups = EXPV_CHUNK // nk
    slabs = []
    for t in range(groups):
        r0 = pl.multiple_of((j * groups + t) * pitch, SUBLANES)
        slabs.append(pltpu.bitcast(s_ref[pl.ds(r0, tm // 2), :], BF16))
    part = jnp.dot(jnp.concatenate(slabs, axis=1), v_ref[...], preferred_element_type=F32)

    @pl.when(j == 0)
    def _():
        acc_ref[...] = part

    @pl.when(j > 0)
    def _():
        acc_ref[...] += part

    @pl.when(j == pl.num_programs(1) - 1)
    def _():
        o_ref[...] = h_ref[...] + acc_ref[...]


def _expert_v(zsel, w, aidx, bidx, v_b, h1, tm=512):
    n = zsel.shape[0]
    nexp = v_b.shape[0]
    nslot = PEER_HEADS * PEER_TOPK
    slot = lambda: pl.BlockSpec((tm, nslot), lambda i, j: (i, 0))
    return pl.pallas_call(
        _expv_kernel,
        grid=(n // tm, nexp // EXPV_CHUNK),
        in_specs=[slot(), slot(), slot(), slot(),
                  pl.BlockSpec((EXPV_CHUNK, D_MODEL), lambda i, j: (j, 0)),
                  pl.BlockSpec((tm, D_MODEL), lambda i, j: (i, 0))],
        out_specs=pl.BlockSpec((tm, D_MODEL), lambda i, j: (i, 0)),
        out_shape=jax.ShapeDtypeStruct((n, D_MODEL), F32),
        scratch_shapes=[pltpu.VMEM((PEER_NKEYS * (tm // 2 + SLAB_PAD), PEER_NKEYS), U32),
                        pltpu.VMEM((tm, D_MODEL), F32),
                        pltpu.VMEM((tm, nslot), F32)],
        compiler_params=_cparams(("parallel", "arbitrary")),
        name="peer_expert_v",
    )(zsel, w, aidx, bidx, v_b, h1)


def _ple_kernel(h_ref, p_ref, gp_ref, wg_ref, wp_ref, gf_ref, o_ref):
    h = h_ref[...]
    xn = _rms(h, gp_ref[...]).astype(BF16)
    gate = jax.nn.sigmoid(jnp.dot(xn, wg_ref[...], preferred_element_type=F32))
    proj = jnp.dot(p_ref[...].astype(BF16), wp_ref[...], preferred_element_type=F32)
    o_ref[...] = _rms(h + gate * proj, gf_ref[...])


def _ple(h2, p2, g_ple, wg_b, wp_b, g_final, tm=1024):
    n = h2.shape[0]
    return pl.pallas_call(
        _ple_kernel,
        grid=(n // tm,),
        in_specs=[pl.BlockSpec((tm, D_MODEL), lambda i: (i, 0)),
                  pl.BlockSpec((tm, PLE_DIM), lambda i: (i, 0)),
                  pl.BlockSpec((1, D_MODEL), lambda i: (0, 0)),
                  pl.BlockSpec((D_MODEL, D_MODEL), lambda i: (0, 0)),
                  pl.BlockSpec((PLE_DIM, D_MODEL), lambda i: (0, 0)),
                  pl.BlockSpec((1, D_MODEL), lambda i: (0, 0))],
        out_specs=pl.BlockSpec((tm, D_MODEL), lambda i: (i, 0)),
        out_shape=jax.ShapeDtypeStruct((n, D_MODEL), F32),
        compiler_params=_cparams(("parallel",)),
        name="ple_final",
    )(h2, p2, g_ple, wg_b, wp_b, g_final)


def _block_diag(w):
    g, d, _ = w.shape
    eye = jnp.eye(g, dtype=w.dtype)
    return (eye[:, None, :, None] * w[:, :, None, :]).reshape(g * d, g * d)


def kernel(x, p, g_mix, w_in, conv_w, conv_b, w_rg, b_rg, w_ig, b_ig, lru_lambda, g_lru_out,
           g_attn_out, w_out, rel_bias, g_ffn, peer_wq, peer_k1, peer_k2, peer_u, peer_v,
           g_ple, w_ple_gate, w_ple_proj, g_final):
    batch, seq, dm = x.shape
    n = batch * seq
    assert w_in.shape[0] == 1, "single-layer problem: the last kernel fuses the final norm"
    row = lambda v: v.reshape(1, -1).astype(F32)
    btab = _bias_tables(rel_bias)
    h = x.reshape(n, dm)
    for i in range(1):
        xl, gate, q, k, v = _inproj(h, row(g_mix[i]), w_in[i].astype(BF16))
        y_lru = _lru(xl, gate, conv_w[i], row(conv_b[i]),
                     _block_diag(w_rg[i]).astype(BF16), row(b_rg[i]),
                     _block_diag(w_ig[i]).astype(BF16), row(b_ig[i]),
                     row(lru_lambda[i]), row(g_lru_out[i]), batch, seq)
        y_att = _attention(q, k, v, btab, batch, seq)
        h1, xn = _outproj(y_lru, y_att, h, row(g_attn_out[i]), w_out[i].astype(BF16),
                          row(g_ffn[i]))
        aidx, bidx, gates = _route(xn, peer_wq[i].T.astype(BF16), peer_k1[i].astype(BF16),
                                   peer_k2[i].astype(BF16))
        zsel = _expert_u(xn, peer_u[i].astype(BF16), aidx, bidx)
        h2 = _expert_v(zsel, gates, aidx, bidx, peer_v[i].astype(BF16), h1)
        h = _ple(h2, p[i].reshape(n, -1), row(g_ple[i]), w_ple_gate[i].astype(BF16),
                 w_ple_proj[i].astype(BF16), g_final.reshape(1, -1))
    return h.reshape(batch, seq, dm)
```

```python
import functools
import math

import jax
import jax.numpy as jnp
import numpy as np
from jax import lax
from jax.experimental import pallas as pl
from jax.experimental.pallas import tpu as pltpu

F32 = jnp.float32
BF16 = jnp.bfloat16
I32 = jnp.int32

D_MODEL = 1024
PLE_DIM = 256
D_LRU = 512
LRU_BLOCKS = 8
CONV_WIDTH = 4
LRU_C = 8.0
D_ATTN = 512
N_ATT_HEADS = 8
HEAD_DIM = 64
DILATED_GROUPS = ((128, 1), (512, 4), (2048, 16))
ATT_BLOCK = 128
REL_BUCKETS = 32
REL_MAX_DIST = 2048
PEER_HEADS = 8
PEER_NKEYS = 128
PEER_QDIM = 256
PEER_TOPK = 16
NORM_EPS = 1e-6

LANES = 128
SUBLANES = 8
NEG_BIG = -1e30
VMEM_LIMIT = 56 * 1024 * 1024


def _cparams(sem, flags=None):
    return pltpu.CompilerParams(dimension_semantics=sem, vmem_limit_bytes=VMEM_LIMIT, flags=flags)


def _rms(x, g):
    ms = jnp.mean(x * x, axis=-1, keepdims=True)
    return (x * lax.rsqrt(ms + NORM_EPS)) * g


def _gelu(x):
    c = math.sqrt(2.0 / math.pi)
    return x * (0.5 * (1.0 + jnp.tanh(c * (x + 0.044715 * (x * x * x)))))


def _inproj_kernel(x_ref, g_ref, w_ref, xl_ref, gate_ref, q_ref, k_ref, v_ref):
    u = _rms(x_ref[...], g_ref[...]).astype(BF16)
    outs = (xl_ref, gate_ref, q_ref, k_ref, v_ref)
    for i, o_ref in enumerate(outs):
        z = jnp.dot(u, w_ref[:, i * 512:(i + 1) * 512], preferred_element_type=F32)
        if o_ref is q_ref:
            z = z * (HEAD_DIM ** -0.5)
        o_ref[...] = z


def _inproj(x2, g_mix, w_in_b, tm=1024):
    n = x2.shape[0]
    blk = lambda: pl.BlockSpec((tm, 512), lambda i: (i, 0))
    return pl.pallas_call(
        _inproj_kernel,
        grid=(n // tm,),
        in_specs=[pl.BlockSpec((tm, D_MODEL), lambda i: (i, 0)),
                  pl.BlockSpec((1, D_MODEL), lambda i: (0, 0)),
                  pl.BlockSpec((D_MODEL, 2560), lambda i: (0, 0))],
        out_specs=[blk() for _ in range(5)],
        out_shape=[jax.ShapeDtypeStruct((n, 512), F32) for _ in range(5)],
        compiler_params=_cparams(("parallel",)),
        name="inproj",
    )(x2, g_mix, w_in_b)


LRU_CHUNK = 64
LRU_UNROLL = 4


def _lru_kernel(xl_ref, gate_ref, cw_ref, cb_ref, wrg_ref, brg_ref, wig_ref, big_ref,
                lam_ref, g_ref, o_ref, xp_ref, h_ref):
    T = xl_ref.shape[0]
    tc = LRU_CHUNK
    xp_ref[0:SUBLANES, :] = jnp.zeros((SUBLANES, D_LRU), F32)
    xp_ref[SUBLANES:SUBLANES + T, :] = xl_ref[...]
    h_ref[...] = jnp.zeros_like(h_ref)
    nlam = -lam_ref[...]
    sp = jnp.maximum(nlam, 0.0) + jnp.log1p(jnp.exp(-jnp.abs(nlam)))
    sub_row = lax.broadcasted_iota(I32, (tc // SUBLANES, SUBLANES, D_LRU), 1)

    def chunk(c, carry):
        t0 = pl.multiple_of(c * tc, tc)
        xa = xp_ref[pl.ds(t0, tc + SUBLANES), :]
        xc = cb_ref[...] + cw_ref[CONV_WIDTH - 1:CONV_WIDTH, :] * xa[SUBLANES:, :]
        for j in range(CONV_WIDTH - 1):
            sh = pltpu.roll(xa, CONV_WIDTH - 1 - j, axis=0)[SUBLANES:, :]
            xc = xc + cw_ref[j:j + 1, :] * sh
        xcb = xc.astype(BF16)
        r = jax.nn.sigmoid(jnp.dot(xcb, wrg_ref[...], preferred_element_type=F32) + brg_ref[...])
        ig = jax.nn.sigmoid(jnp.dot(xcb, wig_ref[...], preferred_element_type=F32) + big_ref[...])
        log_a = (-LRU_C) * r * sp
        a = jnp.exp(log_a)
        om = -jnp.tanh(log_a) * (a * a + 1.0)
        root = jnp.where(om > 0.0, om * lax.rsqrt(om), 0.0)
        b = root * (ig * xc)
        a = a.reshape(tc // SUBLANES, SUBLANES, D_LRU)
        b = b.reshape(tc // SUBLANES, SUBLANES, D_LRU)
        d = 1
        while d < SUBLANES:
            keep = sub_row >= d
            a_sh = jnp.where(keep, pltpu.roll(a, d, axis=1), 1.0)
            b_sh = jnp.where(keep, pltpu.roll(b, d, axis=1), 0.0)
            b = a * b_sh + b
            a = a * a_sh
            d *= 2
        last = h_ref[...]
        hs = []
        for g in range(tc // SUBLANES):
            hg = b[g] + a[g] * last
            hs.append(hg)
            last = hg[SUBLANES - 1:SUBLANES, :]
        h = jnp.concatenate(hs, axis=0)
        h_ref[...] = last
        y = h * _gelu(gate_ref[pl.ds(t0, tc), :])
        o_ref[pl.ds(t0, tc), :] = _rms(y, g_ref[...]).astype(BF16)
        return carry

    lax.fori_loop(0, T // tc, chunk, 0, unroll=LRU_UNROLL)


def _lru(xl, gate, conv_w, conv_b, wrg_bd, b_rg, wig_bd, b_ig, lam, g_lru, batch, seq):
    vec = lambda: pl.BlockSpec((1, D_LRU), lambda b: (0, 0))
    mat = lambda: pl.BlockSpec((D_LRU, D_LRU), lambda b: (0, 0))
    return pl.pallas_call(
        _lru_kernel,
        grid=(batch,),
        in_specs=[pl.BlockSpec((seq, D_LRU), lambda b: (b, 0)),
                  pl.BlockSpec((seq, D_LRU), lambda b: (b, 0)),
                  pl.BlockSpec((CONV_WIDTH, D_LRU), lambda b: (0, 0)),
                  vec(), mat(), vec(), mat(), vec(), vec(), vec()],
        out_specs=pl.BlockSpec((seq, D_LRU), lambda b: (b, 0)),
        out_shape=jax.ShapeDtypeStruct((batch * seq, D_LRU), BF16),
        scratch_shapes=[pltpu.VMEM((seq + SUBLANES, D_LRU), F32),
                        pltpu.VMEM((1, D_LRU), F32)],
        compiler_params=_cparams(("parallel",)),
        name="rglru",
    )(xl, gate, conv_w, conv_b, wrg_bd, b_rg, wig_bd, b_ig, lam, g_lru)


ATT_UNROLL = 8


def _attend(qf, kf, vf, bias_a, bias_b, is_a):
    kb = kf.astype(BF16)
    vb = vf.astype(BF16)
    dn = (((1,), (1,)), ((), ()))
    outs = []
    for q_h, bias in ((jnp.where(is_a, qf, 0.0), bias_a), (jnp.where(is_a, 0.0, qf), bias_b)):
        s = lax.dot_general(q_h.astype(BF16), kb, dn, preferred_element_type=F32) + bias
        m = jnp.max(s, axis=-1, keepdims=True)
        e = jnp.exp(s - m)
        den = jnp.sum(e, axis=-1, keepdims=True)
        num = jnp.dot(e.astype(BF16), vb, preferred_element_type=F32)
        outs.append((num, m, den))
    (na, ma, da), (nb, mb, db) = outs
    shape = na.shape
    return (jnp.where(is_a, na, nb),
            jnp.where(is_a, jnp.broadcast_to(ma, shape), jnp.broadcast_to(mb, shape)),
            jnp.where(is_a, jnp.broadcast_to(da, shape), jnp.broadcast_to(db, shape)))


def _attn_kernel(q_ref, k_ref, v_ref, bt_ref, o_ref, num_ref, m_ref, den_ref):
    T = q_ref.shape[0]
    blk = ATT_BLOCK
    is_a = lax.broadcasted_iota(I32, (blk, LANES), 1) < HEAD_DIM

    def rows(start, size, d):
        return pl.ds(start, size) if d == 1 else pl.ds(start, size, stride=d)

    def first_block(br, d, r):
        sl = rows(r, blk, d)
        pad = jnp.zeros((blk, LANES), F32)
        left = lax.broadcasted_iota(I32, (blk, 2 * blk), 1) < blk
        out = _attend(q_ref[sl, :], jnp.concatenate([pad, k_ref[sl, :]], axis=0),
                      jnp.concatenate([pad, v_ref[sl, :]], axis=0),
                      jnp.where(left, NEG_BIG, bt_ref[br, 0]),
                      jnp.where(left, NEG_BIG, bt_ref[br, 1]), is_a)
        for ref, val in zip((num_ref, m_ref, den_ref), out):
            ref[br, sl, :] = val

    def later_block(br, d, r, n):
        qs = rows(r + n * (blk * d), blk, d)
        ks = rows(r + (n - 1) * (blk * d), 2 * blk, d)
        out = _attend(q_ref[qs, :], k_ref[ks, :], v_ref[ks, :],
                      bt_ref[br, 0], bt_ref[br, 1], is_a)
        for ref, val in zip((num_ref, m_ref, den_ref), out):
            ref[br, qs, :] = val

    for br, (_, d) in enumerate(DILATED_GROUPS):
        nb = T // (d * blk)
        def firsts(r, carry, br=br, d=d):
            first_block(br, d, r)
            return carry

        def laters(i, carry, br=br, d=d, nb=nb):
            r = lax.div(i, nb - 1)
            later_block(br, d, r, 1 + lax.rem(i, nb - 1))
            return carry

        if d == 1:
            first_block(br, d, 0)
        else:
            lax.fori_loop(0, d, firsts, 0, unroll=ATT_UNROLL)
        if nb > 1:
            lax.fori_loop(0, d * (nb - 1), laters, 0, unroll=ATT_UNROLL)

    def merge(c, carry):
        sl = pl.ds(pl.multiple_of(c * blk, blk), blk)
        m0, m1, m2 = m_ref[0, sl, :], m_ref[1, sl, :], m_ref[2, sl, :]
        mm = jnp.maximum(jnp.maximum(m0, m1), m2)
        w0, w1, w2 = jnp.exp(m0 - mm), jnp.exp(m1 - mm), jnp.exp(m2 - mm)
        num = w0 * num_ref[0, sl, :] + w1 * num_ref[1, sl, :] + w2 * num_ref[2, sl, :]
        den = w0 * den_ref[0, sl, :] + w1 * den_ref[1, sl, :] + w2 * den_ref[2, sl, :]
        o_ref[sl, :] = num / den
        return carry

    lax.fori_loop(0, T // blk, merge, 0)


def _attention(q, k, v, btab, batch, seq):
    npairs = D_ATTN // LANES
    qkv = lambda: pl.BlockSpec((seq, LANES), lambda b, p: (b, p))
    return pl.pallas_call(
        _attn_kernel,
        grid=(batch, npairs),
        in_specs=[qkv(), qkv(), qkv(),
                  pl.BlockSpec((3, 2, ATT_BLOCK, 2 * ATT_BLOCK), lambda b, p: (0, p, 0, 0))],
        out_specs=pl.BlockSpec((seq, LANES), lambda b, p: (b, p)),
        out_shape=jax.ShapeDtypeStruct((batch * seq, D_ATTN), F32),
        scratch_shapes=[pltpu.VMEM((3, seq, LANES), F32) for _ in range(3)],
        compiler_params=_cparams(("parallel", "parallel")),
        name="dilated_attn",
    )(q, k, v, btab)


def _bias_tables(rel_bias):
    tabs = []
    for window, d in DILATED_GROUPS:
        w_sub = window // d
        qi = jnp.arange(ATT_BLOCK)[:, None]
        kj = jnp.arange(2 * ATT_BLOCK)[None, :]
        dist = qi + ATT_BLOCK - kj
        max_exact = REL_BUCKETS // 2
        nn = jnp.maximum(dist * d, 0)
        nf = jnp.maximum(nn, 1).astype(F32)
        large = max_exact + (jnp.log(nf / max_exact) / math.log(REL_MAX_DIST / max_exact)
                             * (REL_BUCKETS - max_exact)).astype(I32)
        large = jnp.minimum(large, REL_BUCKETS - 1)
        bucket = jnp.where(nn < max_exact, nn, large)
        rb = rel_bias.astype(F32)
        bias = jnp.zeros((N_ATT_HEADS,) + bucket.shape, F32)
        for bkt in range(REL_BUCKETS):
            bias = jnp.where(bucket[None] == bkt, rb[bkt][:, None, None], bias)
        ok = (dist >= 0) & (dist <= w_sub)
        tabs.append(jnp.where(ok[None], bias, NEG_BIG))
    return jnp.stack(tabs, axis=0)


def _outproj_kernel(yl_ref, ya_ref, x_ref, ga_ref, w_ref, gf_ref, h_ref, xn_ref):
    ya = _rms(ya_ref[...], ga_ref[...]).astype(BF16)
    acc = jnp.dot(yl_ref[...], w_ref[0:D_LRU, :], preferred_element_type=F32)
    acc = acc + jnp.dot(ya, w_ref[D_LRU:, :], preferred_element_type=F32)
    h = x_ref[...] + acc
    h_ref[...] = h
    xn_ref[...] = _rms(h, gf_ref[...]).astype(BF16)


def _outproj(y_lru, y_att, x2, g_att, w_out_b, g_ffn, tm=1024):
    n = x2.shape[0]
    return pl.pallas_call(
        _outproj_kernel,
        grid=(n // tm,),
        in_specs=[pl.BlockSpec((tm, D_LRU), lambda i: (i, 0)),
                  pl.BlockSpec((tm, D_ATTN), lambda i: (i, 0)),
                  pl.BlockSpec((tm, D_MODEL), lambda i: (i, 0)),
                  pl.BlockSpec((1, D_ATTN), lambda i: (0, 0)),
                  pl.BlockSpec((D_MODEL, D_MODEL), lambda i: (0, 0)),
                  pl.BlockSpec((1, D_MODEL), lambda i: (0, 0))],
        out_specs=[pl.BlockSpec((tm, D_MODEL), lambda i: (i, 0)),
                   pl.BlockSpec((tm, D_MODEL), lambda i: (i, 0))],
        out_shape=[jax.ShapeDtypeStruct((n, D_MODEL), F32),
                   jax.ShapeDtypeStruct((n, D_MODEL), BF16)],
        compiler_params=_cparams(("parallel",)),
        name="outproj",
    )(y_lru, y_att, x2, g_att, w_out_b, g_ffn)


def _tree(op, xs):
    xs = list(xs)
    while len(xs) > 1:
        xs = [op(xs[i], xs[i + 1]) for i in range(0, len(xs) - 1, 2)] + (xs[-1:] if len(xs) % 2 else [])
    return xs[0]


RK_TM = 1024
RK_G = RK_TM // LANES
_CAND_CELLS = tuple((i, j) for i in range(PEER_TOPK) for j in range(PEER_TOPK)
                    if (i + 1) * (j + 1) <= PEER_TOPK)


def _oddeven_pairs(n):
    pairs = []
    p = 1
    while p < n:
        k = p
        while k >= 1:
            for j in range(k % p, n - k, 2 * k):
                for i in range(min(k, n - j - k)):
                    if (i + j) // (2 * p) == (i + j + k) // (2 * p):
                        pairs.append((i + j, i + j + k))
            k //= 2
        p *= 2
    return pairs


_SORT16 = _oddeven_pairs(PEER_TOPK)
_BITONIC16 = tuple((i, i + s) for s in (8, 4, 2, 1) for i in range(PEER_TOPK) if not i & s)


def _before(va, ia, vb, ib):
    return jnp.logical_or(va > vb, jnp.logical_and(va == vb, ia < ib))


def _exchange(vals, ids, lo, hi):
    first = _before(vals[lo], ids[lo], vals[hi], ids[hi])
    vals[lo], vals[hi] = (jnp.where(first, vals[lo], vals[hi]), jnp.where(first, vals[hi], vals[lo]))
    ids[lo], ids[hi] = (jnp.where(first, ids[lo], ids[hi]), jnp.where(first, ids[hi], ids[lo]))


def _sort16(lst):
    vals, ids = list(lst[0]), list(lst[1])
    for lo, hi in _SORT16:
        _exchange(vals, ids, lo, hi)
    return vals, ids


def _merge16(lst_a, lst_b):
    (va, ia), (vb, ib) = lst_a, lst_b
    vals, ids = [], []
    for w in range(PEER_TOPK):
        r = PEER_TOPK - 1 - w
        first = _before(va[w], ia[w], vb[r], ib[r])
        vals.append(jnp.where(first, va[w], vb[r]))
        ids.append(jnp.where(first, ia[w], ib[r]))
    for lo, hi in _BITONIC16:
        _exchange(vals, ids, lo, hi)
    return vals, ids


def _sorted_topk(ref, c, val_ref, id_ref, lv_ref, li_ref):
    ngroups = PEER_NKEYS // PEER_TOPK
    tile = lambda r: slice(r * RK_G, (r + 1) * RK_G)

    def put(g, lst):
        for w in range(PEER_TOPK):
            lv_ref[g, w] = lst[0][w]
            li_ref[g, w] = lst[1][w]

    def get(g):
        return ([lv_ref[g, w] for w in range(PEER_TOPK)], [li_ref[g, w] for w in range(PEER_TOPK)])

    for g in range(ngroups):
        keys = range(g * PEER_TOPK, (g + 1) * PEER_TOPK)
        put(g, _sort16(([ref[c, tile(r), :] for r in keys],
                        [jnp.full((RK_G, LANES), float(r), F32) for r in keys])))
    width = ngroups
    while width > 2:
        for g in range(width // 2):
            put(g, _merge16(get(2 * g), get(2 * g + 1)))
        width //= 2
    vals, ids = _merge16(get(0), get(1))
    for w in range(PEER_TOPK):
        val_ref[c, w] = vals[w]
        id_ref[c, w] = ids[w]


def _route_kernel(xn_ref, wqt_ref, k1_ref, k2_ref, a_ref, b_ref, w_ref,
                  qt_ref, s_ref, tv_ref, ti_ref, oa_ref, ob_ref, ow_ref, lv_ref, li_ref):
    half = PEER_QDIM // 2
    dn = (((1,), (1,)), ((), ()))
    qt_ref[...] = lax.dot_general(wqt_ref[...], xn_ref[...], dn, preferred_element_type=F32)

    def head(h, carry):
        for c, k_ref in enumerate((k1_ref, k2_ref)):
            row0 = pl.multiple_of(h * PEER_QDIM + c * half, half)
            s = jnp.dot(k_ref[...], qt_ref[pl.ds(row0, half), :].astype(BF16),
                        preferred_element_type=F32)
            for g in range(RK_G):
                s_ref[c, pl.ds(g, PEER_NKEYS, stride=RK_G), :] = s[:, g * LANES:(g + 1) * LANES]

        for c in range(2):
            _sorted_topk(s_ref, c, tv_ref, ti_ref, lv_ref, li_ref)

        t1 = [tv_ref[0, i] for i in range(PEER_TOPK)]
        t2 = [tv_ref[1, j] for j in range(PEER_TOPK)]
        pad_v = jnp.full((RK_G, LANES), NEG_BIG, F32)
        pad_i = jnp.full((RK_G, LANES), float(PEER_TOPK * PEER_TOPK), F32)

        def cells(ijs):
            vals = [t1[i] + t2[j] for i, j in ijs] + [pad_v] * (PEER_TOPK - len(ijs))
            ids = ([jnp.full((RK_G, LANES), float(i * PEER_TOPK + j), F32) for i, j in ijs]
                   + [pad_i] * (PEER_TOPK - len(ijs)))
            return vals, ids

        rest = [(i, j) for i, j in _CAND_CELLS if i >= 2]
        assert len(rest) == 2 * PEER_TOPK + 10 - PEER_TOPK
        row01 = _merge16(cells([(0, j) for j in range(16)]), cells([(1, j) for j in range(8)]))
        mid = _sort16(cells(rest[:PEER_TOPK]))
        tail = _sort16(cells(rest[PEER_TOPK:]))
        top, top_pos = _merge16(row01, _merge16(mid, tail))

        es = [jnp.exp(t - top[0]) for t in top]
        inv = 1.0 / _tree(jnp.add, es)
        for k in range(PEER_TOPK):
            pos = top_pos[k].astype(I32)
            ri = lax.shift_right_logical(pos, 4)
            rj = lax.bitwise_and(pos, PEER_TOPK - 1)
            a_sel = jnp.zeros((RK_G, LANES), F32)
            b_sel = jnp.zeros((RK_G, LANES), F32)
            for r in range(PEER_TOPK):
                a_sel = jnp.where(ri == r, ti_ref[0, r], a_sel)
                b_sel = jnp.where(rj == r, ti_ref[1, r], b_sel)
            rows = pl.ds(pl.multiple_of((h * PEER_TOPK + k) * RK_G, RK_G), RK_G)
            oa_ref[rows, :] = a_sel
            ob_ref[rows, :] = b_sel
            ow_ref[rows, :] = es[k] * inv
        return carry

    lax.fori_loop(0, PEER_HEADS, head, 0)
    nslot = PEER_HEADS * PEER_TOPK
    for g in range(RK_G):
        rows = pl.ds(g, nslot, stride=RK_G)
        tok = slice(g * LANES, (g + 1) * LANES)
        a_ref[tok, :] = oa_ref[rows, :].T.astype(I32)
        b_ref[tok, :] = ob_ref[rows, :].T.astype(I32)
        w_ref[tok, :] = ow_ref[rows, :].T


def _route(xn, wqt_b, k1_b, k2_b):
    n = xn.shape[0]
    tm = RK_TM
    nslot = PEER_HEADS * PEER_TOPK
    out = lambda: pl.BlockSpec((tm, nslot), lambda i: (i, 0))
    vregs = lambda *lead: pltpu.VMEM(lead + (RK_G, LANES), F32)
    tiles = lambda *lead: pltpu.VMEM(lead[:-1] + (lead[-1] * RK_G, LANES), F32)
    return pl.pallas_call(
        _route_kernel,
        grid=(n // tm,),
        in_specs=[pl.BlockSpec((tm, D_MODEL), lambda i: (i, 0)),
                  pl.BlockSpec((PEER_HEADS * PEER_QDIM, D_MODEL), lambda i: (0, 0)),
                  pl.BlockSpec((PEER_NKEYS, PEER_QDIM // 2), lambda i: (0, 0)),
                  pl.BlockSpec((PEER_NKEYS, PEER_QDIM // 2), lambda i: (0, 0))],
        out_specs=[out(), out(), out()],
        out_shape=[jax.ShapeDtypeStruct((n, nslot), I32),
                   jax.ShapeDtypeStruct((n, nslot), I32),
                   jax.ShapeDtypeStruct((n, nslot), F32)],
        scratch_shapes=[pltpu.VMEM((PEER_HEADS * PEER_QDIM, tm), F32),
                        tiles(2, PEER_NKEYS), vregs(2, PEER_TOPK), vregs(2, PEER_TOPK),
                        tiles(nslot), tiles(nslot), tiles(nslot),
                        vregs(PEER_NKEYS // PEER_TOPK, PEER_TOPK),
                        vregs(PEER_NKEYS // PEER_TOPK, PEER_TOPK)],
        compiler_params=_cparams(("parallel",)),
        name="peer_route",
    )(xn, wqt_b, k1_b, k2_b)


EXP_CHUNK = 2048
EXP_NCHUNK = PEER_NKEYS * PEER_NKEYS // EXP_CHUNK
EXPV_CHUNK = 4096


def _expu_kernel(xn_ref, u_ref, a_ref, b_ref, z_ref, buf0_ref, buf1_ref):
    tm = xn_ref.shape[0]
    i = pl.program_id(0)
    j = pl.program_id(1)
    nt = pl.num_programs(0) - 1
    assert EXP_NCHUNK % 2 == 0
    groups_per_chunk = EXP_CHUNK // PEER_NKEYS

    def matmul_into(buf_ref):
        buf_ref[...] = lax.dot_general(xn_ref[...], u_ref[...], (((1,), (1,)), ((), ())),
                                       preferred_element_type=F32)

    def gather_from(buf_ref):
        a0 = lax.rem(j + EXP_NCHUNK - 1, EXP_NCHUNK) * groups_per_chunk
        for g in range(tm // SUBLANES):
            rs = slice(g * SUBLANES, (g + 1) * SUBLANES)
            bi = b_ref[rs, :]
            ai = a_ref[rs, :] - a0
            acc = z_ref[rs, :]
            for t in range(groups_per_chunk):
                got = jnp.take_along_axis(buf_ref[rs, t * PEER_NKEYS:(t + 1) * PEER_NKEYS], bi,
                                          axis=1, mode="promise_in_bounds")
                acc = jnp.where(ai == t, got, acc)
            z_ref[rs, :] = acc

    odd = lax.rem(j, 2) == 1
    real = i < nt
    first = jnp.logical_and(i == 0, j == 0)

    @pl.when(first)
    def _():
        matmul_into(buf0_ref)

    @pl.when(jnp.logical_and(real, j == 1))
    def _():
        z_ref[...] = jnp.zeros_like(z_ref)

    @pl.when(jnp.logical_and(real, odd))
    def _():
        gather_from(buf0_ref)
        matmul_into(buf1_ref)

    @pl.when(jnp.logical_and(jnp.logical_and(real, jnp.logical_not(odd)), jnp.logical_not(first)))
    def _():
        gather_from(buf1_ref)
        matmul_into(buf0_ref)

    @pl.when(jnp.logical_and(i == nt, j == 0))
    def _():
        gather_from(buf1_ref)


def _expert_u(xn, u_b, aidx, bidx, tm=1024):
    n = xn.shape[0]
    nexp = u_b.shape[0]
    assert nexp == EXP_NCHUNK * EXP_CHUNK
    nslot = PEER_HEADS * PEER_TOPK
    nt = n // tm
    lag = lambda i, j: (jnp.clip(jnp.where(j == 0, i - 1, i), 0, nt - 1), 0)
    return pl.pallas_call(
        _expu_kernel,
        grid=(nt + 1, EXP_NCHUNK),
        in_specs=[pl.BlockSpec((tm, D_MODEL), lambda i, j: (jnp.minimum(i, nt - 1), 0)),
                  pl.BlockSpec((EXP_CHUNK, D_MODEL), lambda i, j: (jnp.where(i == nt, 0, j), 0)),
                  pl.BlockSpec((tm, nslot), lag),
                  pl.BlockSpec((tm, nslot), lag)],
        out_specs=pl.BlockSpec((tm, nslot), lag),
        out_shape=jax.ShapeDtypeStruct((n, nslot), F32),
        scratch_shapes=[pltpu.VMEM((tm, EXP_CHUNK), F32), pltpu.VMEM((tm, EXP_CHUNK), F32)],
        compiler_params=_cparams(("arbitrary", "arbitrary")),
        name="peer_expert_u",
    )(xn, u_b, aidx, bidx)


SLAB_PAD = SUBLANES
SLAB_TOKENS = 256
U32 = jnp.uint32


def _bf16_bits(x):
    return pltpu.bitcast(x.astype(BF16).astype(F32), U32)


def _expv_kernel(z_ref, w_ref, a_ref, b_ref, v_ref, h_ref, o_ref, s_ref, acc_ref, c_ref):
    tm = z_ref.shape[0]
    nk = PEER_NKEYS
    pitch = tm // 2 + SLAB_PAD
    j = pl.program_id(1)

    @pl.when(j == 0)
    def _():
        c_ref[...] = w_ref[...] * _gelu(z_ref[...])
        sub = lax.broadcasted_iota(I32, (nk, nk), 0)
        dn = (((1,), (1,)), ((), ()))

        def slab(n):
            arow = a_ref[pl.ds(n, 1), :]
            brow = b_ref[pl.ds(n, 1), :]
            crow = c_ref[pl.ds(n, 1), :]
            oat = jnp.where(arow == sub, 1.0, 0.0).astype(BF16)
            wobt = jnp.where(brow == sub, crow, 0.0).astype(BF16)
            return lax.dot_general(oat, wobt, dn, preferred_element_type=F32)

        def toks(g, carry):
            base = g * (SLAB_TOKENS // 2)
            for t in range(SLAB_TOKENS // 2):
                row = base + t
                lo = lax.shift_right_logical(_bf16_bits(slab(2 * row)), jnp.uint32(16))
                hi = _bf16_bits(slab(2 * row + 1))
                s_ref[pl.ds(row, nk, stride=pitch), :] = lo | hi
            return carry

        lax.fori_loop(0, tm // SLAB_TOKENS, toks, 0)

    groups = EXPV_CHUNK // nk
    slabs = []
    for t in range(groups):
        r0 = pl.multiple_of((j * groups + t) * pitch, SUBLANES)
        slabs.append(pltpu.bitcast(s_ref[pl.ds(r0, tm // 2), :], BF16))
    part = jnp.dot(jnp.concatenate(slabs, axis=1), v_ref[...], preferred_element_type=F32)

    @pl.when(j == 0)
    def _():
        acc_ref[...] = part

    @pl.when(j > 0)
    def _():
        acc_ref[...] += part

    @pl.when(j == pl.num_programs(1) - 1)
    def _():
        o_ref[...] = h_ref[...] + acc_ref[...]


def _expert_v(zsel, w, aidx, bidx, v_b, h1, tm=512):
    n = zsel.shape[0]
    nexp = v_b.shape[0]
    nslot = PEER_HEADS * PEER_TOPK
    slot = lambda: pl.BlockSpec((tm, nslot), lambda i, j: (i, 0))
    return pl.pallas_call(
        _expv_kernel,
        grid=(n // tm, nexp // EXPV_CHUNK),
        in_specs=[slot(), slot(), slot(), slot(),
                  pl.BlockSpec((EXPV_CHUNK, D_MODEL), lambda i, j: (j, 0)),
                  pl.BlockSpec((tm, D_MODEL), lambda i, j: (i, 0))],
        out_specs=pl.BlockSpec((tm, D_MODEL), lambda i, j: (i, 0)),
        out_shape=jax.ShapeDtypeStruct((n, D_MODEL), F32),
        scratch_shapes=[pltpu.VMEM((PEER_NKEYS * (tm // 2 + SLAB_PAD), PEER_NKEYS), U32),
                        pltpu.VMEM((tm, D_MODEL), F32),
                        pltpu.VMEM((tm, nslot), F32)],
        compiler_params=_cparams(("parallel", "arbitrary")),
        name="peer_expert_v",
    )(zsel, w, aidx, bidx, v_b, h1)


def _ple_kernel(h_ref, p_ref, gp_ref, wg_ref, wp_ref, gf_ref, o_ref):
    h = h_ref[...]
    xn = _rms(h, gp_ref[...]).astype(BF16)
    gate = jax.nn.sigmoid(jnp.dot(xn, wg_ref[...], preferred_element_type=F32))
    proj = jnp.dot(p_ref[...].astype(BF16), wp_ref[...], preferred_element_type=F32)
    o_ref[...] = _rms(h + gate * proj, gf_ref[...])


def _ple(h2, p2, g_ple, wg_b, wp_b, g_final, tm=1024):
    n = h2.shape[0]
    return pl.pallas_call(
        _ple_kernel,
        grid=(n // tm,),
        in_specs=[pl.BlockSpec((tm, D_MODEL), lambda i: (i, 0)),
                  pl.BlockSpec((tm, PLE_DIM), lambda i: (i, 0)),
                  pl.BlockSpec((1, D_MODEL), lambda i: (0, 0)),
                  pl.BlockSpec((D_MODEL, D_MODEL), lambda i: (0, 0)),
                  pl.BlockSpec((PLE_DIM, D_MODEL), lambda i: (0, 0)),
                  pl.BlockSpec((1, D_MODEL), lambda i: (0, 0))],
        out_specs=pl.BlockSpec((tm, D_MODEL), lambda i: (i, 0)),
        out_shape=jax.ShapeDtypeStruct((n, D_MODEL), F32),
        compiler_params=_cparams(("parallel",)),
        name="ple_final",
    )(h2, p2, g_ple, wg_b, wp_b, g_final)


def _block_diag(w):
    g, d, _ = w.shape
    eye = jnp.eye(g, dtype=w.dtype)
    return (eye[:, None, :, None] * w[:, :, None, :]).reshape(g * d, g * d)


def kernel(x, p, g_mix, w_in, conv_w, conv_b, w_rg, b_rg, w_ig, b_ig, lru_lambda, g_lru_out,
           g_attn_out, w_out, rel_bias, g_ffn, peer_wq, peer_k1, peer_k2, peer_u, peer_v,
           g_ple, w_ple_gate, w_ple_proj, g_final):
    batch, seq, dm = x.shape
    n = batch * seq
    assert w_in.shape[0] == 1, "single-layer problem: the last kernel fuses the final norm"
    row = lambda v: v.reshape(1, -1).astype(F32)
    btab = _bias_tables(rel_bias)
    h = x.reshape(n, dm)
    for i in range(1):
        xl, gate, q, k, v = _inproj(h, row(g_mix[i]), w_in[i].astype(BF16))
        y_lru = _lru(xl, gate, conv_w[i], row(conv_b[i]),
                     _block_diag(w_rg[i]).astype(BF16), row(b_rg[i]),
                     _block_diag(w_ig[i]).astype(BF16), row(b_ig[i]),
                     row(lru_lambda[i]), row(g_lru_out[i]), batch, seq)
        y_att = _attention(q, k, v, btab, batch, seq)
        h1, xn = _outproj(y_lru, y_att, h, row(g_attn_out[i]), w_out[i].astype(BF16),
                          row(g_ffn[i]))
        aidx, bidx, gates = _route(xn, peer_wq[i].T.astype(BF16), peer_k1[i].astype(BF16),
                                   peer_k2[i].astype(BF16))
        zsel = _expert_u(xn, peer_u[i].astype(BF16), aidx, bidx)
        h2 = _expert_v(zsel, gates, aidx, bidx, peer_v[i].astype(BF16), h1)
        h = _ple(h2, p[i].reshape(n, -1), row(g_ple[i]), w_ple_gate[i].astype(BF16),
                 w_ple_proj[i].astype(BF16), g_final.reshape(1, -1))
    return h.reshape(batch, seq, dm)
```

```python
import math

import jax
import jax.numpy as jnp
from jax import lax
from jax.experimental import pallas as pl
from jax.experimental.pallas import tpu as pltpu

F32 = jnp.float32
BF16 = jnp.bfloat16
I32 = jnp.int32

D_MODEL = 1024
PLE_DIM = 256
D_LRU = 512
LRU_BLOCKS = 8
CONV_WIDTH = 4
LRU_C = 8.0
D_ATTN = 512
N_ATT_HEADS = 8
HEAD_DIM = 64
DILATED_GROUPS = ((128, 1), (512, 4), (2048, 16))
ATT_BLOCK = 128
REL_BUCKETS = 32
REL_MAX_DIST = 2048
PEER_HEADS = 8
PEER_NKEYS = 128
PEER_QDIM = 256
PEER_TOPK = 16
NORM_EPS = 1e-6

LANES = 128
SUBLANES = 8
NEG_BIG = -1e30
VMEM_LIMIT = 56 * 1024 * 1024


def _cparams(sem, flags=None):
    return pltpu.CompilerParams(dimension_semantics=sem, vmem_limit_bytes=VMEM_LIMIT, flags=flags)


def _rms(x, g):
    ms = jnp.mean(x * x, axis=-1, keepdims=True)
    return (x * lax.rsqrt(ms + NORM_EPS)) * g


def _gelu(x):
    c = math.sqrt(2.0 / math.pi)
    return x * (0.5 * (1.0 + jnp.tanh(c * (x + 0.044715 * (x * x * x)))))


def _inproj_kernel(x_ref, g_ref, w_ref, xl_ref, gate_ref, q_ref, k_ref, v_ref):
    u = _rms(x_ref[...], g_ref[...]).astype(BF16)
    outs = (xl_ref, gate_ref, q_ref, k_ref, v_ref)
    for i, o_ref in enumerate(outs):
        z = jnp.dot(u, w_ref[:, i * 512:(i + 1) * 512], preferred_element_type=F32)
        if o_ref is q_ref:
            z = z * (HEAD_DIM ** -0.5)
        o_ref[...] = z


def _inproj(x2, g_mix, w_in_b, tm=1024):
    n = x2.shape[0]
    blk = lambda: pl.BlockSpec((tm, 512), lambda i: (i, 0))
    return pl.pallas_call(
        _inproj_kernel,
        grid=(n // tm,),
        in_specs=[pl.BlockSpec((tm, D_MODEL), lambda i: (i, 0)),
                  pl.BlockSpec((1, D_MODEL), lambda i: (0, 0)),
                  pl.BlockSpec((D_MODEL, 2560), lambda i: (0, 0))],
        out_specs=[blk() for _ in range(5)],
        out_shape=[jax.ShapeDtypeStruct((n, 512), F32) for _ in range(5)],
        compiler_params=_cparams(("parallel",)),
        name="inproj",
    )(x2, g_mix, w_in_b)


LRU_CHUNK = 64
LRU_UNROLL = 4


def _lru_kernel(xl_ref, gate_ref, cw_ref, cb_ref, wrg_ref, brg_ref, wig_ref, big_ref,
                lam_ref, g_ref, o_ref, xp_ref, h_ref):
    T = xl_ref.shape[0]
    tc = LRU_CHUNK
    xp_ref[0:SUBLANES, :] = jnp.zeros((SUBLANES, D_LRU), F32)
    xp_ref[SUBLANES:SUBLANES + T, :] = xl_ref[...]
    h_ref[...] = jnp.zeros_like(h_ref)
    nlam = -lam_ref[...]
    sp = jnp.maximum(nlam, 0.0) + jnp.log1p(jnp.exp(-jnp.abs(nlam)))
    sub_row = lax.broadcasted_iota(I32, (tc // SUBLANES, SUBLANES, D_LRU), 1)

    def chunk(c, carry):
        t0 = pl.multiple_of(c * tc, tc)
        xa = xp_ref[pl.ds(t0, tc + SUBLANES), :]
        xc = cb_ref[...] + cw_ref[CONV_WIDTH - 1:CONV_WIDTH, :] * xa[SUBLANES:, :]
        for j in range(CONV_WIDTH - 1):
            sh = pltpu.roll(xa, CONV_WIDTH - 1 - j, axis=0)[SUBLANES:, :]
            xc = xc + cw_ref[j:j + 1, :] * sh
        xcb = xc.astype(BF16)
        r = jax.nn.sigmoid(jnp.dot(xcb, wrg_ref[...], preferred_element_type=F32) + brg_ref[...])
        ig = jax.nn.sigmoid(jnp.dot(xcb, wig_ref[...], preferred_element_type=F32) + big_ref[...])
        log_a = (-LRU_C) * r * sp
        a = jnp.exp(log_a)
        om = -jnp.tanh(log_a) * (a * a + 1.0)
        root = jnp.where(om > 0.0, om * lax.rsqrt(om), 0.0)
        b = root * (ig * xc)
        a = a.reshape(tc // SUBLANES, SUBLANES, D_LRU)
        b = b.reshape(tc // SUBLANES, SUBLANES, D_LRU)
        d = 1
        while d < SUBLANES:
            keep = sub_row >= d
            a_sh = jnp.where(keep, pltpu.roll(a, d, axis=1), 1.0)
            b_sh = jnp.where(keep, pltpu.roll(b, d, axis=1), 0.0)
            b = a * b_sh + b
            a = a * a_sh
            d *= 2
        last = h_ref[...]
        hs = []
        for g in range(tc // SUBLANES):
            hg = b[g] + a[g] * last
            hs.append(hg)
            last = hg[SUBLANES - 1:SUBLANES, :]
        h = jnp.concatenate(hs, axis=0)
        h_ref[...] = last
        y = h * _gelu(gate_ref[pl.ds(t0, tc), :])
        o_ref[pl.ds(t0, tc), :] = _rms(y, g_ref[...]).astype(BF16)
        return carry

    lax.fori_loop(0, T // tc, chunk, 0, unroll=LRU_UNROLL)


def _lru(xl, gate, conv_w, conv_b, wrg_bd, b_rg, wig_bd, b_ig, lam, g_lru, batch, seq):
    vec = lambda: pl.BlockSpec((1, D_LRU), lambda b: (0, 0))
    mat = lambda: pl.BlockSpec((D_LRU, D_LRU), lambda b: (0, 0))
    return pl.pallas_call(
        _lru_kernel,
        grid=(batch,),
        in_specs=[pl.BlockSpec((seq, D_LRU), lambda b: (b, 0)),
                  pl.BlockSpec((seq, D_LRU), lambda b: (b, 0)),
                  pl.BlockSpec((CONV_WIDTH, D_LRU), lambda b: (0, 0)),
                  vec(), mat(), vec(), mat(), vec(), vec(), vec()],
        out_specs=pl.BlockSpec((seq, D_LRU), lambda b: (b, 0)),
        out_shape=jax.ShapeDtypeStruct((batch * seq, D_LRU), BF16),
        scratch_shapes=[pltpu.VMEM((seq + SUBLANES, D_LRU), F32),
                        pltpu.VMEM((1, D_LRU), F32)],
        compiler_params=_cparams(("parallel",)),
        name="rglru",
    )(xl, gate, conv_w, conv_b, wrg_bd, b_rg, wig_bd, b_ig, lam, g_lru)


ATT_UNROLL = 16


def _attend(qf, kf, vf, bias_a, bias_b, is_a):
    kb = kf.astype(BF16)
    vb = vf.astype(BF16)
    dn = (((1,), (1,)), ((), ()))
    outs = []
    for q_h, bias in ((jnp.where(is_a, qf, 0.0), bias_a), (jnp.where(is_a, 0.0, qf), bias_b)):
        s = lax.dot_general(q_h.astype(BF16), kb, dn, preferred_element_type=F32) + bias
        m = jnp.max(s, axis=-1, keepdims=True)
        e = jnp.exp(s - m)
        den = jnp.sum(e, axis=-1, keepdims=True)
        num = jnp.dot(e.astype(BF16), vb, preferred_element_type=F32)
        outs.append((num, m, den))
    (na, ma, da), (nb, mb, db) = outs
    shape = na.shape
    return (jnp.where(is_a, na, nb),
            jnp.where(is_a, jnp.broadcast_to(ma, shape), jnp.broadcast_to(mb, shape)),
            jnp.where(is_a, jnp.broadcast_to(da, shape), jnp.broadcast_to(db, shape)))


def _attn_kernel(q_ref, k_ref, v_ref, bt_ref, o_ref, num_ref, m_ref, den_ref):
    T = q_ref.shape[0]
    blk = ATT_BLOCK
    is_a = lax.broadcasted_iota(I32, (blk, LANES), 1) < HEAD_DIM

    def rows(start, size, d):
        return pl.ds(start, size) if d == 1 else pl.ds(start, size, stride=d)

    def first_block(br, d, r):
        sl = rows(r, blk, d)
        pad = jnp.zeros((blk, LANES), F32)
        left = lax.broadcasted_iota(I32, (blk, 2 * blk), 1) < blk
        out = _attend(q_ref[sl, :], jnp.concatenate([pad, k_ref[sl, :]], axis=0),
                      jnp.concatenate([pad, v_ref[sl, :]], axis=0),
                      jnp.where(left, NEG_BIG, bt_ref[br, 0]),
                      jnp.where(left, NEG_BIG, bt_ref[br, 1]), is_a)
        for ref, val in zip((num_ref, m_ref, den_ref), out):
            ref[br, sl, :] = val

    def later_block(br, d, r, n):
        qs = rows(r + n * (blk * d), blk, d)
        ks = rows(r + (n - 1) * (blk * d), 2 * blk, d)
        out = _attend(q_ref[qs, :], k_ref[ks, :], v_ref[ks, :],
                      bt_ref[br, 0], bt_ref[br, 1], is_a)
        for ref, val in zip((num_ref, m_ref, den_ref), out):
            ref[br, qs, :] = val

    for br, (_, d) in enumerate(DILATED_GROUPS):
        nb = T // (d * blk)
        def firsts(r, carry, br=br, d=d):
            first_block(br, d, r)
            return carry

        def laters(i, carry, br=br, d=d, nb=nb):
            r = lax.div(i, nb - 1)
            later_block(br, d, r, 1 + lax.rem(i, nb - 1))
            return carry

        if d == 1:
            first_block(br, d, 0)
        else:
            lax.fori_loop(0, d, firsts, 0, unroll=ATT_UNROLL)
        if nb > 1:
            lax.fori_loop(0, d * (nb - 1), laters, 0, unroll=ATT_UNROLL)

    def merge(c, carry):
        sl = pl.ds(pl.multiple_of(c * blk, blk), blk)
        m0, m1, m2 = m_ref[0, sl, :], m_ref[1, sl, :], m_ref[2, sl, :]
        mm = jnp.maximum(jnp.maximum(m0, m1), m2)
        w0, w1, w2 = jnp.exp(m0 - mm), jnp.exp(m1 - mm), jnp.exp(m2 - mm)
        num = w0 * num_ref[0, sl, :] + w1 * num_ref[1, sl, :] + w2 * num_ref[2, sl, :]
        den = w0 * den_ref[0, sl, :] + w1 * den_ref[1, sl, :] + w2 * den_ref[2, sl, :]
        o_ref[sl, :] = num / den
        return carry

    lax.fori_loop(0, T // blk, merge, 0)


def _attention(q, k, v, btab, batch, seq):
    npairs = D_ATTN // LANES
    qkv = lambda: pl.BlockSpec((seq, LANES), lambda b, p: (b, p))
    return pl.pallas_call(
        _attn_kernel,
        grid=(batch, npairs),
        in_specs=[qkv(), qkv(), qkv(),
                  pl.BlockSpec((3, 2, ATT_BLOCK, 2 * ATT_BLOCK), lambda b, p: (0, p, 0, 0))],
        out_specs=pl.BlockSpec((seq, LANES), lambda b, p: (b, p)),
        out_shape=jax.ShapeDtypeStruct((batch * seq, D_ATTN), F32),
        scratch_shapes=[pltpu.VMEM((3, seq, LANES), F32) for _ in range(3)],
        compiler_params=_cparams(("parallel", "parallel")),
        name="dilated_attn",
    )(q, k, v, btab)


def _bias_tables(rel_bias):
    tabs = []
    for window, d in DILATED_GROUPS:
        w_sub = window // d
        qi = jnp.arange(ATT_BLOCK)[:, None]
        kj = jnp.arange(2 * ATT_BLOCK)[None, :]
        dist = qi + ATT_BLOCK - kj
        max_exact = REL_BUCKETS // 2
        nn = jnp.maximum(dist * d, 0)
        nf = jnp.maximum(nn, 1).astype(F32)
        large = max_exact + (jnp.log(nf / max_exact) / math.log(REL_MAX_DIST / max_exact)
                             * (REL_BUCKETS - max_exact)).astype(I32)
        large = jnp.minimum(large, REL_BUCKETS - 1)
        bucket = jnp.where(nn < max_exact, nn, large)
        rb = rel_bias.astype(F32)
        bias = jnp.zeros((N_ATT_HEADS,) + bucket.shape, F32)
        for bkt in range(REL_BUCKETS):
            bias = jnp.where(bucket[None] == bkt, rb[bkt][:, None, None], bias)
        ok = (dist >= 0) & (dist <= w_sub)
        tabs.append(jnp.where(ok[None], bias, NEG_BIG))
    return jnp.stack(tabs, axis=0)


def _outproj_kernel(yl_ref, ya_ref, x_ref, ga_ref, w_ref, gf_ref, h_ref, xn_ref):
    ya = _rms(ya_ref[...], ga_ref[...]).astype(BF16)
    acc = jnp.dot(yl_ref[...], w_ref[0:D_LRU, :], preferred_element_type=F32)
    acc = acc + jnp.dot(ya, w_ref[D_LRU:, :], preferred_element_type=F32)
    h = x_ref[...] + acc
    h_ref[...] = h
    xn_ref[...] = _rms(h, gf_ref[...]).astype(BF16)


def _outproj(y_lru, y_att, x2, g_att, w_out_b, g_ffn, tm=1024):
    n = x2.shape[0]
    return pl.pallas_call(
        _outproj_kernel,
        grid=(n // tm,),
        in_specs=[pl.BlockSpec((tm, D_LRU), lambda i: (i, 0)),
                  pl.BlockSpec((tm, D_ATTN), lambda i: (i, 0)),
                  pl.BlockSpec((tm, D_MODEL), lambda i: (i, 0)),
                  pl.BlockSpec((1, D_ATTN), lambda i: (0, 0)),
                  pl.BlockSpec((D_MODEL, D_MODEL), lambda i: (0, 0)),
                  pl.BlockSpec((1, D_MODEL), lambda i: (0, 0))],
        out_specs=[pl.BlockSpec((tm, D_MODEL), lambda i: (i, 0)),
                   pl.BlockSpec((tm, D_MODEL), lambda i: (i, 0))],
        out_shape=[jax.ShapeDtypeStruct((n, D_MODEL), F32),
                   jax.ShapeDtypeStruct((n, D_MODEL), BF16)],
        compiler_params=_cparams(("parallel",)),
        name="outproj",
    )(y_lru, y_att, x2, g_att, w_out_b, g_ffn)


def _tree(op, xs):
    xs = list(xs)
    while len(xs) > 1:
        xs = [op(xs[i], xs[i + 1]) for i in range(0, len(xs) - 1, 2)] + (xs[-1:] if len(xs) % 2 else [])
    return xs[0]


RK_TM = 1024
RK_G = RK_TM // LANES
_CAND_CELLS = tuple((i, j) for i in range(PEER_TOPK) for j in range(PEER_TOPK)
                    if (i + 1) * (j + 1) <= PEER_TOPK)


def _oddeven_pairs(n):
    pairs = []
    p = 1
    while p < n:
        k = p
        while k >= 1:
            for j in range(k % p, n - k, 2 * k):
                for i in range(min(k, n - j - k)):
                    if (i + j) // (2 * p) == (i + j + k) // (2 * p):
                        pairs.append((i + j, i + j + k))
            k //= 2
        p *= 2
    return pairs


_SORT16 = _oddeven_pairs(PEER_TOPK)
_BITONIC16 = tuple((i, i + s) for s in (8, 4, 2, 1) for i in range(PEER_TOPK) if not i & s)


def _before(va, ia, vb, ib):
    return jnp.logical_or(va > vb, jnp.logical_and(va == vb, ia < ib))


def _exchange(vals, ids, lo, hi):
    first = _before(vals[lo], ids[lo], vals[hi], ids[hi])
    vals[lo], vals[hi] = (jnp.where(first, vals[lo], vals[hi]), jnp.where(first, vals[hi], vals[lo]))
    ids[lo], ids[hi] = (jnp.where(first, ids[lo], ids[hi]), jnp.where(first, ids[hi], ids[lo]))


def _sort16(lst):
    vals, ids = list(lst[0]), list(lst[1])
    for lo, hi in _SORT16:
        _exchange(vals, ids, lo, hi)
    return vals, ids


def _merge16(lst_a, lst_b):
    (va, ia), (vb, ib) = lst_a, lst_b
    vals, ids = [], []
    for w in range(PEER_TOPK):
        r = PEER_TOPK - 1 - w
        first = _before(va[w], ia[w], vb[r], ib[r])
        vals.append(jnp.where(first, va[w], vb[r]))
        ids.append(jnp.where(first, ia[w], ib[r]))
    for lo, hi in _BITONIC16:
        _exchange(vals, ids, lo, hi)
    return vals, ids


def _sorted_topk(ref, c, val_ref, id_ref, lv_ref, li_ref):
    ngroups = PEER_NKEYS // PEER_TOPK
    tile = lambda r: slice(r * RK_G, (r + 1) * RK_G)

    def put(g, lst):
        for w in range(PEER_TOPK):
            lv_ref[g, w] = lst[0][w]
            li_ref[g, w] = lst[1][w]

    def get(g):
        return ([lv_ref[g, w] for w in range(PEER_TOPK)], [li_ref[g, w] for w in range(PEER_TOPK)])

    for g in range(ngroups):
        keys = range(g * PEER_TOPK, (g + 1) * PEER_TOPK)
        put(g, _sort16(([ref[c, tile(r), :] for r in keys],
                        [jnp.full((RK_G, LANES), float(r), F32) for r in keys])))
    width = ngroups
    while width > 2:
        for g in range(width // 2):
            put(g, _merge16(get(2 * g), get(2 * g + 1)))
        width //= 2
    vals, ids = _merge16(get(0), get(1))
    for w in range(PEER_TOPK):
        val_ref[c, w] = vals[w]
        id_ref[c, w] = ids[w]


def _route_kernel(xn_ref, wqt_ref, k1_ref, k2_ref, a_ref, b_ref, w_ref,
                  qt_ref, s_ref, tv_ref, ti_ref, oa_ref, ob_ref, ow_ref, lv_ref, li_ref):
    half = PEER_QDIM // 2
    dn = (((1,), (1,)), ((), ()))
    qt_ref[...] = lax.dot_general(wqt_ref[...], xn_ref[...], dn, preferred_element_type=F32)

    def head(h, carry):
        for c, k_ref in enumerate((k1_ref, k2_ref)):
            row0 = pl.multiple_of(h * PEER_QDIM + c * half, half)
            s = jnp.dot(k_ref[...], qt_ref[pl.ds(row0, half), :].astype(BF16),
                        preferred_element_type=F32)
            for g in range(RK_G):
                s_ref[c, pl.ds(g, PEER_NKEYS, stride=RK_G), :] = s[:, g * LANES:(g + 1) * LANES]

        for c in range(2):
            _sorted_topk(s_ref, c, tv_ref, ti_ref, lv_ref, li_ref)

        t1 = [tv_ref[0, i] for i in range(PEER_TOPK)]
        t2 = [tv_ref[1, j] for j in range(PEER_TOPK)]
        pad_v = jnp.full((RK_G, LANES), NEG_BIG, F32)
        pad_i = jnp.full((RK_G, LANES), float(PEER_TOPK * PEER_TOPK), F32)

        def cells(ijs):
            vals = [t1[i] + t2[j] for i, j in ijs] + [pad_v] * (PEER_TOPK - len(ijs))
            ids = ([jnp.full((RK_G, LANES), float(i * PEER_TOPK + j), F32) for i, j in ijs]
                   + [pad_i] * (PEER_TOPK - len(ijs)))
            return vals, ids

        rest = [(i, j) for i, j in _CAND_CELLS if i >= 2]
        assert len(rest) == 2 * PEER_TOPK + 10 - PEER_TOPK
        row01 = _merge16(cells([(0, j) for j in range(16)]), cells([(1, j) for j in range(8)]))
        mid = _sort16(cells(rest[:PEER_TOPK]))
        tail = _sort16(cells(rest[PEER_TOPK:]))
        top, top_pos = _merge16(row01, _merge16(mid, tail))

        es = [jnp.exp(t - top[0]) for t in top]
        inv = 1.0 / _tree(jnp.add, es)
        for k in range(PEER_TOPK):
            pos = top_pos[k].astype(I32)
            ri = lax.shift_right_logical(pos, 4)
            rj = lax.bitwise_and(pos, PEER_TOPK - 1)
            a_sel = jnp.zeros((RK_G, LANES), F32)
            b_sel = jnp.zeros((RK_G, LANES), F32)
            for r in range(PEER_TOPK):
                a_sel = jnp.where(ri == r, ti_ref[0, r], a_sel)
                b_sel = jnp.where(rj == r, ti_ref[1, r], b_sel)
            rows = pl.ds(pl.multiple_of((h * PEER_TOPK + k) * RK_G, RK_G), RK_G)
            oa_ref[rows, :] = a_sel
            ob_ref[rows, :] = b_sel
            ow_ref[rows, :] = es[k] * inv
        return carry

    lax.fori_loop(0, PEER_HEADS, head, 0)
    nslot = PEER_HEADS * PEER_TOPK
    for g in range(RK_G):
        rows = pl.ds(g, nslot, stride=RK_G)
        tok = slice(g * LANES, (g + 1) * LANES)
        a_ref[tok, :] = oa_ref[rows, :].T.astype(I32)
        b_ref[tok, :] = ob_ref[rows, :].T.astype(I32)
        w_ref[tok, :] = ow_ref[rows, :].T


def _route(xn, wqt_b, k1_b, k2_b):
    n = xn.shape[0]
    tm = RK_TM
    nslot = PEER_HEADS * PEER_TOPK
    out = lambda: pl.BlockSpec((tm, nslot), lambda i: (i, 0))
    vregs = lambda *lead: pltpu.VMEM(lead + (RK_G, LANES), F32)
    tiles = lambda *lead: pltpu.VMEM(lead[:-1] + (lead[-1] * RK_G, LANES), F32)
    return pl.pallas_call(
        _route_kernel,
        grid=(n // tm,),
        in_specs=[pl.BlockSpec((tm, D_MODEL), lambda i: (i, 0)),
                  pl.BlockSpec((PEER_HEADS * PEER_QDIM, D_MODEL), lambda i: (0, 0)),
                  pl.BlockSpec((PEER_NKEYS, PEER_QDIM // 2), lambda i: (0, 0)),
                  pl.BlockSpec((PEER_NKEYS, PEER_QDIM // 2), lambda i: (0, 0))],
        out_specs=[out(), out(), out()],
        out_shape=[jax.ShapeDtypeStruct((n, nslot), I32),
                   jax.ShapeDtypeStruct((n, nslot), I32),
                   jax.ShapeDtypeStruct((n, nslot), F32)],
        scratch_shapes=[pltpu.VMEM((PEER_HEADS * PEER_QDIM, tm), F32),
                        tiles(2, PEER_NKEYS), vregs(2, PEER_TOPK), vregs(2, PEER_TOPK),
                        tiles(nslot), tiles(nslot), tiles(nslot),
                        vregs(PEER_NKEYS // PEER_TOPK, PEER_TOPK),
                        vregs(PEER_NKEYS // PEER_TOPK, PEER_TOPK)],
        compiler_params=_cparams(("parallel",)),
        name="peer_route",
    )(xn, wqt_b, k1_b, k2_b)


EXP_CHUNK = 2048
EXP_NCHUNK = PEER_NKEYS * PEER_NKEYS // EXP_CHUNK
EXPV_CHUNK = 4096


def _expu_kernel(xn_ref, u_ref, a_ref, b_ref, z_ref, buf0_ref, buf1_ref):
    tm = xn_ref.shape[0]
    i = pl.program_id(0)
    j = pl.program_id(1)
    nt = pl.num_programs(0) - 1
    assert EXP_NCHUNK % 2 == 0
    groups_per_chunk = EXP_CHUNK // PEER_NKEYS

    def matmul_into(buf_ref):
        buf_ref[...] = lax.dot_general(xn_ref[...], u_ref[...], (((1,), (1,)), ((), ())),
                                       preferred_element_type=F32)

    def gather_from(buf_ref):
        a0 = lax.rem(j + EXP_NCHUNK - 1, EXP_NCHUNK) * groups_per_chunk
        for g in range(tm // SUBLANES):
            rs = slice(g * SUBLANES, (g + 1) * SUBLANES)
            bi = b_ref[rs, :]
            ai = a_ref[rs, :] - a0
            acc = z_ref[rs, :]
            for t in range(groups_per_chunk):
                got = jnp.take_along_axis(buf_ref[rs, t * PEER_NKEYS:(t + 1) * PEER_NKEYS], bi,
                                          axis=1, mode="promise_in_bounds")
                acc = jnp.where(ai == t, got, acc)
            z_ref[rs, :] = acc

    odd = lax.rem(j, 2) == 1
    real = i < nt
    first = jnp.logical_and(i == 0, j == 0)

    @pl.when(first)
    def _():
        matmul_into(buf0_ref)

    @pl.when(jnp.logical_and(real, j == 1))
    def _():
        z_ref[...] = jnp.zeros_like(z_ref)

    @pl.when(jnp.logical_and(real, odd))
    def _():
        gather_from(buf0_ref)
        matmul_into(buf1_ref)

    @pl.when(jnp.logical_and(jnp.logical_and(real, jnp.logical_not(odd)), jnp.logical_not(first)))
    def _():
        gather_from(buf1_ref)
        matmul_into(buf0_ref)

    @pl.when(jnp.logical_and(i == nt, j == 0))
    def _():
        gather_from(buf1_ref)


def _expert_u(xn, u_b, aidx, bidx, tm=1024):
    n = xn.shape[0]
    nexp = u_b.shape[0]
    assert nexp == EXP_NCHUNK * EXP_CHUNK
    nslot = PEER_HEADS * PEER_TOPK
    nt = n // tm
    lag = lambda i, j: (jnp.clip(jnp.where(j == 0, i - 1, i), 0, nt - 1), 0)
    return pl.pallas_call(
        _expu_kernel,
        grid=(nt + 1, EXP_NCHUNK),
        in_specs=[pl.BlockSpec((tm, D_MODEL), lambda i, j: (jnp.minimum(i, nt - 1), 0)),
                  pl.BlockSpec((EXP_CHUNK, D_MODEL), lambda i, j: (jnp.where(i == nt, 0, j), 0)),
                  pl.BlockSpec((tm, nslot), lag),
                  pl.BlockSpec((tm, nslot), lag)],
        out_specs=pl.BlockSpec((tm, nslot), lag),
        out_shape=jax.ShapeDtypeStruct((n, nslot), F32),
        scratch_shapes=[pltpu.VMEM((tm, EXP_CHUNK), F32), pltpu.VMEM((tm, EXP_CHUNK), F32)],
        compiler_params=_cparams(("arbitrary", "arbitrary")),
        name="peer_expert_u",
    )(xn, u_b, aidx, bidx)


SLAB_PAD = SUBLANES
SLAB_TOKENS = 256
U32 = jnp.uint32


def _bf16_bits(x):
    return pltpu.bitcast(x.astype(BF16).astype(F32), U32)


def _expv_kernel(z_ref, w_ref, a_ref, b_ref, v_ref, h_ref, o_ref, s_ref, acc_ref, c_ref):
    tm = z_ref.shape[0]
    nk = PEER_NKEYS
    pitch = tm // 2 + SLAB_PAD
    j = pl.program_id(1)

    @pl.when(j == 0)
    def _():
        c_ref[...] = w_ref[...] * _gelu(z_ref[...])
        sub = lax.broadcasted_iota(I32, (nk, nk), 0)
        dn = (((1,), (1,)), ((), ()))

        def slab(n):
            arow = a_ref[pl.ds(n, 1), :]
            brow = b_ref[pl.ds(n, 1), :]
            crow = c_ref[pl.ds(n, 1), :]
            oat = jnp.where(arow == sub, 1.0, 0.0).astype(BF16)
            wobt = jnp.where(brow == sub, crow, 0.0).astype(BF16)
            return lax.dot_general(oat, wobt, dn, preferred_element_type=F32)

        def toks(g, carry):
            base = g * (SLAB_TOKENS // 2)
            for t in range(SLAB_TOKENS // 2):
                row = base + t
                lo = lax.shift_right_logical(_bf16_bits(slab(2 * row)), jnp.uint32(16))
                hi = _bf16_bits(slab(2 * row + 1))
                s_ref[pl.ds(row, nk, stride=pitch), :] = lo | hi
            return carry

        lax.fori_loop(0, tm // SLAB_TOKENS, toks, 0)

    groups = EXPV_CHUNK // nk
    slabs = []
    for t in range(groups):
        r0 = pl.multiple_of((j * groups + t) * pitch, SUBLANES)
        slabs.append(pltpu.bitcast(s_ref[pl.ds(r0, tm // 2), :], BF16))
    part = jnp.dot(jnp.concatenate(slabs, axis=1), v_ref[...], preferred_element_type=F32)

    @pl.when(j == 0)
    def _():
        acc_ref[...] = part

    @pl.when(j > 0)
    def _():
        acc_ref[...] += part

    @pl.when(j == pl.num_programs(1) - 1)
    def _():
        o_ref[...] = h_ref[...] + acc_ref[...]


def _expert_v(zsel, w, aidx, bidx, v_b, h1, tm=512):
    n = zsel.shape[0]
    nexp = v_b.shape[0]
    nslot = PEER_HEADS * PEER_TOPK
    slot = lambda: pl.BlockSpec((tm, nslot), lambda i, j: (i, 0))
    return pl.pallas_call(
        _expv_kernel,
        grid=(n // tm, nexp // EXPV_CHUNK),
        in_specs=[slot(), slot(), slot(), slot(),
                  pl.BlockSpec((EXPV_CHUNK, D_MODEL), lambda i, j: (j, 0)),
                  pl.BlockSpec((tm, D_MODEL), lambda i, j: (i, 0))],
        out_specs=pl.BlockSpec((tm, D_MODEL), lambda i, j: (i, 0)),
        out_shape=jax.ShapeDtypeStruct((n, D_MODEL), F32),
        scratch_shapes=[pltpu.VMEM((PEER_NKEYS * (tm // 2 + SLAB_PAD), PEER_NKEYS), U32),
                        pltpu.VMEM((tm, D_MODEL), F32),
                        pltpu.VMEM((tm, nslot), F32)],
        compiler_params=_cparams(("parallel", "arbitrary")),
        name="peer_expert_v",
    )(zsel, w, aidx, bidx, v_b, h1)


def _ple_kernel(h_ref, p_ref, gp_ref, wg_ref, wp_ref, gf_ref, o_ref):
    h = h_ref[...]
    xn = _rms(h, gp_ref[...]).astype(BF16)
    gate = jax.nn.sigmoid(jnp.dot(xn, wg_ref[...], preferred_element_type=F32))
    proj = jnp.dot(p_ref[...].astype(BF16), wp_ref[...], preferred_element_type=F32)
    o_ref[...] = _rms(h + gate * proj, gf_ref[...])


def _ple(h2, p2, g_ple, wg_b, wp_b, g_final, tm=1024):
    n = h2.shape[0]
    return pl.pallas_call(
        _ple_kernel,
        grid=(n // tm,),
        in_specs=[pl.BlockSpec((tm, D_MODEL), lambda i: (i, 0)),
                  pl.BlockSpec((tm, PLE_DIM), lambda i: (i, 0)),
                  pl.BlockSpec((1, D_MODEL), lambda i: (0, 0)),
                  pl.BlockSpec((D_MODEL, D_MODEL), lambda i: (0, 0)),
                  pl.BlockSpec((PLE_DIM, D_MODEL), lambda i: (0, 0)),
                  pl.BlockSpec((1, D_MODEL), lambda i: (0, 0))],
        out_specs=pl.BlockSpec((tm, D_MODEL), lambda i: (i, 0)),
        out_shape=jax.ShapeDtypeStruct((n, D_MODEL), F32),
        compiler_params=_cparams(("parallel",)),
        name="ple_final",
    )(h2, p2, g_ple, wg_b, wp_b, g_final)


def _block_diag(w):
    g, d, _ = w.shape
    eye = jnp.eye(g, dtype=w.dtype)
    return (eye[:, None, :, None] * w[:, :, None, :]).reshape(g * d, g * d)


def kernel(x, p, g_mix, w_in, conv_w, conv_b, w_rg, b_rg, w_ig, b_ig, lru_lambda, g_lru_out,
           g_attn_out, w_out, rel_bias, g_ffn, peer_wq, peer_k1, peer_k2, peer_u, peer_v,
           g_ple, w_ple_gate, w_ple_proj, g_final):
    batch, seq, dm = x.shape
    n = batch * seq
    assert w_in.shape[0] == 1, "single-layer problem: the last kernel fuses the final norm"
    row = lambda v: v.reshape(1, -1).astype(F32)
    btab = _bias_tables(rel_bias)
    h = x.reshape(n, dm)
    for i in range(1):
        xl, gate, q, k, v = _inproj(h, row(g_mix[i]), w_in[i].astype(BF16))
        y_lru = _lru(xl, gate, conv_w[i], row(conv_b[i]),
                     _block_diag(w_rg[i]).astype(BF16), row(b_rg[i]),
                     _block_diag(w_ig[i]).astype(BF16), row(b_ig[i]),
                     row(lru_lambda[i]), row(g_lru_out[i]), batch, seq)
        y_att = _attention(q, k, v, btab, batch, seq)
        h1, xn = _outproj(y_lru, y_att, h, row(g_attn_out[i]), w_out[i].astype(BF16),
                          row(g_ffn[i]))
        aidx, bidx, gates = _route(xn, peer_wq[i].T.astype(BF16), peer_k1[i].astype(BF16),
                                   peer_k2[i].astype(BF16))
        zsel = _expert_u(xn, peer_u[i].astype(BF16), aidx, bidx)
        h2 = _expert_v(zsel, gates, aidx, bidx, peer_v[i].astype(BF16), h1)
        h = _ple(h2, p[i].reshape(n, -1), row(g_ple[i]), w_ple_gate[i].astype(BF16),
                 w_ple_proj[i].astype(BF16), g_final.reshape(1, -1))
    return h.reshape(batch, seq, dm)
```

```python
import math

import jax
import jax.numpy as jnp
from jax import lax
from jax.experimental import pallas as pl
from jax.experimental.pallas import tpu as pltpu

F32 = jnp.float32
BF16 = jnp.bfloat16
I32 = jnp.int32

D_MODEL = 1024
PLE_DIM = 256
D_LRU = 512
LRU_BLOCKS = 8
CONV_WIDTH = 4
LRU_C = 8.0
D_ATTN = 512
N_ATT_HEADS = 8
HEAD_DIM = 64
DILATED_GROUPS = ((128, 1), (512, 4), (2048, 16))
ATT_BLOCK = 128
REL_BUCKETS = 32
REL_MAX_DIST = 2048
PEER_HEADS = 8
PEER_NKEYS = 128
PEER_QDIM = 256
PEER_TOPK = 16
NORM_EPS = 1e-6

LANES = 128
SUBLANES = 8
NEG_BIG = -1e30
VMEM_LIMIT = 56 * 1024 * 1024


def _cparams(sem, flags=None):
    return pltpu.CompilerParams(dimension_semantics=sem, vmem_limit_bytes=VMEM_LIMIT, flags=flags)


def _rms(x, g):
    ms = jnp.mean(x * x, axis=-1, keepdims=True)
    return (x * lax.rsqrt(ms + NORM_EPS)) * g


def _gelu(x):
    c = math.sqrt(2.0 / math.pi)
    return x * (0.5 * (1.0 + jnp.tanh(c * (x + 0.044715 * (x * x * x)))))


def _inproj_kernel(x_ref, g_ref, w_ref, xl_ref, gate_ref, q_ref, k_ref, v_ref):
    u = _rms(x_ref[...], g_ref[...]).astype(BF16)
    outs = (xl_ref, gate_ref, q_ref, k_ref, v_ref)
    for i, o_ref in enumerate(outs):
        z = jnp.dot(u, w_ref[:, i * 512:(i + 1) * 512], preferred_element_type=F32)
        if o_ref is q_ref:
            z = z * (HEAD_DIM ** -0.5)
        o_ref[...] = z


def _inproj(x2, g_mix, w_in_b, tm=1024):
    n = x2.shape[0]
    blk = lambda: pl.BlockSpec((tm, 512), lambda i: (i, 0))
    return pl.pallas_call(
        _inproj_kernel,
        grid=(n // tm,),
        in_specs=[pl.BlockSpec((tm, D_MODEL), lambda i: (i, 0)),
                  pl.BlockSpec((1, D_MODEL), lambda i: (0, 0)),
                  pl.BlockSpec((D_MODEL, 2560), lambda i: (0, 0))],
        out_specs=[blk() for _ in range(5)],
        out_shape=[jax.ShapeDtypeStruct((n, 512), F32) for _ in range(5)],
        compiler_params=_cparams(("parallel",)),
        name="inproj",
    )(x2, g_mix, w_in_b)


LRU_CHUNK = 64
LRU_UNROLL = 4


def _lru_kernel(xl_ref, gate_ref, cw_ref, cb_ref, wrg_ref, brg_ref, wig_ref, big_ref,
                lam_ref, g_ref, o_ref, xp_ref, h_ref):
    T = xl_ref.shape[0]
    tc = LRU_CHUNK
    xp_ref[0:SUBLANES, :] = jnp.zeros((SUBLANES, D_LRU), F32)
    xp_ref[SUBLANES:SUBLANES + T, :] = xl_ref[...]
    h_ref[...] = jnp.zeros_like(h_ref)
    nlam = -lam_ref[...]
    sp = jnp.maximum(nlam, 0.0) + jnp.log1p(jnp.exp(-jnp.abs(nlam)))
    sub_row = lax.broadcasted_iota(I32, (tc // SUBLANES, SUBLANES, D_LRU), 1)

    def chunk(c, carry):
        t0 = pl.multiple_of(c * tc, tc)
        xa = xp_ref[pl.ds(t0, tc + SUBLANES), :]
        xc = cb_ref[...] + cw_ref[CONV_WIDTH - 1:CONV_WIDTH, :] * xa[SUBLANES:, :]
        for j in range(CONV_WIDTH - 1):
            sh = pltpu.roll(xa, CONV_WIDTH - 1 - j, axis=0)[SUBLANES:, :]
            xc = xc + cw_ref[j:j + 1, :] * sh
        xcb = xc.astype(BF16)
        r = jax.nn.sigmoid(jnp.dot(xcb, wrg_ref[...], preferred_element_type=F32) + brg_ref[...])
        ig = jax.nn.sigmoid(jnp.dot(xcb, wig_ref[...], preferred_element_type=F32) + big_ref[...])
        log_a = (-LRU_C) * r * sp
        a = jnp.exp(log_a)
        om = -jnp.tanh(log_a) * (a * a + 1.0)
        root = jnp.where(om > 0.0, om * lax.rsqrt(om), 0.0)
        b = root * (ig * xc)
        a = a.reshape(tc // SUBLANES, SUBLANES, D_LRU)
        b = b.reshape(tc // SUBLANES, SUBLANES, D_LRU)
        d = 1
        while d < SUBLANES:
            keep = sub_row >= d
            a_sh = jnp.where(keep, pltpu.roll(a, d, axis=1), 1.0)
            b_sh = jnp.where(keep, pltpu.roll(b, d, axis=1), 0.0)
            b = a * b_sh + b
            a = a * a_sh
            d *= 2
        last = h_ref[...]
        hs = []
        for g in range(tc // SUBLANES):
            hg = b[g] + a[g] * last
            hs.append(hg)
            last = hg[SUBLANES - 1:SUBLANES, :]
        h = jnp.concatenate(hs, axis=0)
        h_ref[...] = last
        y = h * _gelu(gate_ref[pl.ds(t0, tc), :])
        o_ref[pl.ds(t0, tc), :] = _rms(y, g_ref[...]).astype(BF16)
        return carry

    lax.fori_loop(0, T // tc, chunk, 0, unroll=LRU_UNROLL)


def _lru(xl, gate, conv_w, conv_b, wrg_bd, b_rg, wig_bd, b_ig, lam, g_lru, batch, seq):
    vec = lambda: pl.BlockSpec((1, D_LRU), lambda b: (0, 0))
    mat = lambda: pl.BlockSpec((D_LRU, D_LRU), lambda b: (0, 0))
    return pl.pallas_call(
        _lru_kernel,
        grid=(batch,),
        in_specs=[pl.BlockSpec((seq, D_LRU), lambda b: (b, 0)),
                  pl.BlockSpec((seq, D_LRU), lambda b: (b, 0)),
                  pl.BlockSpec((CONV_WIDTH, D_LRU), lambda b: (0, 0)),
                  vec(), mat(), vec(), mat(), vec(), vec(), vec()],
        out_specs=pl.BlockSpec((seq, D_LRU), lambda b: (b, 0)),
        out_shape=jax.ShapeDtypeStruct((batch * seq, D_LRU), BF16),
        scratch_shapes=[pltpu.VMEM((seq + SUBLANES, D_LRU), F32),
                        pltpu.VMEM((1, D_LRU), F32)],
        compiler_params=_cparams(("parallel",)),
        name="rglru",
    )(xl, gate, conv_w, conv_b, wrg_bd, b_rg, wig_bd, b_ig, lam, g_lru)


ATT_UNROLL = 16


def _attend(qf, kf, vf, bias_a, bias_b, is_a):
    kb = kf.astype(BF16)
    vb = vf.astype(BF16)
    dn = (((1,), (1,)), ((), ()))
    outs = []
    for q_h, bias in ((jnp.where(is_a, qf, 0.0), bias_a), (jnp.where(is_a, 0.0, qf), bias_b)):
        s = lax.dot_general(q_h.astype(BF16), kb, dn, preferred_element_type=F32) + bias
        m = jnp.max(s, axis=-1, keepdims=True)
        e = jnp.exp(s - m)
        den = jnp.sum(e, axis=-1, keepdims=True)
        num = jnp.dot(e.astype(BF16), vb, preferred_element_type=F32)
        outs.append((num, m, den))
    (na, ma, da), (nb, mb, db) = outs
    shape = na.shape
    return (jnp.where(is_a, na, nb),
            jnp.where(is_a, jnp.broadcast_to(ma, shape), jnp.broadcast_to(mb, shape)),
            jnp.where(is_a, jnp.broadcast_to(da, shape), jnp.broadcast_to(db, shape)))


def _attn_kernel(q_ref, k_ref, v_ref, bt_ref, o_ref, num_ref, m_ref, den_ref):
    T = q_ref.shape[0]
    blk = ATT_BLOCK
    is_a = lax.broadcasted_iota(I32, (blk, LANES), 1) < HEAD_DIM

    def rows(start, size, d):
        return pl.ds(start, size) if d == 1 else pl.ds(start, size, stride=d)

    def first_block(br, d, r):
        sl = rows(r, blk, d)
        pad = jnp.zeros((blk, LANES), F32)
        left = lax.broadcasted_iota(I32, (blk, 2 * blk), 1) < blk
        out = _attend(q_ref[sl, :], jnp.concatenate([pad, k_ref[sl, :]], axis=0),
                      jnp.concatenate([pad, v_ref[sl, :]], axis=0),
                      jnp.where(left, NEG_BIG, bt_ref[br, 0]),
                      jnp.where(left, NEG_BIG, bt_ref[br, 1]), is_a)
        for ref, val in zip((num_ref, m_ref, den_ref), out):
            ref[br, sl, :] = val

    def later_block(br, d, r, n):
        qs = rows(r + n * (blk * d), blk, d)
        ks = rows(r + (n - 1) * (blk * d), 2 * blk, d)
        out = _attend(q_ref[qs, :], k_ref[ks, :], v_ref[ks, :],
                      bt_ref[br, 0], bt_ref[br, 1], is_a)
        for ref, val in zip((num_ref, m_ref, den_ref), out):
            ref[br, qs, :] = val

    for br, (_, d) in enumerate(DILATED_GROUPS):
        nb = T // (d * blk)
        def firsts(r, carry, br=br, d=d):
            first_block(br, d, r)
            return carry

        def laters(i, carry, br=br, d=d, nb=nb):
            r = lax.div(i, nb - 1)
            later_block(br, d, r, 1 + lax.rem(i, nb - 1))
            return carry

        if d == 1:
            first_block(br, d, 0)
        else:
            lax.fori_loop(0, d, firsts, 0, unroll=ATT_UNROLL)
        if nb > 1:
            lax.fori_loop(0, d * (nb - 1), laters, 0, unroll=ATT_UNROLL)

    def merge(c, carry):
        sl = pl.ds(pl.multiple_of(c * blk, blk), blk)
        m0, m1, m2 = m_ref[0, sl, :], m_ref[1, sl, :], m_ref[2, sl, :]
        mm = jnp.maximum(jnp.maximum(m0, m1), m2)
        w0, w1, w2 = jnp.exp(m0 - mm), jnp.exp(m1 - mm), jnp.exp(m2 - mm)
        num = w0 * num_ref[0, sl, :] + w1 * num_ref[1, sl, :] + w2 * num_ref[2, sl, :]
        den = w0 * den_ref[0, sl, :] + w1 * den_ref[1, sl, :] + w2 * den_ref[2, sl, :]
        o_ref[sl, :] = num / den
        return carry

    lax.fori_loop(0, T // blk, merge, 0)


def _attention(q, k, v, btab, batch, seq):
    npairs = D_ATTN // LANES
    qkv = lambda: pl.BlockSpec((seq, LANES), lambda b, p: (b, p))
    return pl.pallas_call(
        _attn_kernel,
        grid=(batch, npairs),
        in_specs=[qkv(), qkv(), qkv(),
                  pl.BlockSpec((3, 2, ATT_BLOCK, 2 * ATT_BLOCK), lambda b, p: (0, p, 0, 0))],
        out_specs=pl.BlockSpec((seq, LANES), lambda b, p: (b, p)),
        out_shape=jax.ShapeDtypeStruct((batch * seq, D_ATTN), F32),
        scratch_shapes=[pltpu.VMEM((3, seq, LANES), F32) for _ in range(3)],
        compiler_params=_cparams(("parallel", "parallel")),
        name="dilated_attn",
    )(q, k, v, btab)


def _bias_tables(rel_bias):
    tabs = []
    for window, d in DILATED_GROUPS:
        w_sub = window // d
        qi = jnp.arange(ATT_BLOCK)[:, None]
        kj = jnp.arange(2 * ATT_BLOCK)[None, :]
        dist = qi + ATT_BLOCK - kj
        max_exact = REL_BUCKETS // 2
        nn = jnp.maximum(dist * d, 0)
        nf = jnp.maximum(nn, 1).astype(F32)
        large = max_exact + (jnp.log(nf / max_exact) / math.log(REL_MAX_DIST / max_exact)
                             * (REL_BUCKETS - max_exact)).astype(I32)
        large = jnp.minimum(large, REL_BUCKETS - 1)
        bucket = jnp.where(nn < max_exact, nn, large)
        rb = rel_bias.astype(F32)
        bias = jnp.zeros((N_ATT_HEADS,) + bucket.shape, F32)
        for bkt in range(REL_BUCKETS):
            bias = jnp.where(bucket[None] == bkt, rb[bkt][:, None, None], bias)
        ok = (dist >= 0) & (dist <= w_sub)
        tabs.append(jnp.where(ok[None], bias, NEG_BIG))
    return jnp.stack(tabs, axis=0)


def _outproj_kernel(yl_ref, ya_ref, x_ref, ga_ref, w_ref, gf_ref, h_ref, xn_ref):
    ya = _rms(ya_ref[...], ga_ref[...]).astype(BF16)
    acc = jnp.dot(yl_ref[...], w_ref[0:D_LRU, :], preferred_element_type=F32)
    acc = acc + jnp.dot(ya, w_ref[D_LRU:, :], preferred_element_type=F32)
    h = x_ref[...] + acc
    h_ref[...] = h
    xn_ref[...] = _rms(h, gf_ref[...]).astype(BF16)


def _outproj(y_lru, y_att, x2, g_att, w_out_b, g_ffn, tm=1024):
    n = x2.shape[0]
    return pl.pallas_call(
        _outproj_kernel,
        grid=(n // tm,),
        in_specs=[pl.BlockSpec((tm, D_LRU), lambda i: (i, 0)),
                  pl.BlockSpec((tm, D_ATTN), lambda i: (i, 0)),
                  pl.BlockSpec((tm, D_MODEL), lambda i: (i, 0)),
                  pl.BlockSpec((1, D_ATTN), lambda i: (0, 0)),
                  pl.BlockSpec((D_MODEL, D_MODEL), lambda i: (0, 0)),
                  pl.BlockSpec((1, D_MODEL), lambda i: (0, 0))],
        out_specs=[pl.BlockSpec((tm, D_MODEL), lambda i: (i, 0)),
                   pl.BlockSpec((tm, D_MODEL), lambda i: (i, 0))],
        out_shape=[jax.ShapeDtypeStruct((n, D_MODEL), F32),
                   jax.ShapeDtypeStruct((n, D_MODEL), BF16)],
        compiler_params=_cparams(("parallel",)),
        name="outproj",
    )(y_lru, y_att, x2, g_att, w_out_b, g_ffn)


def _tree(op, xs):
    xs = list(xs)
    while len(xs) > 1:
        xs = [op(xs[i], xs[i + 1]) for i in range(0, len(xs) - 1, 2)] + (xs[-1:] if len(xs) % 2 else [])
    return xs[0]


RK_TM = 1024
RK_G = RK_TM // LANES
_CAND_CELLS = tuple((i, j) for i in range(PEER_TOPK) for j in range(PEER_TOPK)
                    if (i + 1) * (j + 1) <= PEER_TOPK)


def _oddeven_pairs(n):
    pairs = []
    p = 1
    while p < n:
        k = p
        while k >= 1:
            for j in range(k % p, n - k, 2 * k):
                for i in range(min(k, n - j - k)):
                    if (i + j) // (2 * p) == (i + j + k) // (2 * p):
                        pairs.append((i + j, i + j + k))
            k //= 2
        p *= 2
    return pairs


_SORT16 = _oddeven_pairs(PEER_TOPK)
_BITONIC16 = tuple((i, i + s) for s in (8, 4, 2, 1) for i in range(PEER_TOPK) if not i & s)


def _before(va, ia, vb, ib):
    return jnp.logical_or(va > vb, jnp.logical_and(va == vb, ia < ib))


def _exchange(vals, ids, lo, hi):
    first = _before(vals[lo], ids[lo], vals[hi], ids[hi])
    vals[lo], vals[hi] = (jnp.where(first, vals[lo], vals[hi]), jnp.where(first, vals[hi], vals[lo]))
    ids[lo], ids[hi] = (jnp.where(first, ids[lo], ids[hi]), jnp.where(first, ids[hi], ids[lo]))


def _sort16(lst):
    vals, ids = list(lst[0]), list(lst[1])
    for lo, hi in _SORT16:
        _exchange(vals, ids, lo, hi)
    return vals, ids


def _merge16(lst_a, lst_b):
    (va, ia), (vb, ib) = lst_a, lst_b
    vals, ids = [], []
    for w in range(PEER_TOPK):
        r = PEER_TOPK - 1 - w
        first = _before(va[w], ia[w], vb[r], ib[r])
        vals.append(jnp.where(first, va[w], vb[r]))
        ids.append(jnp.where(first, ia[w], ib[r]))
    for lo, hi in _BITONIC16:
        _exchange(vals, ids, lo, hi)
    return vals, ids


def _sorted_topk(ref, c, val_ref, id_ref, lv_ref, li_ref):
    ngroups = PEER_NKEYS // PEER_TOPK
    tile = lambda r: slice(r * RK_G, (r + 1) * RK_G)

    def put(g, lst):
        for w in range(PEER_TOPK):
            lv_ref[g, w] = lst[0][w]
            li_ref[g, w] = lst[1][w]

    def get(g):
        return ([lv_ref[g, w] for w in range(PEER_TOPK)], [li_ref[g, w] for w in range(PEER_TOPK)])

    for g in range(ngroups):
        keys = range(g * PEER_TOPK, (g + 1) * PEER_TOPK)
        put(g, _sort16(([ref[c, tile(r), :] for r in keys],
                        [jnp.full((RK_G, LANES), float(r), F32) for r in keys])))
    width = ngroups
    while width > 2:
        for g in range(width // 2):
            put(g, _merge16(get(2 * g), get(2 * g + 1)))
        width //= 2
    vals, ids = _merge16(get(0), get(1))
    for w in range(PEER_TOPK):
        val_ref[c, w] = vals[w]
        id_ref[c, w] = ids[w]


def _route_kernel(xn_ref, wqt_ref, k1_ref, k2_ref, a_ref, b_ref, w_ref,
                  qt_ref, s_ref, tv_ref, ti_ref, oa_ref, ob_ref, ow_ref, lv_ref, li_ref):
    half = PEER_QDIM // 2
    dn = (((1,), (1,)), ((), ()))
    qt_ref[...] = lax.dot_general(wqt_ref[...], xn_ref[...], dn, preferred_element_type=F32)

    def head(h, carry):
        for c, k_ref in enumerate((k1_ref, k2_ref)):
            row0 = pl.multiple_of(h * PEER_QDIM + c * half, half)
            s = jnp.dot(k_ref[...], qt_ref[pl.ds(row0, half), :].astype(BF16),
                        preferred_element_type=F32)
            for g in range(RK_G):
                s_ref[c, pl.ds(g, PEER_NKEYS, stride=RK_G), :] = s[:, g * LANES:(g + 1) * LANES]

        for c in range(2):
            _sorted_topk(s_ref, c, tv_ref, ti_ref, lv_ref, li_ref)

        t1 = [tv_ref[0, i] for i in range(PEER_TOPK)]
        t2 = [tv_ref[1, j] for j in range(PEER_TOPK)]
        pad_v = jnp.full((RK_G, LANES), NEG_BIG, F32)
        pad_i = jnp.full((RK_G, LANES), float(PEER_TOPK * PEER_TOPK), F32)

        def cells(ijs):
            vals = [t1[i] + t2[j] for i, j in ijs] + [pad_v] * (PEER_TOPK - len(ijs))
            ids = ([jnp.full((RK_G, LANES), float(i * PEER_TOPK + j), F32) for i, j in ijs]
                   + [pad_i] * (PEER_TOPK - len(ijs)))
            return vals, ids

        rest = [(i, j) for i, j in _CAND_CELLS if i >= 2]
        assert len(rest) == 2 * PEER_TOPK + 10 - PEER_TOPK
        row01 = _merge16(cells([(0, j) for j in range(16)]), cells([(1, j) for j in range(8)]))
        mid = _sort16(cells(rest[:PEER_TOPK]))
        tail = _sort16(cells(rest[PEER_TOPK:]))
        top, top_pos = _merge16(row01, _merge16(mid, tail))

        es = [jnp.exp(t - top[0]) for t in top]
        inv = 1.0 / _tree(jnp.add, es)
        for k in range(PEER_TOPK):
            pos = top_pos[k].astype(I32)
            ri = lax.shift_right_logical(pos, 4)
            rj = lax.bitwise_and(pos, PEER_TOPK - 1)
            a_sel = jnp.zeros((RK_G, LANES), F32)
            b_sel = jnp.zeros((RK_G, LANES), F32)
            for r in range(PEER_TOPK):
                a_sel = jnp.where(ri == r, ti_ref[0, r], a_sel)
                b_sel = jnp.where(rj == r, ti_ref[1, r], b_sel)
            rows = pl.ds(pl.multiple_of((h * PEER_TOPK + k) * RK_G, RK_G), RK_G)
            oa_ref[rows, :] = a_sel
            ob_ref[rows, :] = b_sel
            ow_ref[rows, :] = es[k] * inv
        return carry

    lax.fori_loop(0, PEER_HEADS, head, 0)
    nslot = PEER_HEADS * PEER_TOPK
    for g in range(RK_G):
        rows = pl.ds(g, nslot, stride=RK_G)
        tok = slice(g * LANES, (g + 1) * LANES)
        a_ref[tok, :] = oa_ref[rows, :].T.astype(I32)
        b_ref[tok, :] = ob_ref[rows, :].T.astype(I32)
        w_ref[tok, :] = ow_ref[rows, :].T


def _route(xn, wqt_b, k1_b, k2_b):
    n = xn.shape[0]
    tm = RK_TM
    nslot = PEER_HEADS * PEER_TOPK
    out = lambda: pl.BlockSpec((tm, nslot), lambda i: (i, 0))
    vregs = lambda *lead: pltpu.VMEM(lead + (RK_G, LANES), F32)
    tiles = lambda *lead: pltpu.VMEM(lead[:-1] + (lead[-1] * RK_G, LANES), F32)
    return pl.pallas_call(
        _route_kernel,
        grid=(n // tm,),
        in_specs=[pl.BlockSpec((tm, D_MODEL), lambda i: (i, 0)),
                  pl.BlockSpec((PEER_HEADS * PEER_QDIM, D_MODEL), lambda i: (0, 0)),
                  pl.BlockSpec((PEER_NKEYS, PEER_QDIM // 2), lambda i: (0, 0)),
                  pl.BlockSpec((PEER_NKEYS, PEER_QDIM // 2), lambda i: (0, 0))],
        out_specs=[out(), out(), out()],
        out_shape=[jax.ShapeDtypeStruct((n, nslot), I32),
                   jax.ShapeDtypeStruct((n, nslot), I32),
                   jax.ShapeDtypeStruct((n, nslot), F32)],
        scratch_shapes=[pltpu.VMEM((PEER_HEADS * PEER_QDIM, tm), F32),
                        tiles(2, PEER_NKEYS), vregs(2, PEER_TOPK), vregs(2, PEER_TOPK),
                        tiles(nslot), tiles(nslot), tiles(nslot),
                        vregs(PEER_NKEYS // PEER_TOPK, PEER_TOPK),
                        vregs(PEER_NKEYS // PEER_TOPK, PEER_TOPK)],
        compiler_params=_cparams(("parallel",)),
        name="peer_route",
    )(xn, wqt_b, k1_b, k2_b)


EXP_CHUNK = 2048
EXP_NCHUNK = PEER_NKEYS * PEER_NKEYS // EXP_CHUNK
EXPV_CHUNK = 4096


def _expu_kernel(xn_ref, u_ref, a_ref, b_ref, z_ref, buf0_ref, buf1_ref):
    tm = xn_ref.shape[0]
    i = pl.program_id(0)
    j = pl.program_id(1)
    nt = pl.num_programs(0) - 1
    assert EXP_NCHUNK % 2 == 0
    groups_per_chunk = EXP_CHUNK // PEER_NKEYS

    def matmul_into(buf_ref):
        buf_ref[...] = lax.dot_general(xn_ref[...], u_ref[...], (((1,), (1,)), ((), ())),
                                       preferred_element_type=F32)

    def gather_from(buf_ref):
        a0 = lax.rem(j + EXP_NCHUNK - 1, EXP_NCHUNK) * groups_per_chunk
        for g in range(tm // SUBLANES):
            rs = slice(g * SUBLANES, (g + 1) * SUBLANES)
            bi = b_ref[rs, :]
            ai = a_ref[rs, :] - a0
            acc = z_ref[rs, :]
            for t in range(groups_per_chunk):
                got = jnp.take_along_axis(buf_ref[rs, t * PEER_NKEYS:(t + 1) * PEER_NKEYS], bi,
                                          axis=1, mode="promise_in_bounds")
                acc = jnp.where(ai == t, got, acc)
            z_ref[rs, :] = acc

    odd = lax.rem(j, 2) == 1
    real = i < nt
    first = jnp.logical_and(i == 0, j == 0)

    @pl.when(first)
    def _():
        matmul_into(buf0_ref)

    @pl.when(jnp.logical_and(real, j == 1))
    def _():
        z_ref[...] = jnp.zeros_like(z_ref)

    @pl.when(jnp.logical_and(real, odd))
    def _():
        gather_from(buf0_ref)
        matmul_into(buf1_ref)

    @pl.when(jnp.logical_and(jnp.logical_and(real, jnp.logical_not(odd)), jnp.logical_not(first)))
    def _():
        gather_from(buf1_ref)
        matmul_into(buf0_ref)

    @pl.when(jnp.logical_and(i == nt, j == 0))
    def _():
        gather_from(buf1_ref)


def _expert_u(xn, u_b, aidx, bidx, tm=1024):
    n = xn.shape[0]
    nexp = u_b.shape[0]
    assert nexp == EXP_NCHUNK * EXP_CHUNK
    nslot = PEER_HEADS * PEER_TOPK
    nt = n // tm
    lag = lambda i, j: (jnp.clip(jnp.where(j == 0, i - 1, i), 0, nt - 1), 0)
    return pl.pallas_call(
        _expu_kernel,
        grid=(nt + 1, EXP_NCHUNK),
        in_specs=[pl.BlockSpec((tm, D_MODEL), lambda i, j: (jnp.minimum(i, nt - 1), 0)),
                  pl.BlockSpec((EXP_CHUNK, D_MODEL), lambda i, j: (jnp.where(i == nt, 0, j), 0)),
                  pl.BlockSpec((tm, nslot), lag),
                  pl.BlockSpec((tm, nslot), lag)],
        out_specs=pl.BlockSpec((tm, nslot), lag),
        out_shape=jax.ShapeDtypeStruct((n, nslot), F32),
        scratch_shapes=[pltpu.VMEM((tm, EXP_CHUNK), F32), pltpu.VMEM((tm, EXP_CHUNK), F32)],
        compiler_params=_cparams(("arbitrary", "arbitrary")),
        name="peer_expert_u",
    )(xn, u_b, aidx, bidx)


SLAB_PAD = SUBLANES
SLAB_TOKENS = 256
U32 = jnp.uint32


def _bf16_bits(x):
    return pltpu.bitcast(x.astype(BF16).astype(F32), U32)


def _expv_kernel(z_ref, w_ref, a_ref, b_ref, v_ref, h_ref, p_ref, gp_ref, wg_ref, wp_ref, gf_ref,
                 o_ref, s_ref, acc_ref, c_ref):
    tm = z_ref.shape[0]
    nk = PEER_NKEYS
    pitch = tm // 2 + SLAB_PAD
    j = pl.program_id(1)

    @pl.when(j == 0)
    def _():
        c_ref[...] = w_ref[...] * _gelu(z_ref[...])
        sub = lax.broadcasted_iota(I32, (nk, nk), 0)
        dn = (((1,), (1,)), ((), ()))

        def slab(n):
            arow = a_ref[pl.ds(n, 1), :]
            brow = b_ref[pl.ds(n, 1), :]
            crow = c_ref[pl.ds(n, 1), :]
            oat = jnp.where(arow == sub, 1.0, 0.0).astype(BF16)
            wobt = jnp.where(brow == sub, crow, 0.0).astype(BF16)
            return lax.dot_general(oat, wobt, dn, preferred_element_type=F32)

        def toks(g, carry):
            base = g * (SLAB_TOKENS // 2)
            for t in range(SLAB_TOKENS // 2):
                row = base + t
                lo = lax.shift_right_logical(_bf16_bits(slab(2 * row)), jnp.uint32(16))
                hi = _bf16_bits(slab(2 * row + 1))
                s_ref[pl.ds(row, nk, stride=pitch), :] = lo | hi
            return carry

        lax.fori_loop(0, tm // SLAB_TOKENS, toks, 0)

    groups = EXPV_CHUNK // nk
    slabs = []
    for t in range(groups):
        r0 = pl.multiple_of((j * groups + t) * pitch, SUBLANES)
        slabs.append(pltpu.bitcast(s_ref[pl.ds(r0, tm // 2), :], BF16))
    part = jnp.dot(jnp.concatenate(slabs, axis=1), v_ref[...], preferred_element_type=F32)

    @pl.when(j == 0)
    def _():
        acc_ref[...] = part

    @pl.when(j > 0)
    def _():
        acc_ref[...] += part

    @pl.when(j == pl.num_programs(1) - 1)
    def _():
        h = h_ref[...] + acc_ref[...]
        xn = _rms(h, gp_ref[...]).astype(BF16)
        gate = jax.nn.sigmoid(jnp.dot(xn, wg_ref[...], preferred_element_type=F32))
        proj = jnp.dot(p_ref[...].astype(BF16), wp_ref[...], preferred_element_type=F32)
        o_ref[...] = _rms(h + gate * proj, gf_ref[...])


def _expert_v(zsel, w, aidx, bidx, v_b, h1, p2, g_ple, wg_b, wp_b, g_final, tm=512):
    n = zsel.shape[0]
    nexp = v_b.shape[0]
    nslot = PEER_HEADS * PEER_TOPK
    slot = lambda: pl.BlockSpec((tm, nslot), lambda i, j: (i, 0))
    const = lambda r, c: pl.BlockSpec((r, c), lambda i, j: (0, 0))
    return pl.pallas_call(
        _expv_kernel,
        grid=(n // tm, nexp // EXPV_CHUNK),
        in_specs=[slot(), slot(), slot(), slot(),
                  pl.BlockSpec((EXPV_CHUNK, D_MODEL), lambda i, j: (j, 0)),
                  pl.BlockSpec((tm, D_MODEL), lambda i, j: (i, 0)),
                  pl.BlockSpec((tm, PLE_DIM), lambda i, j: (i, 0)),
                  const(1, D_MODEL), const(D_MODEL, D_MODEL), const(PLE_DIM, D_MODEL),
                  const(1, D_MODEL)],
        out_specs=pl.BlockSpec((tm, D_MODEL), lambda i, j: (i, 0)),
        out_shape=jax.ShapeDtypeStruct((n, D_MODEL), F32),
        scratch_shapes=[pltpu.VMEM((PEER_NKEYS * (tm // 2 + SLAB_PAD), PEER_NKEYS), U32),
                        pltpu.VMEM((tm, D_MODEL), F32),
                        pltpu.VMEM((tm, nslot), F32)],
        compiler_params=_cparams(("parallel", "arbitrary")),
        name="peer_expert_v",
    )(zsel, w, aidx, bidx, v_b, h1, p2, g_ple, wg_b, wp_b, g_final)


def _block_diag(w):
    g, d, _ = w.shape
    eye = jnp.eye(g, dtype=w.dtype)
    return (eye[:, None, :, None] * w[:, :, None, :]).reshape(g * d, g * d)


def kernel(x, p, g_mix, w_in, conv_w, conv_b, w_rg, b_rg, w_ig, b_ig, lru_lambda, g_lru_out,
           g_attn_out, w_out, rel_bias, g_ffn, peer_wq, peer_k1, peer_k2, peer_u, peer_v,
           g_ple, w_ple_gate, w_ple_proj, g_final):
    batch, seq, dm = x.shape
    n = batch * seq
    assert w_in.shape[0] == 1, "single-layer problem: the last kernel fuses the final norm"
    row = lambda v: v.reshape(1, -1).astype(F32)
    btab = _bias_tables(rel_bias)
    h = x.reshape(n, dm)
    for i in range(1):
        xl, gate, q, k, v = _inproj(h, row(g_mix[i]), w_in[i].astype(BF16))
        y_lru = _lru(xl, gate, conv_w[i], row(conv_b[i]),
                     _block_diag(w_rg[i]).astype(BF16), row(b_rg[i]),
                     _block_diag(w_ig[i]).astype(BF16), row(b_ig[i]),
                     row(lru_lambda[i]), row(g_lru_out[i]), batch, seq)
        y_att = _attention(q, k, v, btab, batch, seq)
        h1, xn = _outproj(y_lru, y_att, h, row(g_attn_out[i]), w_out[i].astype(BF16),
                          row(g_ffn[i]))
        aidx, bidx, gates = _route(xn, peer_wq[i].T.astype(BF16), peer_k1[i].astype(BF16),
                                   peer_k2[i].astype(BF16))
        zsel = _expert_u(xn, peer_u[i].astype(BF16), aidx, bidx)
        h = _expert_v(zsel, gates, aidx, bidx, peer_v[i].astype(BF16), h1, p[i].reshape(n, -1),
                      row(g_ple[i]), w_ple_gate[i].astype(BF16), w_ple_proj[i].astype(BF16),
                      g_final.reshape(1, -1))
    return h.reshape(batch, seq, dm)
```
